```python
import jax, jax.numpy as jnp
from jax import lax
import numpy as np

D_MODEL = 1024
BATCH = 8
SEQ = 8192
DEPTH = 1
DEC_BATCH = 32
DEC_SEQ = 16
PAST_LEN = 1024

CHUNK = 64
Q_BLOCK = 128
HEAD_DIM = 64
SB_HEADS = 8
FOX_HEADS = 8
SB_WIDTH = SB_HEADS * HEAD_DIM
FOX_WIDTH = FOX_HEADS * HEAD_DIM
IN_WIDTH = 3 * SB_WIDTH + 3 * FOX_WIDTH + FOX_HEADS
N_GROUPS = 4
EXPERTS_PER_GROUP = 4
N_EXPERTS = N_GROUPS * EXPERTS_PER_GROUP
TOP_K = 2
EXPERT_HIDDEN = 256
N_MOD = 6
RMS_EPS = 1e-6

kernel_name = 'stickbreak_fox_hmoe_stream_step'


def rmsnorm(x, g):
    xf = x.astype(jnp.float32)
    y = xf * lax.rsqrt(jnp.mean(xf * xf, axis=-1, keepdims=True) + RMS_EPS)
    return (y * g.astype(jnp.float32)).astype(x.dtype)


def to_heads(a, n_heads):
    b, t, _ = a.shape
    return a.reshape(b, t, n_heads, HEAD_DIM).transpose(0, 2, 1, 3)


def from_heads(a):
    b, h, t, d = a.shape
    return a.transpose(0, 2, 1, 3).reshape(b, t, h * d)


def sb_attend(q_pos, q, k, v, k_pos):
    z = jnp.einsum('bhqd,bhkd->bhqk', q.astype(jnp.float32), k.astype(jnp.float32)) * (HEAD_DIM ** -0.5)
    mask = k_pos[None, :] < q_pos[:, None]
    log_skip = jnp.where(mask, jax.nn.log_sigmoid(-z), 0.0)
    later = lax.cumsum(log_skip, axis=3, reverse=True) - log_skip
    w = jnp.where(mask, jnp.exp(jax.nn.log_sigmoid(z) + later), 0.0)
    return jnp.einsum('bhqk,bhkd->bhqd', w, v.astype(jnp.float32)).astype(v.dtype)


def fox_attend(q_pos, q, fq, k, v, k_pos, fk):
    z = jnp.einsum('bhqd,bhkd->bhqk', q.astype(jnp.float32), k.astype(jnp.float32)) * (HEAD_DIM ** -0.5)
    z = z + fq[..., :, None] - fk[..., None, :]
    z = jnp.where(k_pos[None, :] <= q_pos[:, None], z, -jnp.inf)
    p = jax.nn.softmax(z, axis=-1)
    return jnp.einsum('bhqk,bhkd->bhqd', p, v.astype(jnp.float32)).astype(v.dtype)


def sweep_queries(attend, q_pos, *q_side):
    t = q_pos.shape[0]
    if t % Q_BLOCK != 0 or t <= Q_BLOCK:
        return attend(q_pos, *q_side)
    nb = t // Q_BLOCK

    def split(a):
        a = a.reshape(a.shape[:2] + (nb, Q_BLOCK) + a.shape[3:])
        return jnp.moveaxis(a, 2, 0)

    out = lax.map(lambda blk: attend(blk[0], *blk[1]),
                  (q_pos.reshape(nb, Q_BLOCK), tuple(split(a) for a in q_side)))
    out = jnp.moveaxis(out, 0, 2)
    return out.reshape(out.shape[:2] + (t,) + out.shape[4:])


def hier_moe(h, w_rg, b_rg, w_re, b_re, w1, w3, w2):
    b, t, d = h.shape
    tok = h.reshape(b * t, d)
    g_prob = jax.nn.softmax((tok @ w_rg + b_rg).astype(jnp.float32), axis=-1)
    g_idx = jnp.argmax(g_prob, axis=-1)
    g_w = jnp.take_along_axis(g_prob, g_idx[:, None], axis=1)
    e_logits = (tok @ w_re + b_re).astype(jnp.float32).reshape(-1, N_GROUPS, EXPERTS_PER_GROUP)
    e_sel = jnp.take_along_axis(e_logits, g_idx[:, None, None], axis=1)[:, 0]
    e_prob = jax.nn.softmax(e_sel, axis=-1)
    top_w, top_i = lax.top_k(e_prob, TOP_K)
    top_w = top_w / jnp.sum(top_w, axis=-1, keepdims=True)
    expert_id = g_idx[:, None] * EXPERTS_PER_GROUP + top_i
    combine = jnp.sum(jax.nn.one_hot(expert_id, N_EXPERTS, dtype=jnp.float32)
                      * (g_w * top_w)[..., None], axis=1).astype(h.dtype)
    out = jnp.zeros_like(tok)
    for e in range(N_EXPERTS):
        act = jax.nn.silu(tok @ w1[e]) * (tok @ w3[e])
        out = out + combine[:, e:e + 1] * (act @ w2[e])
    return out.reshape(b, t, d)


def trunk_layer(x, c, past_sb_k, past_sb_v, past_fox_k, past_fox_v, past_fox_logf,
                w_ada, b_ada, g_mix, w_in, b_f, w_ba, w_bb, w_gate, b_gate, w_out,
                g_moe, w_rg, b_rg, w_re, b_re, w1, w3, w2):
    t = x.shape[1]
    p_len = past_sb_k.shape[2]
    ada = (jax.nn.silu(c) @ w_ada + b_ada)[:, None, :]
    shift1, scale1, gate1, shift2, scale2, gate2 = jnp.split(ada, N_MOD, axis=-1)

    h = rmsnorm(x, g_mix) * (1.0 + scale1) + shift1
    proj = h @ w_in
    cuts = [SB_WIDTH, 2 * SB_WIDTH, 3 * SB_WIDTH, 3 * SB_WIDTH + FOX_WIDTH,
            3 * SB_WIDTH + 2 * FOX_WIDTH, 3 * SB_WIDTH + 3 * FOX_WIDTH]
    qa, ka, va, qb, kb, vb, fg = jnp.split(proj, cuts, axis=-1)
    qa, ka, va = to_heads(qa, SB_HEADS), to_heads(ka, SB_HEADS), to_heads(va, SB_HEADS)
    qb, kb, vb = to_heads(qb, FOX_HEADS), to_heads(kb, FOX_HEADS), to_heads(vb, FOX_HEADS)
    logf = jax.nn.log_sigmoid((fg + b_f).astype(jnp.float32)).transpose(0, 2, 1)

    q_pos = p_len + jnp.arange(t)
    k_pos = jnp.arange(p_len + t)
    k_sb = jnp.concatenate([past_sb_k.astype(ka.dtype), ka], axis=2)
    v_sb = jnp.concatenate([past_sb_v.astype(va.dtype), va], axis=2)
    k_fx = jnp.concatenate([past_fox_k.astype(kb.dtype), kb], axis=2)
    v_fx = jnp.concatenate([past_fox_v.astype(vb.dtype), vb], axis=2)
    f_all = jnp.cumsum(jnp.concatenate([past_fox_logf.astype(jnp.float32), logf], axis=2), axis=2)
    f_q = f_all[:, :, p_len:]

    ya = sweep_queries(lambda qp, qq: sb_attend(qp, qq, k_sb, v_sb, k_pos), q_pos, qa)
    yb = sweep_queries(lambda qp, qq, ff: fox_attend(qp, qq, ff, k_fx, v_fx, k_pos, f_all), q_pos, qb, f_q)
    ua = from_heads(ya) @ w_ba
    ub = from_heads(yb) @ w_bb
    ga, gb = jnp.split(jax.nn.sigmoid(h @ w_gate + b_gate), 2, axis=-1)
    x = x + gate1 * ((ga * ua + gb * ub) @ w_out)

    h2 = rmsnorm(x, g_moe) * (1.0 + scale2) + shift2
    x = x + gate2 * hier_moe(h2, w_rg, b_rg, w_re, b_re, w1, w3, w2)
    return x, ka, va, kb, vb, logf


def setup_inputs(seed: int = 0) -> dict:
    key = jax.random.key(seed)
    ks = jax.random.split(key, 32)
    f32 = jnp.float32

    def nrm(k, shape, scale):
        return jax.random.normal(k, shape, f32) * scale

    d = D_MODEL
    return {
        'x_prompt': nrm(ks[0], (BATCH, SEQ, d), 1.0),
        'x_sample': nrm(ks[1], (DEC_BATCH, DEC_SEQ, d), 1.0),
        'cache_sb_k': nrm(ks[2], (DEPTH, DEC_BATCH, SB_HEADS, PAST_LEN, HEAD_DIM), 1.0),
        'cache_sb_v': nrm(ks[3], (DEPTH, DEC_BATCH, SB_HEADS, PAST_LEN, HEAD_DIM), 1.0),
        'cache_fox_k': nrm(ks[4], (DEPTH, DEC_BATCH, FOX_HEADS, PAST_LEN, HEAD_DIM), 1.0),
        'cache_fox_v': nrm(ks[5], (DEPTH, DEC_BATCH, FOX_HEADS, PAST_LEN, HEAD_DIM), 1.0),
        'cache_fox_logf': jax.nn.log_sigmoid(2.0 + nrm(ks[6], (DEPTH, DEC_BATCH, FOX_HEADS, PAST_LEN), 1.0)),
        'c_prompt': nrm(ks[7], (BATCH, d), 1.0),
        'c_sample': nrm(ks[8], (DEC_BATCH, d), 1.0),
        'w_ada': nrm(ks[9], (DEPTH, d, N_MOD * d), d ** -0.5),
        'b_ada': nrm(ks[10], (DEPTH, N_MOD * d), 0.01),
        'g_mix': 1.0 + nrm(ks[11], (DEPTH, d), 0.02),
        'w_in': nrm(ks[12], (DEPTH, d, IN_WIDTH), d ** -0.5),
        'b_f': 2.0 + nrm(ks[13], (DEPTH, FOX_HEADS), 0.1),
        'w_ba': nrm(ks[14], (DEPTH, SB_WIDTH, d), SB_WIDTH ** -0.5),
        'w_bb': nrm(ks[15], (DEPTH, FOX_WIDTH, d), FOX_WIDTH ** -0.5),
        'w_gate': nrm(ks[16], (DEPTH, d, 2 * d), d ** -0.5),
        'b_gate': nrm(ks[17], (DEPTH, 2 * d), 0.01),
        'w_out': nrm(ks[18], (DEPTH, d, d), d ** -0.5),
        'g_moe': 1.0 + nrm(ks[19], (DEPTH, d), 0.02),
        'w_rg': nrm(ks[20], (DEPTH, d, N_GROUPS), d ** -0.5),
        'b_rg': nrm(ks[21], (DEPTH, N_GROUPS), 0.01),
        'w_re': nrm(ks[22], (DEPTH, d, N_EXPERTS), d ** -0.5),
        'b_re': nrm(ks[23], (DEPTH, N_EXPERTS), 0.01),
        'w1': nrm(ks[24], (DEPTH, N_EXPERTS, d, EXPERT_HIDDEN), d ** -0.5),
        'w3': nrm(ks[25], (DEPTH, N_EXPERTS, d, EXPERT_HIDDEN), d ** -0.5),
        'w2': nrm(ks[26], (DEPTH, N_EXPERTS, EXPERT_HIDDEN, d), EXPERT_HIDDEN ** -0.5),
        'g_final': 1.0 + nrm(ks[27], (d,), 0.02),
    }


def reference(x_prompt, x_sample, cache_sb_k, cache_sb_v, cache_fox_k, cache_fox_v, cache_fox_logf,
              c_prompt, c_sample, w_ada, b_ada, g_mix, w_in, b_f, w_ba, w_bb, w_gate, b_gate, w_out,
              g_moe, w_rg, b_rg, w_re, b_re, w1, w3, w2, g_final):
    xp, xs = x_prompt, x_sample
    psk, psv, pfk, pfv, pfl = [], [], [], [], []
    ssk, ssv, sfk, sfv, sfl = [], [], [], [], []
    for l in range(DEPTH):
        weights = (w_ada[l], b_ada[l], g_mix[l], w_in[l], b_f[l], w_ba[l], w_bb[l], w_gate[l], b_gate[l],
                   w_out[l], g_moe[l], w_rg[l], b_rg[l], w_re[l], b_re[l], w1[l], w3[l], w2[l])
        empty_sb = jnp.zeros((xp.shape[0], SB_HEADS, 0, HEAD_DIM), xp.dtype)
        empty_fx = jnp.zeros((xp.shape[0], FOX_HEADS, 0, HEAD_DIM), xp.dtype)
        empty_lf = jnp.zeros((xp.shape[0], FOX_HEADS, 0), jnp.float32)
        xp, ka, va, kb, vb, lf = trunk_layer(xp, c_prompt, empty_sb, empty_sb, empty_fx, empty_fx, empty_lf, *weights)
        psk.append(ka); psv.append(va); pfk.append(kb); pfv.append(vb); pfl.append(lf)
        xs, ka, va, kb, vb, lf = trunk_layer(xs, c_sample, cache_sb_k[l], cache_sb_v[l], cache_fox_k[l],
                                             cache_fox_v[l], cache_fox_logf[l], *weights)
        ssk.append(ka); ssv.append(va); sfk.append(kb); sfv.append(vb); sfl.append(lf)
    y_prompt = rmsnorm(xp, g_final)
    y_sample = rmsnorm(xs, g_final)
    return (y_prompt, y_sample,
            jnp.stack(psk), jnp.stack(psv), jnp.stack(pfk), jnp.stack(pfv), jnp.stack(pfl),
            jnp.stack(ssk), jnp.stack(ssv), jnp.stack(sfk), jnp.stack(sfv), jnp.stack(sfl))
```

```python
import functools

import jax
import jax.numpy as jnp
from jax import lax
from jax.experimental import pallas as pl
from jax.experimental.pallas import tpu as pltpu

F32 = jnp.float32
BF16 = jnp.bfloat16

HEAD_DIM = 64
RMS_EPS = 1e-6
N_MOD = 6
LANES = 128
NEG_BIG = -1e30

PRE_TILE = 256
Q_TILE = 512
POST_TILE = 512
MOE_TILE = 1024
VMEM_LIMIT = 56 * 1024 * 1024


def _cparams(sem):
    return pltpu.CompilerParams(dimension_semantics=sem, vmem_limit_bytes=VMEM_LIMIT)


def _log_sigmoid(x):
    return jnp.minimum(x, 0.0) - jnp.log1p(jnp.exp(-jnp.abs(x)))


def _rms_mod(x, g, scale, shift):
    ms = jnp.mean(x * x, axis=-1, keepdims=True)
    y = x * lax.rsqrt(ms + RMS_EPS)
    return (y * g) * (1.0 + scale) + shift


def _mod(mods_ref, i, per_token):
    return mods_ref[i] if per_token else mods_ref[0, i:i + 1, :]


def _dot(a, b):
    return jnp.dot(a, b, preferred_element_type=F32)


def _split3(f):
    hi = f.astype(BF16).astype(F32)
    r = f - hi
    mid = r.astype(BF16).astype(F32)
    lo = (r - mid).astype(BF16).astype(F32)
    return hi, mid, lo


def _lane_cumsum(x):
    n = x.shape[1]
    lane = lax.broadcasted_iota(jnp.int32, x.shape, 1)
    d = 1
    while d < n:
        x = x + jnp.where(lane >= d, pltpu.roll(x, d, axis=1), 0.0)
        d *= 2
    return x


def _ada_kernel(c_ref, w_ref, b_ref, o_ref):
    c = c_ref[...]
    s = c * jax.nn.sigmoid(c)
    o_ref[...] = jnp.dot(s, w_ref[...], preferred_element_type=F32,
                         precision=lax.Precision.HIGHEST) + b_ref[...]


def _ada(c_all, w_ada, b_ada):
    n, d = c_all.shape
    nout = w_ada.shape[1]
    tn = 1024
    return pl.pallas_call(
        _ada_kernel,
        grid=(nout // tn,),
        in_specs=[pl.BlockSpec((n, d), lambda j: (0, 0)),
                  pl.BlockSpec((d, tn), lambda j: (0, j)),
                  pl.BlockSpec((1, tn), lambda j: (0, j))],
        out_specs=pl.BlockSpec((n, tn), lambda j: (0, j)),
        out_shape=jax.ShapeDtypeStruct((n, nout), F32),
        compiler_params=_cparams(("arbitrary",)),
        name="ada",
    )(c_all, w_ada, b_ada.reshape(1, nout))


def _pre_core(x, mods_ref, gmix_ref, win_ref, wg_ref, bg_ref, per_token):
    h = _rms_mod(x, gmix_ref[...], _mod(mods_ref, 1, per_token), _mod(mods_ref, 0, per_token))
    hb = h.astype(BF16)
    proj = _dot(hb, win_ref[...])
    gates = jax.nn.sigmoid(_dot(hb, wg_ref[...]) + bg_ref[...])
    return proj, gates.astype(BF16)


def _pre_prompt_kernel(x_ref, mods_ref, gmix_ref, win_ref, bf_ref, wg_ref, bg_ref,
                       qsT_ref, ks_ref, vsT_ref, qfT_ref, kf_ref, vfT_ref,
                       ksl_ref, vsl_ref, kfl_ref, vfl_ref, logf_ref, gates_ref,
                       carry_ref, *, n_heads):
    tm = x_ref.shape[1]
    w = n_heads * HEAD_DIM

    @pl.when(pl.program_id(1) == 0)
    def _():
        carry_ref[...] = jnp.zeros_like(carry_ref)

    proj, gates = _pre_core(x_ref[0], mods_ref, gmix_ref, win_ref, wg_ref, bg_ref, False)
    gates_ref[0] = gates
    qa, ka, va, qb, kb, vb = [proj[:, i * w:(i + 1) * w] for i in range(6)]
    fg = proj[:, 6 * w:6 * w + LANES]
    scale = HEAD_DIM ** -0.5
    qaT = (qa * scale).T
    vaT = va.T
    qbT = (qb * scale).T
    vbT = vb.T

    logfT = _log_sigmoid(fg + bf_ref[...]).T[0:n_heads, :]
    logf_ref[0] = logfT
    f = _lane_cumsum(logfT) + carry_ref[:, 0:1]
    carry_ref[...] = jnp.broadcast_to(f[:, tm - 1:tm], carry_ref.shape)
    f_hi, f_mid, f_lo = _split3(f)

    row64 = lax.broadcasted_iota(jnp.int32, (HEAD_DIM, tm), 0)
    row8 = lax.broadcasted_iota(jnp.int32, (8, tm), 0)
    zeros64 = jnp.zeros((HEAD_DIM, tm), BF16)
    ke_parts = []
    for h in range(n_heads):
        def bc(a, n):
            return jnp.broadcast_to(a[h:h + 1, :], (n, tm))
        qe = jnp.where(row64 < 3, 1.0,
                       jnp.where(row64 == 3, bc(f_hi, HEAD_DIM),
                                 jnp.where(row64 == 4, bc(f_mid, HEAD_DIM),
                                           jnp.where(row64 == 5, bc(f_lo, HEAD_DIM), 0.0))))
        sl = slice(h * HEAD_DIM, (h + 1) * HEAD_DIM)
        qfT_ref[0, h, 0:HEAD_DIM, :] = qbT[sl, :].astype(BF16)
        qfT_ref[0, h, HEAD_DIM:2 * HEAD_DIM, :] = qe.astype(BF16)
        qsT_ref[0, h, 0:HEAD_DIM, :] = qaT[sl, :].astype(BF16)
        qsT_ref[0, h, HEAD_DIM:2 * HEAD_DIM, :] = zeros64
        vsT_ref[0, h, 0] = vaT[sl, :].astype(BF16)
        vfT_ref[0, h, 0] = vbT[sl, :].astype(BF16)
        ksl_ref[0, h] = ka[:, sl]
        vsl_ref[0, h] = va[:, sl]
        kfl_ref[0, h] = kb[:, sl]
        vfl_ref[0, h] = vb[:, sl]
        ke_parts.append(
            jnp.where(row8 == 0, -bc(f_hi, 8),
                      jnp.where(row8 == 1, -bc(f_mid, 8),
                                jnp.where(row8 == 2, -bc(f_lo, 8),
                                          jnp.where(row8 < 6, 1.0, 0.0)))))
    ke_parts.append(jnp.zeros((LANES - 8 * n_heads, tm), F32))
    ke = jnp.concatenate(ke_parts, axis=0).T

    lane = lax.broadcasted_iota(jnp.int32, (tm, LANES), 1)
    for h in range(n_heads):
        base = (h // 2) * LANES
        ka_slab = ka[:, base:base + LANES]
        kb_slab = kb[:, base:base + LANES]
        if h % 2:
            ka_slab = pltpu.roll(ka_slab, HEAD_DIM, axis=1)
            kb_slab = pltpu.roll(kb_slab, HEAD_DIM, axis=1)
        ext = pltpu.roll(ke, HEAD_DIM - 8 * h, axis=1)
        ks_ref[0, h] = jnp.where(lane < HEAD_DIM, ka_slab, 0.0).astype(BF16)
        kf_ref[0, h] = jnp.where(lane < HEAD_DIM, kb_slab,
                                 jnp.where(lane < HEAD_DIM + 8, ext, 0.0)).astype(BF16)


def _pre_prompt(x, mods, g_mix, w_in_p, b_f_p, w_gate, b_gate, n_heads):
    b, t, d = x.shape
    tm = PRE_TILE
    nt = t // tm
    h, hd = n_heads, HEAD_DIM
    const2 = lambda bi, ti: (0, 0)
    head_t = lambda bi, ti: (bi, 0, 0, ti)
    head_s = lambda bi, ti: (bi, 0, ti, 0)
    blk_t = lambda bi, ti: (bi, 0, ti, 0, 0)
    out_shape = (
        jax.ShapeDtypeStruct((b, h, 2 * hd, t), BF16),
        jax.ShapeDtypeStruct((b, h, t, 2 * hd), BF16),
        jax.ShapeDtypeStruct((b, h, nt, hd, tm), BF16),
        jax.ShapeDtypeStruct((b, h, 2 * hd, t), BF16),
        jax.ShapeDtypeStruct((b, h, t, 2 * hd), BF16),
        jax.ShapeDtypeStruct((b, h, nt, hd, tm), BF16),
        jax.ShapeDtypeStruct((b, h, t, hd), F32),
        jax.ShapeDtypeStruct((b, h, t, hd), F32),
        jax.ShapeDtypeStruct((b, h, t, hd), F32),
        jax.ShapeDtypeStruct((b, h, t, hd), F32),
        jax.ShapeDtypeStruct((b, h, t), F32),
        jax.ShapeDtypeStruct((b, t, w_gate.shape[1]), BF16),
    )
    out_specs = (
        pl.BlockSpec((1, h, 2 * hd, tm), head_t),
        pl.BlockSpec((1, h, tm, 2 * hd), head_s),
        pl.BlockSpec((1, h, 1, hd, tm), blk_t),
        pl.BlockSpec((1, h, 2 * hd, tm), head_t),
        pl.BlockSpec((1, h, tm, 2 * hd), head_s),
        pl.BlockSpec((1, h, 1, hd, tm), blk_t),
        pl.BlockSpec((1, h, tm, hd), head_s),
        pl.BlockSpec((1, h, tm, hd), head_s),
        pl.BlockSpec((1, h, tm, hd), head_s),
        pl.BlockSpec((1, h, tm, hd), head_s),
        pl.BlockSpec((1, h, tm), lambda bi, ti: (bi, 0, ti)),
        pl.BlockSpec((1, tm, w_gate.shape[1]), lambda bi, ti: (bi, ti, 0)),
    )
    in_specs = [
        pl.BlockSpec((1, tm, d), lambda bi, ti: (bi, ti, 0)),
        pl.BlockSpec((1, N_MOD, d), lambda bi, ti: (bi, 0, 0)),
        pl.BlockSpec((1, d), const2),
        pl.BlockSpec(w_in_p.shape, const2),
        pl.BlockSpec((1, LANES), const2),
        pl.BlockSpec(w_gate.shape, const2),
        pl.BlockSpec((1, w_gate.shape[1]), const2),
    ]
    return pl.pallas_call(
        functools.partial(_pre_prompt_kernel, n_heads=n_heads),
        grid=(b, nt),
        in_specs=in_specs,
        out_specs=out_specs,
        out_shape=out_shape,
        scratch_shapes=[pltpu.VMEM((h, LANES), F32)],
        compiler_params=_cparams(("arbitrary", "arbitrary")),
        name="pre_prompt",
    )(x, mods, g_mix, w_in_p, b_f_p, w_gate, b_gate)


def _pre_sample_kernel(x_ref, mods_ref, gmix_ref, win_ref, bf_ref, wg_ref, bg_ref,
                       proj_ref, logf_ref, gates_ref, *, n_heads):
    w = n_heads * HEAD_DIM
    proj, gates = _pre_core(x_ref[...], mods_ref, gmix_ref, win_ref, wg_ref, bg_ref, True)
    proj_ref[...] = proj
    gates_ref[...] = gates
    logf_ref[...] = _log_sigmoid(proj[:, 6 * w:6 * w + LANES] + bf_ref[...])


def _pre_sample(x, mods_tok, g_mix, w_in_p, b_f_p, w_gate, b_gate, n_heads):
    n, d = x.shape
    full = lambda a: pl.BlockSpec(a.shape, lambda i: (0,) * a.ndim)
    args = (x, mods_tok, g_mix, w_in_p, b_f_p, w_gate, b_gate)
    out_shape = (jax.ShapeDtypeStruct((n, w_in_p.shape[1]), F32),
                 jax.ShapeDtypeStruct((n, LANES), F32),
                 jax.ShapeDtypeStruct((n, w_gate.shape[1]), BF16))
    return pl.pallas_call(
        functools.partial(_pre_sample_kernel, n_heads=n_heads),
        grid=(1,),
        in_specs=[full(a) for a in args],
        out_specs=tuple(pl.BlockSpec(s.shape, lambda i: (0, 0)) for s in out_shape),
        out_shape=out_shape,
        compiler_params=_cparams(("arbitrary",)),
        name="pre_sample",
    )(*args)


def _sb_prompt_kernel(qT_ref, k_ref, vT_ref, tri_ref, o_ref):
    tq = qT_ref.shape[3]
    tk = vT_ref.shape[4]
    r = tq // tk
    qi = pl.program_id(2)
    qT = qT_ref[0, 0]
    tri = tri_ref[...]
    row = lax.broadcasted_iota(jnp.int32, (tk, tq), 0)
    col = lax.broadcasted_iota(jnp.int32, (tk, tq), 1)

    def block(j, carry, rel0):
        c, acc = carry
        k = k_ref[0, 0, pl.ds(pl.multiple_of(j * tk, tk), tk), :]
        vT = vT_ref[0, 0, j]
        z = _dot(k, qT)
        ls_full = -(jnp.maximum(z, 0.0) + jnp.log1p(jnp.exp(-jnp.abs(z))))
        if rel0 is None:
            ls = ls_full
        else:
            mask = (row + rel0) < col
            ls = jnp.where(mask, ls_full, 0.0)
        later = _dot(tri, ls.astype(BF16)) + c
        wgt = jnp.exp(z + ls_full + later)
        if rel0 is not None:
            wgt = jnp.where(mask, wgt, 0.0)
        acc = acc + _dot(vT, wgt.astype(BF16))
        c = c + jnp.sum(ls, axis=0, keepdims=True)
        return c, acc

    carry = (jnp.zeros((1, tq), F32), jnp.zeros((HEAD_DIM, tq), F32))
    for i in reversed(range(r)):
        carry = block(qi * r + i, carry, i * tk)
    n_un = qi * r
    carry = lax.fori_loop(0, n_un, lambda i, cr: block(n_un - 1 - i, cr, None), carry)
    o_ref[0, 0] = carry[1].astype(BF16)


def _fox_prompt_kernel(qT_ref, k_ref, vT_ref, o_ref):
    tq = qT_ref.shape[3]
    tk = vT_ref.shape[4]
    r = tq // tk
    qi = pl.program_id(2)
    qT = qT_ref[0, 0]
    row = lax.broadcasted_iota(jnp.int32, (tk, tq), 0)
    col = lax.broadcasted_iota(jnp.int32, (tk, tq), 1)

    def block(j, carry, rel0):
        m, l, acc = carry
        k = k_ref[0, 0, pl.ds(pl.multiple_of(j * tk, tk), tk), :]
        vT = vT_ref[0, 0, j]
        s = _dot(k, qT)
        if rel0 is not None:
            s = jnp.where((row + rel0) <= col, s, NEG_BIG)
        m_new = jnp.maximum(m, jnp.max(s, axis=0, keepdims=True))
        alpha = jnp.exp(m - m_new)
        p = jnp.exp(s - m_new)
        l = alpha * l + jnp.sum(p, axis=0, keepdims=True)
        acc = alpha * acc + _dot(vT, p.astype(BF16))
        return m_new, l, acc

    carry = (jnp.full((1, tq), NEG_BIG, F32), jnp.zeros((1, tq), F32), jnp.zeros((HEAD_DIM, tq), F32))
    carry = lax.fori_loop(0, qi * r, lambda j, cr: block(j, cr, None), carry)
    for i in range(r):
        carry = block(qi * r + i, carry, i * tk)
    _, l, acc = carry
    o_ref[0, 0] = (acc / l).astype(BF16)


def _attn_prompt(kernel, qT, k, vT, extra=()):
    b, h, kd, t = qT.shape
    nk, tk = vT.shape[2], vT.shape[4]
    tq = Q_TILE
    in_specs = [
        pl.BlockSpec((1, 1, kd, tq), lambda bi, hi, qi: (bi, hi, 0, qi)),
        pl.BlockSpec((1, 1, t, kd), lambda bi, hi, qi: (bi, hi, 0, 0)),
        pl.BlockSpec((1, 1, nk, HEAD_DIM, tk), lambda bi, hi, qi: (bi, hi, 0, 0, 0)),
    ] + [pl.BlockSpec(a.shape, lambda bi, hi, qi: (0, 0)) for a in extra]
    return pl.pallas_call(
        kernel,
        grid=(b, h, t // tq),
        in_specs=in_specs,
        out_specs=pl.BlockSpec((1, 1, HEAD_DIM, tq), lambda bi, hi, qi: (bi, hi, 0, qi)),
        out_shape=jax.ShapeDtypeStruct((b, h, HEAD_DIM, t), BF16),
        compiler_params=_cparams(("parallel", "parallel", "arbitrary")),
        name=kernel.__name__.strip("_"),
    )(qT, k, vT, *extra)


def _sample_kv(ck_ref, kn_ref, cv_ref, vn_ref, h):
    k = jnp.concatenate([ck_ref[0, h], kn_ref[0, h]], axis=0).astype(BF16)
    v = jnp.concatenate([cv_ref[0, h], vn_ref[0, h]], axis=0).astype(BF16)
    return k, v


def _qk(q, k):
    return lax.dot_general(q, k, (((1,), (1,)), ((), ())), preferred_element_type=F32)


def _sb_sample_kernel(q_ref, ck_ref, kn_ref, cv_ref, vn_ref, tri_ref, o_ref):
    n_heads, tq = q_ref.shape[1], q_ref.shape[2]
    p_len = ck_ref.shape[2]
    nk = p_len + kn_ref.shape[2]
    row = lax.broadcasted_iota(jnp.int32, (tq, nk), 0)
    col = lax.broadcasted_iota(jnp.int32, (tq, nk), 1)
    mask = col < row + p_len
    tri = tri_ref[...]
    for h in range(n_heads):
        k, v = _sample_kv(ck_ref, kn_ref, cv_ref, vn_ref, h)
        q = (q_ref[0, h] * HEAD_DIM ** -0.5).astype(BF16)
        z = _qk(q, k)
        ls_full = -(jnp.maximum(z, 0.0) + jnp.log1p(jnp.exp(-jnp.abs(z))))
        ls = jnp.where(mask, ls_full, 0.0)
        later = _dot(ls.astype(BF16), tri)
        wgt = jnp.where(mask, jnp.exp(z + ls_full + later), 0.0)
        o_ref[0, :, h * HEAD_DIM:(h + 1) * HEAD_DIM] = _dot(wgt.astype(BF16), v)


def _fox_sample_kernel(q_ref, ck_ref, kn_ref, cv_ref, vn_ref, lf_ref, o_ref):
    n_heads, tq = q_ref.shape[1], q_ref.shape[2]
    p_len = ck_ref.shape[2]
    nk = p_len + kn_ref.shape[2]
    row = lax.broadcasted_iota(jnp.int32, (tq, nk), 0)
    col = lax.broadcasted_iota(jnp.int32, (tq, nk), 1)
    mask = col <= row + p_len
    diag = col == row + p_len
    f_all = _lane_cumsum(lf_ref[0])
    for h in range(n_heads):
        k, v = _sample_kv(ck_ref, kn_ref, cv_ref, vn_ref, h)
        q = (q_ref[0, h] * HEAD_DIM ** -0.5).astype(BF16)
        fk = f_all[h:h + 1, :]
        fq = jnp.sum(jnp.where(diag, fk, 0.0), axis=1, keepdims=True)
        z = _qk(q, k) + fq - fk
        z = jnp.where(mask, z, NEG_BIG)
        p = jnp.exp(z - jnp.max(z, axis=1, keepdims=True))
        l = jnp.sum(p, axis=1, keepdims=True)
        o_ref[0, :, h * HEAD_DIM:(h + 1) * HEAD_DIM] = _dot(p.astype(BF16), v) / l


def _attn_sample(kernel, q, ck, kn, cv, vn, extra, extra_spec):
    b, h, tq, hd = q.shape
    per_b = lambda a: pl.BlockSpec((1,) + a.shape[1:], lambda bi: (bi,) + (0,) * (a.ndim - 1))
    return pl.pallas_call(
        kernel,
        grid=(b,),
        in_specs=[per_b(a) for a in (q, ck, kn, cv, vn)] + [extra_spec],
        out_specs=pl.BlockSpec((1, tq, h * hd), lambda bi: (bi, 0, 0)),
        out_shape=jax.ShapeDtypeStruct((b, tq, h * hd), F32),
        compiler_params=_cparams(("parallel",)),
        name=kernel.__name__.strip("_"),
    )(q, ck, kn, cv, vn, extra)


def _route(lt, n_groups, epg):
    n = lt.shape[1]
    g = [lt[i:i + 1, :] for i in range(n_groups)]
    gmax = functools.reduce(jnp.maximum, g)
    g_w = 1.0 / functools.reduce(jnp.add, [jnp.exp(gi - gmax) for gi in g])
    is_g, taken = [], None
    for gi in g:
        hit = gi >= gmax
        if taken is not None:
            hit = jnp.logical_and(hit, jnp.logical_not(taken))
        taken = hit if taken is None else jnp.logical_or(taken, hit)
        is_g.append(hit)
    le = [lt[n_groups + i:n_groups + i + 1, :] for i in range(n_groups * epg)]
    e_sel = []
    for i in range(epg):
        v = le[(n_groups - 1) * epg + i]
        for gi in reversed(range(n_groups - 1)):
            v = jnp.where(is_g[gi], le[gi * epg + i], v)
        e_sel.append(v)
    emax = functools.reduce(jnp.maximum, e_sel)
    pe = [jnp.exp(v - emax) for v in e_sel]
    pden = functools.reduce(jnp.add, pe)
    prob = [p / pden for p in pe]

    def first_argmax(vals):
        vmax = functools.reduce(jnp.maximum, vals)
        hits, tk = [], None
        for v in vals:
            hit = v >= vmax
            if tk is not None:
                hit = jnp.logical_and(hit, jnp.logical_not(tk))
            tk = hit if tk is None else jnp.logical_or(tk, hit)
            hits.append(hit)
        return vmax, hits

    p1, t1 = first_argmax(prob)
    p2, t2 = first_argmax([jnp.where(t, -1.0, p) for t, p in zip(t1, prob)])
    tot = p1 + p2
    w1 = g_w * (p1 / tot)
    w2 = g_w * (p2 / tot)
    rows = lax.broadcasted_iota(jnp.int32, (LANES, n), 0)
    comb = jnp.zeros((LANES, n), F32)
    for gi in range(n_groups):
        for i in range(epg):
            val = jnp.where(is_g[gi], jnp.where(t1[i], w1, 0.0) + jnp.where(t2[i], w2, 0.0), 0.0)
            comb = jnp.where(rows == gi * epg + i, jnp.broadcast_to(val, (LANES, n)), comb)
    return comb


def _post_kernel(yaT_ref, ybT_ref, gates_ref, x_ref, mods_ref, wba_ref, wbb_ref, wout_ref,
                 gmoe_ref, wr2_ref, wrhi_ref, br_ref, x1_ref, h2_ref, comb_ref,
                 *, per_token, n_groups, epg):
    d = x_ref.shape[2]
    tdot = lambda aT, w: lax.dot_general(aT, w, (((0,), (0,)), ((), ())), preferred_element_type=F32)
    ua = tdot(yaT_ref[0], wba_ref[...])
    ub = tdot(ybT_ref[0], wbb_ref[...])
    gates = gates_ref[0].astype(F32)
    mix = gates[:, :d] * ua + gates[:, d:] * ub
    x1 = x_ref[0] + _mod(mods_ref, 2, per_token) * _dot(mix.astype(BF16), wout_ref[...])
    x1_ref[0] = x1
    h2 = _rms_mod(x1, gmoe_ref[...], _mod(mods_ref, 4, per_token), _mod(mods_ref, 3, per_token))
    h2_hi = h2.astype(BF16)
    h2_ref[0] = h2_hi
    h2_lo = (h2 - h2_hi.astype(F32)).astype(BF16)
    a = _dot(h2_hi, wr2_ref[...])
    logits = a[:, :LANES] + a[:, LANES:] + _dot(h2_lo, wrhi_ref[...]) + br_ref[...]
    comb_ref[0] = _route(logits.T, n_groups, epg).T


def _post(yaT, ybT, gates, x, mods, w_ba, w_bb, w_out, g_moe, wr2, wrhi, br, per_token, n_groups, epg):
    b, t, d = x.shape
    tm = min(POST_TILE, t)
    w = yaT.shape[1]
    const2 = lambda bi, ti: (0, 0)
    tok = lambda last: pl.BlockSpec((1, tm, last), lambda bi, ti: (bi, ti, 0))
    chan = pl.BlockSpec((1, w, tm), lambda bi, ti: (bi, 0, ti))
    if per_token:
        mods_spec = pl.BlockSpec(mods.shape, lambda bi, ti: (0, 0, 0))
    else:
        mods_spec = pl.BlockSpec((1, N_MOD, d), lambda bi, ti: (bi, 0, 0))
    in_specs = [chan, chan, tok(2 * d), tok(d), mods_spec,
                pl.BlockSpec(w_ba.shape, const2), pl.BlockSpec(w_bb.shape, const2),
                pl.BlockSpec(w_out.shape, const2), pl.BlockSpec((1, d), const2),
                pl.BlockSpec(wr2.shape, const2), pl.BlockSpec(wrhi.shape, const2),
                pl.BlockSpec((1, LANES), const2)]
    out_shape = (jax.ShapeDtypeStruct((b, t, d), F32),
                 jax.ShapeDtypeStruct((b, t, d), BF16),
                 jax.ShapeDtypeStruct((b, t, LANES), F32))
    return pl.pallas_call(
        functools.partial(_post_kernel, per_token=per_token, n_groups=n_groups, epg=epg),
        grid=(b, t // tm),
        in_specs=in_specs,
        out_specs=(tok(d), tok(d), tok(LANES)),
        out_shape=out_shape,
        compiler_params=_cparams(("parallel", "parallel")),
        name="post_sample" if per_token else "post_prompt",
    )(yaT, ybT, gates, x, mods, w_ba, w_bb, w_out, g_moe, wr2, wrhi, br)


def _moe_kernel(h2_ref, comb_ref, x1_ref, mods_ref, w13_ref, w2_ref, gfin_ref, y_ref, acc_ref,
                *, per_token):
    e = pl.program_id(1)
    hid = w2_ref.shape[1]

    @pl.when(e == 0)
    def _():
        acc_ref[...] = jnp.zeros_like(acc_ref)

    a = _dot(h2_ref[...], w13_ref[0])
    a1 = a[:, :hid]
    act = (a1 * jax.nn.sigmoid(a1)) * a[:, hid:]
    comb = comb_ref[...]
    lane = lax.broadcasted_iota(jnp.int32, comb.shape, 1)
    cw = jnp.sum(jnp.where(lane == e, comb, 0.0), axis=1, keepdims=True)
    acc_ref[...] += _dot((act * cw).astype(BF16), w2_ref[0])

    @pl.when(e == pl.num_programs(1) - 1)
    def _():
        x2 = x1_ref[...] + _mod(mods_ref, 5, per_token) * acc_ref[...]
        ms = jnp.mean(x2 * x2, axis=-1, keepdims=True)
        y_ref[...] = (x2 * lax.rsqrt(ms + RMS_EPS)) * gfin_ref[...]


def _moe(h2, comb, x1, mods, w13, w2, g_final, per_token, tokens_per_batch):
    n, d = x1.shape
    tm = min(MOE_TILE, n)
    n_exp = w13.shape[0]
    tiles_per_batch = tokens_per_batch // tm if not per_token else 1
    tok = lambda last: pl.BlockSpec((tm, last), lambda i, e: (i, 0))
    if per_token:
        mods_spec = pl.BlockSpec(mods.shape, lambda i, e: (0, 0, 0))
    else:
        mods_spec = pl.BlockSpec((1, N_MOD, d), lambda i, e: (i // tiles_per_batch, 0, 0))
    return pl.pallas_call(
        functools.partial(_moe_kernel, per_token=per_token),
        grid=(n // tm, n_exp),
        in_specs=[tok(d), tok(LANES), tok(d), mods_spec,
                  pl.BlockSpec((1,) + w13.shape[1:], lambda i, e: (e, 0, 0)),
                  pl.BlockSpec((1,) + w2.shape[1:], lambda i, e: (e, 0, 0)),
                  pl.BlockSpec((1, d), lambda i, e: (0, 0))],
        out_specs=tok(d),
        out_shape=jax.ShapeDtypeStruct((n, d), F32),
        scratch_shapes=[pltpu.VMEM((tm, d), F32)],
        compiler_params=_cparams(("parallel", "arbitrary")),
        name="moe_sample" if per_token else "moe_prompt",
    )(h2, comb, x1, mods, w13, w2, g_final)


def kernel(x_prompt, x_sample, cache_sb_k, cache_sb_v, cache_fox_k, cache_fox_v, cache_fox_logf,
           c_prompt, c_sample, w_ada, b_ada, g_mix, w_in, b_f, w_ba, w_bb, w_gate, b_gate, w_out,
           g_moe, w_rg, b_rg, w_re, b_re, w1, w3, w2, g_final):
    depth = w_ada.shape[0]
    assert depth == 1, "single-layer trunk"
    bp, t, d = x_prompt.shape
    bs, ts, _ = x_sample.shape
    n_heads = cache_sb_k.shape[2]
    p_len = cache_sb_k.shape[3]
    assert cache_fox_k.shape[2] == n_heads and n_heads * HEAD_DIM * 6 + n_heads == w_in.shape[2]
    assert n_heads == 8 and t % Q_TILE == 0 and Q_TILE % PRE_TILE == 0
    n_groups = w_rg.shape[2]
    n_exp = w_re.shape[2]
    epg = n_exp // n_groups
    assert n_groups + n_exp <= LANES
    w = n_heads * HEAD_DIM
    ns = bs * ts

    w_in_p = jnp.pad(w_in[0], ((0, 0), (0, LANES - n_heads))).astype(BF16)
    b_f_p = jnp.pad(b_f[0], (0, LANES - n_heads)).reshape(1, LANES)
    w_gate_b = w_gate[0].astype(BF16)
    b_gate_r = b_gate[0].reshape(1, -1)
    g_mix_r = g_mix[0].reshape(1, d)
    g_moe_r = g_moe[0].reshape(1, d)
    g_fin_r = g_final.reshape(1, d)
    w_ba_b, w_bb_b, w_out_b = w_ba[0].astype(BF16), w_bb[0].astype(BF16), w_out[0].astype(BF16)
    w_r = jnp.pad(jnp.concatenate([w_rg[0], w_re[0]], axis=1), ((0, 0), (0, LANES - n_groups - n_exp)))
    w_r_hi = w_r.astype(BF16)
    w_r_lo = (w_r - w_r_hi.astype(F32)).astype(BF16)
    wr2 = jnp.concatenate([w_r_hi, w_r_lo], axis=1)
    b_r = jnp.pad(jnp.concatenate([b_rg[0], b_re[0]]), (0, LANES - n_groups - n_exp)).reshape(1, LANES)
    w13 = jnp.concatenate([w1[0], w3[0]], axis=2).astype(BF16)
    w2_b = w2[0].astype(BF16)

    ada = _ada(jnp.concatenate([c_prompt, c_sample], axis=0), w_ada[0], b_ada[0])
    mods_p = ada[:bp].reshape(bp, N_MOD, d)
    mods_s = jnp.repeat(ada[bp:].reshape(bs, N_MOD, d).transpose(1, 0, 2), ts, axis=1)

    (qsT, ks, vsT, qfT, kf, vfT, ksl, vsl, kfl, vfl, logf_p, gates_p) = _pre_prompt(
        x_prompt, mods_p, g_mix_r, w_in_p, b_f_p, w_gate_b, b_gate_r, n_heads)
    tk = PRE_TILE
    ids = jnp.arange(tk)
    tri_p = (ids[None, :] > ids[:, None]).astype(BF16)
    yaT = _attn_prompt(_sb_prompt_kernel, qsT, ks, vsT, (tri_p,)).reshape(bp, w, t)
    ybT = _attn_prompt(_fox_prompt_kernel, qfT, kf, vfT).reshape(bp, w, t)
    x1_p, h2_p, comb_p = _post(yaT, ybT, gates_p, x_prompt, mods_p, w_ba_b, w_bb_b, w_out_b,
                               g_moe_r, wr2, w_r_hi, b_r, False, n_groups, epg)
    y_prompt = _moe(h2_p.reshape(bp * t, d), comb_p.reshape(bp * t, LANES), x1_p.reshape(bp * t, d),
                    mods_p, w13, w2_b, g_fin_r, False, t).reshape(bp, t, d)

    proj_s, logf_s, gates_s = _pre_sample(x_sample.reshape(ns, d), mods_s, g_mix_r, w_in_p, b_f_p,
                                          w_gate_b, b_gate_r, n_heads)
    heads = lambda i: proj_s[:, i * w:(i + 1) * w].reshape(bs, ts, n_heads, HEAD_DIM).transpose(0, 2, 1, 3)
    qa_s, ka_s, va_s, qb_s, kb_s, vb_s = [heads(i) for i in range(6)]
    lf_s = logf_s[:, :n_heads].reshape(bs, ts, n_heads).transpose(0, 2, 1)
    pad_k = lambda a: jnp.pad(a, ((0, 0), (0, 0), (0, LANES - ts), (0, 0)))
    nk = p_len + LANES
    ids = jnp.arange(nk)
    tri_s = (ids[:, None] > ids[None, :]).astype(BF16)
    lf_all = jnp.concatenate([cache_fox_logf[0], jnp.pad(lf_s, ((0, 0), (0, 0), (0, LANES - ts)))], axis=2)
    ya_s = _attn_sample(_sb_sample_kernel, qa_s, cache_sb_k[0], pad_k(ka_s), cache_sb_v[0], pad_k(va_s),
                        tri_s, pl.BlockSpec(tri_s.shape, lambda bi: (0, 0)))
    yb_s = _attn_sample(_fox_sample_kernel, qb_s, cache_fox_k[0], pad_k(kb_s), cache_fox_v[0], pad_k(vb_s),
                        lf_all, pl.BlockSpec((1, n_heads, nk), lambda bi: (bi, 0, 0)))
    to_chan = lambda y: y.reshape(ns, w).T.astype(BF16)[None]
    x1_s, h2_s, comb_s = _post(to_chan(ya_s), to_chan(yb_s), gates_s[None], x_sample.reshape(1, ns, d),
                               mods_s, w_ba_b, w_bb_b, w_out_b, g_moe_r, wr2, w_r_hi, b_r,
                               True, n_groups, epg)
    y_sample = _moe(h2_s[0], comb_s[0], x1_s[0], mods_s, w13, w2_b, g_fin_r, True, ns).reshape(bs, ts, d)

    lead = lambda a: a[None]
    return (y_prompt, y_sample,
            lead(ksl), lead(vsl), lead(kfl), lead(vfl), lead(logf_p),
            lead(ka_s), lead(va_s), lead(kb_s), lead(vb_s), lead(lf_s))
```

```python
import functools

import jax
import jax.numpy as jnp
from jax import lax
from jax.experimental import pallas as pl
from jax.experimental.pallas import tpu as pltpu

F32 = jnp.float32
BF16 = jnp.bfloat16

HEAD_DIM = 64
RMS_EPS = 1e-6
N_MOD = 6
LANES = 128
NEG_BIG = -1e30

PRE_TILE = 256
Q_TILE = 1024
STRIP = 256
N_STRIPS = Q_TILE // STRIP
LOG2E = 1.4426950408889634
POST_TILE = 512
MOE_TILE = 1024
VMEM_LIMIT = 56 * 1024 * 1024


def _cparams(sem):
    return pltpu.CompilerParams(dimension_semantics=sem, vmem_limit_bytes=VMEM_LIMIT)


def _log_sigmoid(x):
    return jnp.minimum(x, 0.0) - jnp.log1p(jnp.exp(-jnp.abs(x)))


def _rms_mod(x, g, scale, shift):
    ms = jnp.mean(x * x, axis=-1, keepdims=True)
    y = x * lax.rsqrt(ms + RMS_EPS)
    return (y * g) * (1.0 + scale) + shift


def _mod(mods_ref, i, per_token):
    return mods_ref[i] if per_token else mods_ref[0, i:i + 1, :]


def _dot(a, b):
    return jnp.dot(a, b, preferred_element_type=F32)


def _split3(f):
    hi = f.astype(BF16).astype(F32)
    r = f - hi
    mid = r.astype(BF16).astype(F32)
    lo = (r - mid).astype(BF16).astype(F32)
    return hi, mid, lo


def _lane_cumsum(x):
    n = x.shape[1]
    lane = lax.broadcasted_iota(jnp.int32, x.shape, 1)
    d = 1
    while d < n:
        x = x + jnp.where(lane >= d, pltpu.roll(x, d, axis=1), 0.0)
        d *= 2
    return x


def _ada_kernel(c_ref, w_ref, b_ref, o_ref):
    c = c_ref[...]
    s = c * jax.nn.sigmoid(c)
    o_ref[...] = jnp.dot(s, w_ref[...], preferred_element_type=F32,
                         precision=lax.Precision.HIGHEST) + b_ref[...]


def _ada(c_all, w_ada, b_ada):
    n, d = c_all.shape
    nout = w_ada.shape[1]
    tn = 1024
    return pl.pallas_call(
        _ada_kernel,
        grid=(nout // tn,),
        in_specs=[pl.BlockSpec((n, d), lambda j: (0, 0)),
                  pl.BlockSpec((d, tn), lambda j: (0, j)),
                  pl.BlockSpec((1, tn), lambda j: (0, j))],
        out_specs=pl.BlockSpec((n, tn), lambda j: (0, j)),
        out_shape=jax.ShapeDtypeStruct((n, nout), F32),
        compiler_params=_cparams(("arbitrary",)),
        name="ada",
    )(c_all, w_ada, b_ada.reshape(1, nout))


def _pre_core(x, mods_ref, gmix_ref, win_ref, wg_ref, bg_ref, per_token):
    h = _rms_mod(x, gmix_ref[...], _mod(mods_ref, 1, per_token), _mod(mods_ref, 0, per_token))
    hb = h.astype(BF16)
    proj = _dot(hb, win_ref[...])
    gates = jax.nn.sigmoid(_dot(hb, wg_ref[...]) + bg_ref[...])
    return proj, gates.astype(BF16)


def _pre_prompt_kernel(x_ref, mods_ref, gmix_ref, win_ref, bf_ref, wg_ref, bg_ref,
                       qsT_ref, ks_ref, vsT_ref, qfT_ref, kf_ref, vfT_ref,
                       ksl_ref, vsl_ref, kfl_ref, vfl_ref, logf_ref, gates_ref,
                       carry_ref, *, n_heads):
    tm = x_ref.shape[1]
    w = n_heads * HEAD_DIM

    @pl.when(pl.program_id(1) == 0)
    def _():
        carry_ref[...] = jnp.zeros_like(carry_ref)

    proj, gates = _pre_core(x_ref[0], mods_ref, gmix_ref, win_ref, wg_ref, bg_ref, False)
    gates_ref[0] = gates
    qa, ka, va, qb, kb, vb = [proj[:, i * w:(i + 1) * w] for i in range(6)]
    fg = proj[:, 6 * w:6 * w + LANES]
    scale = HEAD_DIM ** -0.5 * LOG2E
    qaT = (qa * scale).T
    vaT = va.T
    qbT = (qb * scale).T
    vbT = vb.T

    logfT = _log_sigmoid(fg + bf_ref[...]).T[0:n_heads, :]
    logf_ref[0] = logfT
    f = _lane_cumsum(logfT) + carry_ref[:, 0:1]
    carry_ref[...] = jnp.broadcast_to(f[:, tm - 1:tm], carry_ref.shape)
    f_hi, f_mid, f_lo = _split3(f * LOG2E)

    row64 = lax.broadcasted_iota(jnp.int32, (HEAD_DIM, tm), 0)
    row8 = lax.broadcasted_iota(jnp.int32, (8, tm), 0)
    zeros64 = jnp.zeros((HEAD_DIM, tm), BF16)
    ke_parts = []
    for h in range(n_heads):
        def bc(a, n):
            return jnp.broadcast_to(a[h:h + 1, :], (n, tm))
        qe = jnp.where(row64 < 3, 1.0,
                       jnp.where(row64 == 3, bc(f_hi, HEAD_DIM),
                                 jnp.where(row64 == 4, bc(f_mid, HEAD_DIM),
                                           jnp.where(row64 == 5, bc(f_lo, HEAD_DIM), 0.0))))
        sl = slice(h * HEAD_DIM, (h + 1) * HEAD_DIM)
        qfT_ref[0, h, 0:HEAD_DIM, :] = qbT[sl, :].astype(BF16)
        qfT_ref[0, h, HEAD_DIM:2 * HEAD_DIM, :] = qe.astype(BF16)
        qsT_ref[0, h, 0:HEAD_DIM, :] = qaT[sl, :].astype(BF16)
        qsT_ref[0, h, HEAD_DIM:2 * HEAD_DIM, :] = zeros64
        vsT_ref[0, h, 0] = vaT[sl, :].astype(BF16)
        vfT_ref[0, h, 0] = vbT[sl, :].astype(BF16)
        ksl_ref[0, h] = ka[:, sl]
        vsl_ref[0, h] = va[:, sl]
        kfl_ref[0, h] = kb[:, sl]
        vfl_ref[0, h] = vb[:, sl]
        ke_parts.append(
            jnp.where(row8 == 0, -bc(f_hi, 8),
                      jnp.where(row8 == 1, -bc(f_mid, 8),
                                jnp.where(row8 == 2, -bc(f_lo, 8),
                                          jnp.where(row8 < 6, 1.0, 0.0)))))
    ke_parts.append(jnp.zeros((LANES - 8 * n_heads, tm), F32))
    ke = jnp.concatenate(ke_parts, axis=0).T

    lane = lax.broadcasted_iota(jnp.int32, (tm, LANES), 1)
    for h in range(n_heads):
        base = (h // 2) * LANES
        ka_slab = ka[:, base:base + LANES]
        kb_slab = kb[:, base:base + LANES]
        if h % 2:
            ka_slab = pltpu.roll(ka_slab, HEAD_DIM, axis=1)
            kb_slab = pltpu.roll(kb_slab, HEAD_DIM, axis=1)
        ext = pltpu.roll(ke, HEAD_DIM - 8 * h, axis=1)
        ks_ref[0, h] = jnp.where(lane < HEAD_DIM, ka_slab, 0.0).astype(BF16)
        kf_ref[0, h] = jnp.where(lane < HEAD_DIM, kb_slab,
                                 jnp.where(lane < HEAD_DIM + 8, ext, 0.0)).astype(BF16)


def _pre_prompt(x, mods, g_mix, w_in_p, b_f_p, w_gate, b_gate, n_heads):
    b, t, d = x.shape
    tm = PRE_TILE
    nt = t // tm
    h, hd = n_heads, HEAD_DIM
    const2 = lambda bi, ti: (0, 0)
    head_t = lambda bi, ti: (bi, 0, 0, ti)
    head_s = lambda bi, ti: (bi, 0, ti, 0)
    blk_t = lambda bi, ti: (bi, 0, ti, 0, 0)
    out_shape = (
        jax.ShapeDtypeStruct((b, h, 2 * hd, t), BF16),
        jax.ShapeDtypeStruct((b, h, t, 2 * hd), BF16),
        jax.ShapeDtypeStruct((b, h, nt, hd, tm), BF16),
        jax.ShapeDtypeStruct((b, h, 2 * hd, t), BF16),
        jax.ShapeDtypeStruct((b, h, t, 2 * hd), BF16),
        jax.ShapeDtypeStruct((b, h, nt, hd, tm), BF16),
        jax.ShapeDtypeStruct((b, h, t, hd), F32),
        jax.ShapeDtypeStruct((b, h, t, hd), F32),
        jax.ShapeDtypeStruct((b, h, t, hd), F32),
        jax.ShapeDtypeStruct((b, h, t, hd), F32),
        jax.ShapeDtypeStruct((b, h, t), F32),
        jax.ShapeDtypeStruct((b, t, w_gate.shape[1]), BF16),
    )
    out_specs = (
        pl.BlockSpec((1, h, 2 * hd, tm), head_t),
        pl.BlockSpec((1, h, tm, 2 * hd), head_s),
        pl.BlockSpec((1, h, 1, hd, tm), blk_t),
        pl.BlockSpec((1, h, 2 * hd, tm), head_t),
        pl.BlockSpec((1, h, tm, 2 * hd), head_s),
        pl.BlockSpec((1, h, 1, hd, tm), blk_t),
        pl.BlockSpec((1, h, tm, hd), head_s),
        pl.BlockSpec((1, h, tm, hd), head_s),
        pl.BlockSpec((1, h, tm, hd), head_s),
        pl.BlockSpec((1, h, tm, hd), head_s),
        pl.BlockSpec((1, h, tm), lambda bi, ti: (bi, 0, ti)),
        pl.BlockSpec((1, tm, w_gate.shape[1]), lambda bi, ti: (bi, ti, 0)),
    )
    in_specs = [
        pl.BlockSpec((1, tm, d), lambda bi, ti: (bi, ti, 0)),
        pl.BlockSpec((1, N_MOD, d), lambda bi, ti: (bi, 0, 0)),
        pl.BlockSpec((1, d), const2),
        pl.BlockSpec(w_in_p.shape, const2),
        pl.BlockSpec((1, LANES), const2),
        pl.BlockSpec(w_gate.shape, const2),
        pl.BlockSpec((1, w_gate.shape[1]), const2),
    ]
    return pl.pallas_call(
        functools.partial(_pre_prompt_kernel, n_heads=n_heads),
        grid=(b, nt),
        in_specs=in_specs,
        out_specs=out_specs,
        out_shape=out_shape,
        scratch_shapes=[pltpu.VMEM((h, LANES), F32)],
        compiler_params=_cparams(("arbitrary", "arbitrary")),
        name="pre_prompt",
    )(x, mods, g_mix, w_in_p, b_f_p, w_gate, b_gate)


def _pre_sample_kernel(x_ref, mods_ref, gmix_ref, win_ref, bf_ref, wg_ref, bg_ref,
                       proj_ref, logf_ref, gates_ref, *, n_heads):
    w = n_heads * HEAD_DIM
    proj, gates = _pre_core(x_ref[...], mods_ref, gmix_ref, win_ref, wg_ref, bg_ref, True)
    proj_ref[...] = proj
    gates_ref[...] = gates
    logf_ref[...] = _log_sigmoid(proj[:, 6 * w:6 * w + LANES] + bf_ref[...])


def _pre_sample(x, mods_tok, g_mix, w_in_p, b_f_p, w_gate, b_gate, n_heads):
    n, d = x.shape
    full = lambda a: pl.BlockSpec(a.shape, lambda i: (0,) * a.ndim)
    args = (x, mods_tok, g_mix, w_in_p, b_f_p, w_gate, b_gate)
    out_shape = (jax.ShapeDtypeStruct((n, w_in_p.shape[1]), F32),
                 jax.ShapeDtypeStruct((n, LANES), F32),
                 jax.ShapeDtypeStruct((n, w_gate.shape[1]), BF16))
    return pl.pallas_call(
        functools.partial(_pre_sample_kernel, n_heads=n_heads),
        grid=(1,),
        in_specs=[full(a) for a in args],
        out_specs=tuple(pl.BlockSpec(s.shape, lambda i: (0, 0)) for s in out_shape),
        out_shape=out_shape,
        compiler_params=_cparams(("arbitrary",)),
        name="pre_sample",
    )(*args)


FULL, DIAG, SKIP = "full", "diag", "skip"


def _attn_parts(qT_ref, k_ref, vT_ref):
    assert qT_ref.shape[3] == N_STRIPS * STRIP and vT_ref.shape[4] == STRIP
    nk = vT_ref.shape[2]

    def qk(j, z_scr, strips=range(N_STRIPS)):
        k = k_ref[0, 0, pl.ds(pl.multiple_of(j * STRIP, STRIP), STRIP), :]
        for s in strips:
            z_scr[s] = _dot(k, qT_ref[0, 0, :, s * STRIP:(s + 1) * STRIP])

    clip = lambda j: jnp.clip(j, 0, nk - 1)
    row = lax.broadcasted_iota(jnp.int32, (STRIP, STRIP), 0)
    col = lax.broadcasted_iota(jnp.int32, (STRIP, STRIP), 1)
    return qk, clip, row, col


def _modes(r):
    return tuple(FULL if r < s else DIAG if r == s else SKIP for s in range(N_STRIPS))


def _sb_prompt_kernel(qT_ref, k_ref, vT_ref, ntri_ref, o_ref, z_a, z_b, w_a, w_b):
    qk, clip, row, col = _attn_parts(qT_ref, k_ref, vT_ref)
    base = pl.program_id(2) * N_STRIPS
    diag = row < col
    blk = lambda t: clip(base + N_STRIPS - 1 - t)
    sign = jnp.uint32(0x80000000)

    def pv(j, w_scr, carry, which, pending):
        vT = vT_ref[0, 0, j]
        return tuple(((cr[0], cr[1] + _dot(vT, w_scr[s]) * cr[2 + which]) + cr[2:]) if pending[s] else cr
                     for s, cr in enumerate(carry))

    def softplus_phase(z_scr, carry, modes, which):
        carry = list(carry)
        mids = []
        for s, mode in enumerate(modes):
            if mode == SKIP:
                continue
            c, acc, sa, sb = carry[s]
            z = z_scr[s]
            neg_abs = lax.bitcast_convert_type(lax.bitcast_convert_type(z, jnp.uint32) | sign, F32)
            sp_full = jnp.maximum(z, 0.0) + jnp.log2(1.0 + jnp.exp2(neg_abs))
            sp = jnp.where(diag, sp_full, 0.0) if mode == DIAG else sp_full
            sp_b = sp.astype(BF16)
            later = _dot(ntri_ref[...], sp_b)
            mids.append((s, mode, z - sp_full, later))
            scale = jnp.exp2(c)
            c = c + later[0:1, :] - sp_b[0:1, :].astype(F32)
            carry[s] = (c, acc, sa, scale) if which else (c, acc, scale, sb)
        return tuple(carry), mids

    def weight_phase(w_scr, mids):
        for s, mode, lsig, later in mids:
            wgt = jnp.exp2(lsig + later)
            if mode == DIAG:
                wgt = jnp.where(diag, wgt, 0.0)
            w_scr[s] = wgt.astype(BF16)

    all_strips = (True,) * N_STRIPS

    def pair(t, carry, modes_a, modes_b, pend_a=all_strips, pend_b=all_strips):
        carry = pv(blk(t - 2), w_a, carry, 0, pend_a)
        carry = pv(blk(t - 1), w_b, carry, 1, pend_b)
        qk(blk(t + 1), z_b)
        carry, mids_a = softplus_phase(z_a, carry, modes_a, 0)
        qk(blk(t + 2), z_a)
        carry, mids_b = softplus_phase(z_b, carry, modes_b, 1)
        weight_phase(w_a, mids_a)
        weight_phase(w_b, mids_b)
        return carry

    qk(blk(0), z_a)
    one = jnp.ones((1, STRIP), F32)
    carry = ((jnp.zeros((1, STRIP), F32), jnp.zeros((HEAD_DIM, STRIP), F32), one, one),) * N_STRIPS
    pend_a = pend_b = (False,) * N_STRIPS
    for t in range(0, N_STRIPS, 2):
        modes_a, modes_b = _modes(N_STRIPS - 1 - t), _modes(N_STRIPS - 2 - t)
        carry = pair(t, carry, modes_a, modes_b, pend_a, pend_b)
        pend_a = tuple(p or m != SKIP for p, m in zip(pend_a, modes_a))
        pend_b = tuple(p or m != SKIP for p, m in zip(pend_b, modes_b))
    for s in range(N_STRIPS):
        if not pend_a[s]:
            w_a[s] = jnp.zeros((STRIP, STRIP), BF16)
        if not pend_b[s]:
            w_b[s] = jnp.zeros((STRIP, STRIP), BF16)
    full = (FULL,) * N_STRIPS
    carry = lax.fori_loop(0, base // 2, lambda i, cr: pair(N_STRIPS + 2 * i, cr, full, full), carry)
    last = base + N_STRIPS - 1
    carry = pv(blk(last - 1), w_a, carry, 0, all_strips)
    carry = pv(blk(last), w_b, carry, 1, all_strips)
    for s in range(N_STRIPS):
        o_ref[0, 0, :, s * STRIP:(s + 1) * STRIP] = carry[s][1].astype(BF16)


def _fox_prompt_kernel(qT_ref, k_ref, vT_ref, o_ref, z_a, z_b, p_a, p_b):
    qk, clip, row, col = _attn_parts(qT_ref, k_ref, vT_ref)
    base = pl.program_id(2) * N_STRIPS
    diag = row <= col

    def pv(j, p_scr, carry, which, pending):
        vT = vT_ref[0, 0, j]
        return tuple(((cr[0], cr[1], cr[3 + which] * cr[2] + _dot(vT, p_scr[s])) + cr[3:]) if pending[s] else cr
                     for s, cr in enumerate(carry))

    def valu(z_scr, p_scr, carry, modes, which):
        out = []
        for s, mode in enumerate(modes):
            m, l, acc, aa, ab = carry[s]
            if mode == SKIP:
                alpha = jnp.ones((1, STRIP), F32)
            else:
                def scores():
                    sc = z_scr[s]
                    return jnp.where(diag, sc, NEG_BIG) if mode == DIAG else sc

                m_new = jnp.maximum(m, jnp.max(scores(), axis=0, keepdims=True))
                alpha = jnp.exp2(m - m_new)
                p = jnp.exp2(scores() - m_new)
                p_scr[s] = p.astype(BF16)
                m, l = m_new, alpha * l + jnp.sum(p, axis=0, keepdims=True)
            out.append((m, l, acc, aa, alpha) if which else (m, l, acc, alpha, ab))
        return tuple(out)

    all_strips = (True,) * N_STRIPS

    def pair(t, carry, modes_a, modes_b, pend_a=all_strips, pend_b=all_strips):
        carry = pv(clip(t - 2), p_a, carry, 0, pend_a)
        carry = pv(clip(t - 1), p_b, carry, 1, pend_b)
        qk(clip(t + 1), z_b)
        carry = valu(z_a, p_a, carry, modes_a, 0)
        qk(clip(t + 2), z_a)
        return valu(z_b, p_b, carry, modes_b, 1)

    p_a[...] = jnp.zeros(p_a.shape, BF16)
    p_b[...] = jnp.zeros(p_b.shape, BF16)
    qk(0, z_a)
    one = jnp.ones((1, STRIP), F32)
    carry = ((jnp.full((1, STRIP), NEG_BIG, F32), jnp.zeros((1, STRIP), F32),
              jnp.zeros((HEAD_DIM, STRIP), F32), one, one),) * N_STRIPS
    full = (FULL,) * N_STRIPS
    carry = lax.fori_loop(0, base // 2, lambda i, cr: pair(2 * i, cr, full, full), carry)
    pend_a = pend_b = all_strips
    for r in range(0, N_STRIPS, 2):
        carry = pair(base + r, carry, _modes(r), _modes(r + 1), pend_a, pend_b)
        pend_a = tuple(m != SKIP for m in _modes(r))
        pend_b = tuple(m != SKIP for m in _modes(r + 1))
    last = base + N_STRIPS - 1
    carry = pv(clip(last - 1), p_a, carry, 0, pend_a)
    carry = pv(clip(last), p_b, carry, 1, pend_b)
    for s in range(N_STRIPS):
        _, l, acc, _, _ = carry[s]
        o_ref[0, 0, :, s * STRIP:(s + 1) * STRIP] = (acc / l).astype(BF16)


def _attn_prompt(kernel, qT, k, vT, extra=()):
    b, h, kd, t = qT.shape
    nk, tk = vT.shape[2], vT.shape[4]
    tq = Q_TILE
    in_specs = [
        pl.BlockSpec((1, 1, kd, tq), lambda bi, hi, qi: (bi, hi, 0, qi)),
        pl.BlockSpec((1, 1, t, kd), lambda bi, hi, qi: (bi, hi, 0, 0)),
        pl.BlockSpec((1, 1, nk, HEAD_DIM, tk), lambda bi, hi, qi: (bi, hi, 0, 0, 0)),
    ] + [pl.BlockSpec(a.shape, lambda bi, hi, qi: (0, 0)) for a in extra]
    scores = pltpu.VMEM((N_STRIPS, STRIP, STRIP), F32)
    probs = pltpu.VMEM((N_STRIPS, STRIP, STRIP), BF16)
    return pl.pallas_call(
        kernel,
        grid=(b, h, t // tq),
        in_specs=in_specs,
        out_specs=pl.BlockSpec((1, 1, HEAD_DIM, tq), lambda bi, hi, qi: (bi, hi, 0, qi)),
        out_shape=jax.ShapeDtypeStruct((b, h, HEAD_DIM, t), BF16),
        scratch_shapes=[scores, scores, probs, probs],
        compiler_params=_cparams(("parallel", "parallel", "arbitrary")),
        name=kernel.__name__.strip("_"),
    )(qT, k, vT, *extra)


def _sample_kv(ck_ref, kn_ref, cv_ref, vn_ref, h):
    k = jnp.concatenate([ck_ref[0, h], kn_ref[0, h]], axis=0).astype(BF16)
    v = jnp.concatenate([cv_ref[0, h], vn_ref[0, h]], axis=0).astype(BF16)
    return k, v


def _qk(q, k):
    return lax.dot_general(q, k, (((1,), (1,)), ((), ())), preferred_element_type=F32)


def _sb_sample_kernel(q_ref, ck_ref, kn_ref, cv_ref, vn_ref, tri_ref, o_ref):
    n_heads, tq = q_ref.shape[1], q_ref.shape[2]
    p_len = ck_ref.shape[2]
    nk = p_len + kn_ref.shape[2]
    row = lax.broadcasted_iota(jnp.int32, (tq, nk), 0)
    col = lax.broadcasted_iota(jnp.int32, (tq, nk), 1)
    mask = col < row + p_len
    tri = tri_ref[...]
    for h in range(n_heads):
        k, v = _sample_kv(ck_ref, kn_ref, cv_ref, vn_ref, h)
        q = (q_ref[0, h] * HEAD_DIM ** -0.5).astype(BF16)
        z = _qk(q, k)
        ls_full = -(jnp.maximum(z, 0.0) + jnp.log1p(jnp.exp(-jnp.abs(z))))
        ls = jnp.where(mask, ls_full, 0.0)
        later = _dot(ls.astype(BF16), tri)
        wgt = jnp.where(mask, jnp.exp(z + ls_full + later), 0.0)
        o_ref[0, :, h * HEAD_DIM:(h + 1) * HEAD_DIM] = _dot(wgt.astype(BF16), v)


def _fox_sample_kernel(q_ref, ck_ref, kn_ref, cv_ref, vn_ref, lf_ref, o_ref):
    n_heads, tq = q_ref.shape[1], q_ref.shape[2]
    p_len = ck_ref.shape[2]
    nk = p_len + kn_ref.shape[2]
    row = lax.broadcasted_iota(jnp.int32, (tq, nk), 0)
    col = lax.broadcasted_iota(jnp.int32, (tq, nk), 1)
    mask = col <= row + p_len
    diag = col == row + p_len
    f_all = _lane_cumsum(lf_ref[0])
    for h in range(n_heads):
        k, v = _sample_kv(ck_ref, kn_ref, cv_ref, vn_ref, h)
        q = (q_ref[0, h] * HEAD_DIM ** -0.5).astype(BF16)
        fk = f_all[h:h + 1, :]
        fq = jnp.sum(jnp.where(diag, fk, 0.0), axis=1, keepdims=True)
        z = _qk(q, k) + fq - fk
        z = jnp.where(mask, z, NEG_BIG)
        p = jnp.exp(z - jnp.max(z, axis=1, keepdims=True))
        l = jnp.sum(p, axis=1, keepdims=True)
        o_ref[0, :, h * HEAD_DIM:(h + 1) * HEAD_DIM] = _dot(p.astype(BF16), v) / l


def _attn_sample(kernel, q, ck, kn, cv, vn, extra, extra_spec):
    b, h, tq, hd = q.shape
    per_b = lambda a: pl.BlockSpec((1,) + a.shape[1:], lambda bi: (bi,) + (0,) * (a.ndim - 1))
    return pl.pallas_call(
        kernel,
        grid=(b,),
        in_specs=[per_b(a) for a in (q, ck, kn, cv, vn)] + [extra_spec],
        out_specs=pl.BlockSpec((1, tq, h * hd), lambda bi: (bi, 0, 0)),
        out_shape=jax.ShapeDtypeStruct((b, tq, h * hd), F32),
        compiler_params=_cparams(("parallel",)),
        name=kernel.__name__.strip("_"),
    )(q, ck, kn, cv, vn, extra)


def _route(lt, n_groups, epg):
    n = lt.shape[1]
    g = [lt[i:i + 1, :] for i in range(n_groups)]
    gmax = functools.reduce(jnp.maximum, g)
    g_w = 1.0 / functools.reduce(jnp.add, [jnp.exp(gi - gmax) for gi in g])
    is_g, taken = [], None
    for gi in g:
        hit = gi >= gmax
        if taken is not None:
            hit = jnp.logical_and(hit, jnp.logical_not(taken))
        taken = hit if taken is None else jnp.logical_or(taken, hit)
        is_g.append(hit)
    le = [lt[n_groups + i:n_groups + i + 1, :] for i in range(n_groups * epg)]
    e_sel = []
    for i in range(epg):
        v = le[(n_groups - 1) * epg + i]
        for gi in reversed(range(n_groups - 1)):
            v = jnp.where(is_g[gi], le[gi * epg + i], v)
        e_sel.append(v)
    emax = functools.reduce(jnp.maximum, e_sel)
    pe = [jnp.exp(v - emax) for v in e_sel]
    pden = functools.reduce(jnp.add, pe)
    prob = [p / pden for p in pe]

    def first_argmax(vals):
        vmax = functools.reduce(jnp.maximum, vals)
        hits, tk = [], None
        for v in vals:
            hit = v >= vmax
            if tk is not None:
                hit = jnp.logical_and(hit, jnp.logical_not(tk))
            tk = hit if tk is None else jnp.logical_or(tk, hit)
            hits.append(hit)
        return vmax, hits

    p1, t1 = first_argmax(prob)
    p2, t2 = first_argmax([jnp.where(t, -1.0, p) for t, p in zip(t1, prob)])
    tot = p1 + p2
    w1 = g_w * (p1 / tot)
    w2 = g_w * (p2 / tot)
    rows = lax.broadcasted_iota(jnp.int32, (LANES, n), 0)
    comb = jnp.zeros((LANES, n), F32)
    for gi in range(n_groups):
        for i in range(epg):
            val = jnp.where(is_g[gi], jnp.where(t1[i], w1, 0.0) + jnp.where(t2[i], w2, 0.0), 0.0)
            comb = jnp.where(rows == gi * epg + i, jnp.broadcast_to(val, (LANES, n)), comb)
    return comb


def _post_kernel(yaT_ref, ybT_ref, gates_ref, x_ref, mods_ref, wba_ref, wbb_ref, wout_ref,
                 gmoe_ref, wr2_ref, wrhi_ref, br_ref, x1_ref, h2_ref, comb_ref,
                 *, per_token, n_groups, epg):
    d = x_ref.shape[2]
    tdot = lambda aT, w: lax.dot_general(aT, w, (((0,), (0,)), ((), ())), preferred_element_type=F32)
    ua = tdot(yaT_ref[0], wba_ref[...])
    ub = tdot(ybT_ref[0], wbb_ref[...])
    gates = gates_ref[0].astype(F32)
    mix = gates[:, :d] * ua + gates[:, d:] * ub
    x1 = x_ref[0] + _mod(mods_ref, 2, per_token) * _dot(mix.astype(BF16), wout_ref[...])
    x1_ref[0] = x1
    h2 = _rms_mod(x1, gmoe_ref[...], _mod(mods_ref, 4, per_token), _mod(mods_ref, 3, per_token))
    h2_hi = h2.astype(BF16)
    h2_ref[0] = h2_hi
    h2_lo = (h2 - h2_hi.astype(F32)).astype(BF16)
    a = _dot(h2_hi, wr2_ref[...])
    logits = a[:, :LANES] + a[:, LANES:] + _dot(h2_lo, wrhi_ref[...]) + br_ref[...]
    comb_ref[0] = _route(logits.T, n_groups, epg).T


def _post(yaT, ybT, gates, x, mods, w_ba, w_bb, w_out, g_moe, wr2, wrhi, br, per_token, n_groups, epg):
    b, t, d = x.shape
    tm = min(POST_TILE, t)
    w = yaT.shape[1]
    const2 = lambda bi, ti: (0, 0)
    tok = lambda last: pl.BlockSpec((1, tm, last), lambda bi, ti: (bi, ti, 0))
    chan = pl.BlockSpec((1, w, tm), lambda bi, ti: (bi, 0, ti))
    if per_token:
        mods_spec = pl.BlockSpec(mods.shape, lambda bi, ti: (0, 0, 0))
    else:
        mods_spec = pl.BlockSpec((1, N_MOD, d), lambda bi, ti: (bi, 0, 0))
    in_specs = [chan, chan, tok(2 * d), tok(d), mods_spec,
                pl.BlockSpec(w_ba.shape, const2), pl.BlockSpec(w_bb.shape, const2),
                pl.BlockSpec(w_out.shape, const2), pl.BlockSpec((1, d), const2),
                pl.BlockSpec(wr2.shape, const2), pl.BlockSpec(wrhi.shape, const2),
                pl.BlockSpec((1, LANES), const2)]
    out_shape = (jax.ShapeDtypeStruct((b, t, d), F32),
                 jax.ShapeDtypeStruct((b, t, d), BF16),
                 jax.ShapeDtypeStruct((b, t, LANES), F32))
    return pl.pallas_call(
        functools.partial(_post_kernel, per_token=per_token, n_groups=n_groups, epg=epg),
        grid=(b, t // tm),
        in_specs=in_specs,
        out_specs=(tok(d), tok(d), tok(LANES)),
        out_shape=out_shape,
        compiler_params=_cparams(("parallel", "parallel")),
        name="post_sample" if per_token else "post_prompt",
    )(yaT, ybT, gates, x, mods, w_ba, w_bb, w_out, g_moe, wr2, wrhi, br)


def _moe_kernel(h2_ref, comb_ref, x1_ref, mods_ref, w13_ref, w2_ref, gfin_ref, y_ref, acc_ref,
                *, per_token):
    e = pl.program_id(1)
    hid = w2_ref.shape[1]

    @pl.when(e == 0)
    def _():
        acc_ref[...] = jnp.zeros_like(acc_ref)

    a = _dot(h2_ref[...], w13_ref[0])
    a1 = a[:, :hid]
    act = (a1 * jax.nn.sigmoid(a1)) * a[:, hid:]
    comb = comb_ref[...]
    lane = lax.broadcasted_iota(jnp.int32, comb.shape, 1)
    cw = jnp.sum(jnp.where(lane == e, comb, 0.0), axis=1, keepdims=True)
    acc_ref[...] += _dot((act * cw).astype(BF16), w2_ref[0])

    @pl.when(e == pl.num_programs(1) - 1)
    def _():
        x2 = x1_ref[...] + _mod(mods_ref, 5, per_token) * acc_ref[...]
        ms = jnp.mean(x2 * x2, axis=-1, keepdims=True)
        y_ref[...] = (x2 * lax.rsqrt(ms + RMS_EPS)) * gfin_ref[...]


def _moe(h2, comb, x1, mods, w13, w2, g_final, per_token, tokens_per_batch):
    n, d = x1.shape
    tm = min(MOE_TILE, n)
    n_exp = w13.shape[0]
    tiles_per_batch = tokens_per_batch // tm if not per_token else 1
    tok = lambda last: pl.BlockSpec((tm, last), lambda i, e: (i, 0))
    if per_token:
        mods_spec = pl.BlockSpec(mods.shape, lambda i, e: (0, 0, 0))
    else:
        mods_spec = pl.BlockSpec((1, N_MOD, d), lambda i, e: (i // tiles_per_batch, 0, 0))
    return pl.pallas_call(
        functools.partial(_moe_kernel, per_token=per_token),
        grid=(n // tm, n_exp),
        in_specs=[tok(d), tok(LANES), tok(d), mods_spec,
                  pl.BlockSpec((1,) + w13.shape[1:], lambda i, e: (e, 0, 0)),
                  pl.BlockSpec((1,) + w2.shape[1:], lambda i, e: (e, 0, 0)),
                  pl.BlockSpec((1, d), lambda i, e: (0, 0))],
        out_specs=tok(d),
        out_shape=jax.ShapeDtypeStruct((n, d), F32),
        scratch_shapes=[pltpu.VMEM((tm, d), F32)],
        compiler_params=_cparams(("parallel", "arbitrary")),
        name="moe_sample" if per_token else "moe_prompt",
    )(h2, comb, x1, mods, w13, w2, g_final)


def kernel(x_prompt, x_sample, cache_sb_k, cache_sb_v, cache_fox_k, cache_fox_v, cache_fox_logf,
           c_prompt, c_sample, w_ada, b_ada, g_mix, w_in, b_f, w_ba, w_bb, w_gate, b_gate, w_out,
           g_moe, w_rg, b_rg, w_re, b_re, w1, w3, w2, g_final):
    depth = w_ada.shape[0]
    assert depth == 1, "single-layer trunk"
    bp, t, d = x_prompt.shape
    bs, ts, _ = x_sample.shape
    n_heads = cache_sb_k.shape[2]
    p_len = cache_sb_k.shape[3]
    assert cache_fox_k.shape[2] == n_heads and n_heads * HEAD_DIM * 6 + n_heads == w_in.shape[2]
    assert n_heads == 8 and t % Q_TILE == 0 and Q_TILE % PRE_TILE == 0
    n_groups = w_rg.shape[2]
    n_exp = w_re.shape[2]
    epg = n_exp // n_groups
    assert n_groups + n_exp <= LANES
    w = n_heads * HEAD_DIM
    ns = bs * ts

    w_in_p = jnp.pad(w_in[0], ((0, 0), (0, LANES - n_heads))).astype(BF16)
    b_f_p = jnp.pad(b_f[0], (0, LANES - n_heads)).reshape(1, LANES)
    w_gate_b = w_gate[0].astype(BF16)
    b_gate_r = b_gate[0].reshape(1, -1)
    g_mix_r = g_mix[0].reshape(1, d)
    g_moe_r = g_moe[0].reshape(1, d)
    g_fin_r = g_final.reshape(1, d)
    w_ba_b, w_bb_b, w_out_b = w_ba[0].astype(BF16), w_bb[0].astype(BF16), w_out[0].astype(BF16)
    w_r = jnp.pad(jnp.concatenate([w_rg[0], w_re[0]], axis=1), ((0, 0), (0, LANES - n_groups - n_exp)))
    w_r_hi = w_r.astype(BF16)
    w_r_lo = (w_r - w_r_hi.astype(F32)).astype(BF16)
    wr2 = jnp.concatenate([w_r_hi, w_r_lo], axis=1)
    b_r = jnp.pad(jnp.concatenate([b_rg[0], b_re[0]]), (0, LANES - n_groups - n_exp)).reshape(1, LANES)
    w13 = jnp.concatenate([w1[0], w3[0]], axis=2).astype(BF16)
    w2_b = w2[0].astype(BF16)

    ada = _ada(jnp.concatenate([c_prompt, c_sample], axis=0), w_ada[0], b_ada[0])
    mods_p = ada[:bp].reshape(bp, N_MOD, d)
    mods_s = jnp.repeat(ada[bp:].reshape(bs, N_MOD, d).transpose(1, 0, 2), ts, axis=1)

    (qsT, ks, vsT, qfT, kf, vfT, ksl, vsl, kfl, vfl, logf_p, gates_p) = _pre_prompt(
        x_prompt, mods_p, g_mix_r, w_in_p, b_f_p, w_gate_b, b_gate_r, n_heads)
    tk = PRE_TILE
    ids = jnp.arange(tk)
    ntri_p = -(ids[None, :] > ids[:, None]).astype(BF16)
    yaT = _attn_prompt(_sb_prompt_kernel, qsT, ks, vsT, (ntri_p,)).reshape(bp, w, t)
    ybT = _attn_prompt(_fox_prompt_kernel, qfT, kf, vfT).reshape(bp, w, t)
    x1_p, h2_p, comb_p = _post(yaT, ybT, gates_p, x_prompt, mods_p, w_ba_b, w_bb_b, w_out_b,
                               g_moe_r, wr2, w_r_hi, b_r, False, n_groups, epg)
    y_prompt = _moe(h2_p.reshape(bp * t, d), comb_p.reshape(bp * t, LANES), x1_p.reshape(bp * t, d),
                    mods_p, w13, w2_b, g_fin_r, False, t).reshape(bp, t, d)

    proj_s, logf_s, gates_s = _pre_sample(x_sample.reshape(ns, d), mods_s, g_mix_r, w_in_p, b_f_p,
                                          w_gate_b, b_gate_r, n_heads)
    heads = lambda i: proj_s[:, i * w:(i + 1) * w].reshape(bs, ts, n_heads, HEAD_DIM).transpose(0, 2, 1, 3)
    qa_s, ka_s, va_s, qb_s, kb_s, vb_s = [heads(i) for i in range(6)]
    lf_s = logf_s[:, :n_heads].reshape(bs, ts, n_heads).transpose(0, 2, 1)
    pad_k = lambda a: jnp.pad(a, ((0, 0), (0, 0), (0, LANES - ts), (0, 0)))
    nk = p_len + LANES
    ids = jnp.arange(nk)
    tri_s = (ids[:, None] > ids[None, :]).astype(BF16)
    lf_all = jnp.concatenate([cache_fox_logf[0], jnp.pad(lf_s, ((0, 0), (0, 0), (0, LANES - ts)))], axis=2)
    ya_s = _attn_sample(_sb_sample_kernel, qa_s, cache_sb_k[0], pad_k(ka_s), cache_sb_v[0], pad_k(va_s),
                        tri_s, pl.BlockSpec(tri_s.shape, lambda bi: (0, 0)))
    yb_s = _attn_sample(_fox_sample_kernel, qb_s, cache_fox_k[0], pad_k(kb_s), cache_fox_v[0], pad_k(vb_s),
                        lf_all, pl.BlockSpec((1, n_heads, nk), lambda bi: (bi, 0, 0)))
    to_chan = lambda y: y.reshape(ns, w).T.astype(BF16)[None]
    x1_s, h2_s, comb_s = _post(to_chan(ya_s), to_chan(yb_s), gates_s[None], x_sample.reshape(1, ns, d),
                               mods_s, w_ba_b, w_bb_b, w_out_b, g_moe_r, wr2, w_r_hi, b_r,
                               True, n_groups, epg)
    y_sample = _moe(h2_s[0], comb_s[0], x1_s[0], mods_s, w13, w2_b, g_fin_r, True, ns).reshape(bs, ts, d)

    lead = lambda a: a[None]
    return (y_prompt, y_sample,
            lead(ksl), lead(vsl), lead(kfl), lead(vfl), lead(logf_p),
            lead(ka_s), lead(va_s), lead(kb_s), lead(vb_s), lead(lf_s))
```

```python
import functools

import jax
import jax.numpy as jnp
from jax import lax
from jax.experimental import pallas as pl
from jax.experimental.pallas import tpu as pltpu

F32 = jnp.float32
BF16 = jnp.bfloat16

HEAD_DIM = 64
RMS_EPS = 1e-6
N_MOD = 6
LANES = 128
NEG_BIG = -1e30

PRE_TILE = 256
Q_TILE = 1024
STRIP = 256
N_STRIPS = Q_TILE // STRIP
LOG2E = 1.4426950408889634
DEAD_LOG2 = -128.0
NORM_SLACK = 1.02
POST_TILE = 512
MOE_TILE = 1024
VMEM_LIMIT = 56 * 1024 * 1024


def _cparams(sem):
    return pltpu.CompilerParams(dimension_semantics=sem, vmem_limit_bytes=VMEM_LIMIT)


def _log_sigmoid(x):
    return jnp.minimum(x, 0.0) - jnp.log1p(jnp.exp(-jnp.abs(x)))


def _rms_mod(x, g, scale, shift):
    ms = jnp.mean(x * x, axis=-1, keepdims=True)
    y = x * lax.rsqrt(ms + RMS_EPS)
    return (y * g) * (1.0 + scale) + shift


def _mod(mods_ref, i, per_token):
    return mods_ref[i] if per_token else mods_ref[0, i:i + 1, :]


def _dot(a, b):
    return jnp.dot(a, b, preferred_element_type=F32)


def _split3(f):
    hi = f.astype(BF16).astype(F32)
    r = f - hi
    mid = r.astype(BF16).astype(F32)
    lo = (r - mid).astype(BF16).astype(F32)
    return hi, mid, lo


def _lane_cumsum(x):
    n = x.shape[1]
    lane = lax.broadcasted_iota(jnp.int32, x.shape, 1)
    d = 1
    while d < n:
        x = x + jnp.where(lane >= d, pltpu.roll(x, d, axis=1), 0.0)
        d *= 2
    return x


def _ada_kernel(c_ref, w_ref, b_ref, o_ref):
    c = c_ref[...]
    s = c * jax.nn.sigmoid(c)
    o_ref[...] = jnp.dot(s, w_ref[...], preferred_element_type=F32,
                         precision=lax.Precision.HIGHEST) + b_ref[...]


def _ada(c_all, w_ada, b_ada):
    n, d = c_all.shape
    nout = w_ada.shape[1]
    tn = 1024
    return pl.pallas_call(
        _ada_kernel,
        grid=(nout // tn,),
        in_specs=[pl.BlockSpec((n, d), lambda j: (0, 0)),
                  pl.BlockSpec((d, tn), lambda j: (0, j)),
                  pl.BlockSpec((1, tn), lambda j: (0, j))],
        out_specs=pl.BlockSpec((n, tn), lambda j: (0, j)),
        out_shape=jax.ShapeDtypeStruct((n, nout), F32),
        compiler_params=_cparams(("arbitrary",)),
        name="ada",
    )(c_all, w_ada, b_ada.reshape(1, nout))


def _pre_core(x, mods_ref, gmix_ref, win_ref, wg_ref, bg_ref, per_token):
    h = _rms_mod(x, gmix_ref[...], _mod(mods_ref, 1, per_token), _mod(mods_ref, 0, per_token))
    hb = h.astype(BF16)
    proj = _dot(hb, win_ref[...])
    gates = jax.nn.sigmoid(_dot(hb, wg_ref[...]) + bg_ref[...])
    return proj, gates.astype(BF16)


def _pre_prompt_kernel(x_ref, mods_ref, gmix_ref, win_ref, bf_ref, wg_ref, bg_ref, hsel_ref,
                       qsT_ref, ks_ref, vsT_ref, qfT_ref, kf_ref, vfT_ref,
                       ksl_ref, vsl_ref, kfl_ref, vfl_ref, logf_ref, kn2_ref, gates_ref,
                       carry_ref, *, n_heads):
    tm = x_ref.shape[1]
    w = n_heads * HEAD_DIM

    @pl.when(pl.program_id(1) == 0)
    def _():
        carry_ref[...] = jnp.zeros_like(carry_ref)

    proj, gates = _pre_core(x_ref[0], mods_ref, gmix_ref, win_ref, wg_ref, bg_ref, False)
    gates_ref[0] = gates
    qa, ka, va, qb, kb, vb = [proj[:, i * w:(i + 1) * w] for i in range(6)]
    fg = proj[:, 6 * w:6 * w + LANES]
    scale = HEAD_DIM ** -0.5 * LOG2E
    qaT = (qa * scale).T
    kaT = ka.T
    vaT = va.T
    qbT = (qb * scale).T
    kbT = kb.T
    vbT = vb.T
    kn2_ref[0] = _dot(hsel_ref[...], (kbT * kbT).astype(BF16))

    logfT = _log_sigmoid(fg + bf_ref[...]).T[0:n_heads, :]
    logf_ref[0] = logfT
    f = _lane_cumsum(logfT) + carry_ref[:, 0:1]
    carry_ref[...] = jnp.broadcast_to(f[:, tm - 1:tm], carry_ref.shape)
    f_hi, f_mid, f_lo = _split3(f * LOG2E)

    row64 = lax.broadcasted_iota(jnp.int32, (HEAD_DIM, tm), 0)
    row8 = lax.broadcasted_iota(jnp.int32, (8, tm), 0)
    zeros64 = jnp.zeros((HEAD_DIM, tm), BF16)
    ke_parts = []
    for h in range(n_heads):
        def bc(a, n):
            return jnp.broadcast_to(a[h:h + 1, :], (n, tm))
        qe = jnp.where(row64 < 3, 1.0,
                       jnp.where(row64 == 3, bc(f_hi, HEAD_DIM),
                                 jnp.where(row64 == 4, bc(f_mid, HEAD_DIM),
                                           jnp.where(row64 == 5, bc(f_lo, HEAD_DIM), 0.0))))
        sl = slice(h * HEAD_DIM, (h + 1) * HEAD_DIM)
        qfT_ref[0, h, 0:HEAD_DIM, :] = qbT[sl, :].astype(BF16)
        qfT_ref[0, h, HEAD_DIM:2 * HEAD_DIM, :] = qe.astype(BF16)
        qsT_ref[0, h, 0:HEAD_DIM, :] = qaT[sl, :].astype(BF16)
        qsT_ref[0, h, HEAD_DIM:2 * HEAD_DIM, :] = zeros64
        vsT_ref[0, h, 0] = vaT[sl, :].astype(BF16)
        vfT_ref[0, h, 0] = vbT[sl, :].astype(BF16)
        ksl_ref[0, h] = kaT[sl, :]
        vsl_ref[0, h] = vaT[sl, :]
        kfl_ref[0, h] = kbT[sl, :]
        vfl_ref[0, h] = vbT[sl, :]
        ke_parts.append(
            jnp.where(row8 == 0, -bc(f_hi, 8),
                      jnp.where(row8 == 1, -bc(f_mid, 8),
                                jnp.where(row8 == 2, -bc(f_lo, 8),
                                          jnp.where(row8 < 6, 1.0, 0.0)))))
    ke_parts.append(jnp.zeros((LANES - 8 * n_heads, tm), F32))
    ke = jnp.concatenate(ke_parts, axis=0).T

    lane = lax.broadcasted_iota(jnp.int32, (tm, LANES), 1)
    for h in range(n_heads):
        base = (h // 2) * LANES
        ka_slab = ka[:, base:base + LANES]
        kb_slab = kb[:, base:base + LANES]
        if h % 2:
            ka_slab = pltpu.roll(ka_slab, HEAD_DIM, axis=1)
            kb_slab = pltpu.roll(kb_slab, HEAD_DIM, axis=1)
        ext = pltpu.roll(ke, HEAD_DIM - 8 * h, axis=1)
        ks_ref[0, h] = jnp.where(lane < HEAD_DIM, ka_slab, 0.0).astype(BF16)
        kf_ref[0, h] = jnp.where(lane < HEAD_DIM, kb_slab,
                                 jnp.where(lane < HEAD_DIM + 8, ext, 0.0)).astype(BF16)


def _pre_prompt(x, mods, g_mix, w_in_p, b_f_p, w_gate, b_gate, hsel, n_heads):
    b, t, d = x.shape
    tm = PRE_TILE
    nt = t // tm
    h, hd = n_heads, HEAD_DIM
    const2 = lambda bi, ti: (0, 0)
    head_t = lambda bi, ti: (bi, 0, 0, ti)
    head_s = lambda bi, ti: (bi, 0, ti, 0)
    blk_t = lambda bi, ti: (bi, 0, ti, 0, 0)
    out_shape = (
        jax.ShapeDtypeStruct((b, h, 2 * hd, t), BF16),
        jax.ShapeDtypeStruct((b, h, t, 2 * hd), BF16),
        jax.ShapeDtypeStruct((b, h, nt, hd, tm), BF16),
        jax.ShapeDtypeStruct((b, h, 2 * hd, t), BF16),
        jax.ShapeDtypeStruct((b, h, t, 2 * hd), BF16),
        jax.ShapeDtypeStruct((b, h, nt, hd, tm), BF16),
        jax.ShapeDtypeStruct((b, h, hd, t), F32),
        jax.ShapeDtypeStruct((b, h, hd, t), F32),
        jax.ShapeDtypeStruct((b, h, hd, t), F32),
        jax.ShapeDtypeStruct((b, h, hd, t), F32),
        jax.ShapeDtypeStruct((b, h, t), F32),
        jax.ShapeDtypeStruct((b, h, t), F32),
        jax.ShapeDtypeStruct((b, t, w_gate.shape[1]), BF16),
    )
    out_specs = (
        pl.BlockSpec((1, h, 2 * hd, tm), head_t),
        pl.BlockSpec((1, h, tm, 2 * hd), head_s),
        pl.BlockSpec((1, h, 1, hd, tm), blk_t),
        pl.BlockSpec((1, h, 2 * hd, tm), head_t),
        pl.BlockSpec((1, h, tm, 2 * hd), head_s),
        pl.BlockSpec((1, h, 1, hd, tm), blk_t),
        pl.BlockSpec((1, h, hd, tm), head_t),
        pl.BlockSpec((1, h, hd, tm), head_t),
        pl.BlockSpec((1, h, hd, tm), head_t),
        pl.BlockSpec((1, h, hd, tm), head_t),
        pl.BlockSpec((1, h, tm), lambda bi, ti: (bi, 0, ti)),
        pl.BlockSpec((1, h, tm), lambda bi, ti: (bi, 0, ti)),
        pl.BlockSpec((1, tm, w_gate.shape[1]), lambda bi, ti: (bi, ti, 0)),
    )
    in_specs = [
        pl.BlockSpec((1, tm, d), lambda bi, ti: (bi, ti, 0)),
        pl.BlockSpec((1, N_MOD, d), lambda bi, ti: (bi, 0, 0)),
        pl.BlockSpec((1, d), const2),
        pl.BlockSpec(w_in_p.shape, const2),
        pl.BlockSpec((1, LANES), const2),
        pl.BlockSpec(w_gate.shape, const2),
        pl.BlockSpec((1, w_gate.shape[1]), const2),
        pl.BlockSpec(hsel.shape, const2),
    ]
    return pl.pallas_call(
        functools.partial(_pre_prompt_kernel, n_heads=n_heads),
        grid=(b, nt),
        in_specs=in_specs,
        out_specs=out_specs,
        out_shape=out_shape,
        scratch_shapes=[pltpu.VMEM((h, LANES), F32)],
        compiler_params=_cparams(("arbitrary", "arbitrary")),
        name="pre_prompt",
    )(x, mods, g_mix, w_in_p, b_f_p, w_gate, b_gate, hsel)


def _pre_sample_kernel(x_ref, mods_ref, gmix_ref, win_ref, bf_ref, wg_ref, bg_ref,
                       proj_ref, logf_ref, gates_ref, *, n_heads):
    w = n_heads * HEAD_DIM
    proj, gates = _pre_core(x_ref[...], mods_ref, gmix_ref, win_ref, wg_ref, bg_ref, True)
    proj_ref[...] = proj
    gates_ref[...] = gates
    logf_ref[...] = _log_sigmoid(proj[:, 6 * w:6 * w + LANES] + bf_ref[...])


def _pre_sample(x, mods_tok, g_mix, w_in_p, b_f_p, w_gate, b_gate, n_heads):
    n, d = x.shape
    full = lambda a: pl.BlockSpec(a.shape, lambda i: (0,) * a.ndim)
    args = (x, mods_tok, g_mix, w_in_p, b_f_p, w_gate, b_gate)
    out_shape = (jax.ShapeDtypeStruct((n, w_in_p.shape[1]), F32),
                 jax.ShapeDtypeStruct((n, LANES), F32),
                 jax.ShapeDtypeStruct((n, w_gate.shape[1]), BF16))
    return pl.pallas_call(
        functools.partial(_pre_sample_kernel, n_heads=n_heads),
        grid=(1,),
        in_specs=[full(a) for a in args],
        out_specs=tuple(pl.BlockSpec(s.shape, lambda i: (0, 0)) for s in out_shape),
        out_shape=out_shape,
        compiler_params=_cparams(("arbitrary",)),
        name="pre_sample",
    )(*args)


FULL, DIAG, SKIP = "full", "diag", "skip"


def _attn_parts(qT_ref, k_ref, vT_ref):
    assert qT_ref.shape[3] == N_STRIPS * STRIP and vT_ref.shape[4] == STRIP
    nk = vT_ref.shape[2]

    def qk(j, z_scr, strips=range(N_STRIPS)):
        k = k_ref[0, 0, pl.ds(pl.multiple_of(j * STRIP, STRIP), STRIP), :]
        for s in strips:
            z_scr[s] = _dot(k, qT_ref[0, 0, :, s * STRIP:(s + 1) * STRIP])

    clip = lambda j: jnp.clip(j, 0, nk - 1)
    row = lax.broadcasted_iota(jnp.int32, (STRIP, STRIP), 0)
    col = lax.broadcasted_iota(jnp.int32, (STRIP, STRIP), 1)
    return qk, clip, row, col


def _modes(r):
    return tuple(FULL if r < s else DIAG if r == s else SKIP for s in range(N_STRIPS))


def _sb_prompt_kernel(qT_ref, k_ref, vT_ref, ntri_ref, o_ref, z_a, z_b, w_a, w_b):
    qk, clip, row, col = _attn_parts(qT_ref, k_ref, vT_ref)
    base = pl.program_id(2) * N_STRIPS
    diag = row < col
    blk = lambda t: clip(base + N_STRIPS - 1 - t)
    sign = jnp.uint32(0x80000000)

    def pv(j, w_scr, carry, which, pending):
        vT = vT_ref[0, 0, j]
        return tuple(((cr[0], cr[1] + _dot(vT, w_scr[s]) * cr[2 + which]) + cr[2:]) if pending[s] else cr
                     for s, cr in enumerate(carry))

    def softplus_phase(z_scr, carry, modes, which):
        carry = list(carry)
        mids = []
        for s, mode in enumerate(modes):
            if mode == SKIP:
                continue
            c, acc, sa, sb = carry[s]
            z = z_scr[s]
            neg_abs = lax.bitcast_convert_type(lax.bitcast_convert_type(z, jnp.uint32) | sign, F32)
            sp_full = jnp.maximum(z, 0.0) + jnp.log2(1.0 + jnp.exp2(neg_abs))
            sp = jnp.where(diag, sp_full, 0.0) if mode == DIAG else sp_full
            sp_b = sp.astype(BF16)
            later = _dot(ntri_ref[...], sp_b)
            mids.append((s, mode, z - sp_full, later))
            scale = jnp.exp2(c)
            c = c + later[0:1, :] - sp_b[0:1, :].astype(F32)
            carry[s] = (c, acc, sa, scale) if which else (c, acc, scale, sb)
        return tuple(carry), mids

    def weight_phase(w_scr, mids):
        for s, mode, lsig, later in mids:
            wgt = jnp.exp2(lsig + later)
            if mode == DIAG:
                wgt = jnp.where(diag, wgt, 0.0)
            w_scr[s] = wgt.astype(BF16)

    all_strips = (True,) * N_STRIPS

    def pair(t, carry, modes_a, modes_b, pend_a=all_strips, pend_b=all_strips):
        carry = pv(blk(t - 2), w_a, carry, 0, pend_a)
        carry = pv(blk(t - 1), w_b, carry, 1, pend_b)
        qk(blk(t + 1), z_b)
        carry, mids_a = softplus_phase(z_a, carry, modes_a, 0)
        qk(blk(t + 2), z_a)
        carry, mids_b = softplus_phase(z_b, carry, modes_b, 1)
        weight_phase(w_a, mids_a)
        weight_phase(w_b, mids_b)
        return carry

    def alive(carry):
        return jnp.max(functools.reduce(jnp.maximum, [cr[0] for cr in carry])) > DEAD_LOG2

    qk(blk(0), z_a)
    one = jnp.ones((1, STRIP), F32)
    carry = ((jnp.zeros((1, STRIP), F32), jnp.zeros((HEAD_DIM, STRIP), F32), one, one),) * N_STRIPS
    pend_a = pend_b = (False,) * N_STRIPS
    for t in range(0, N_STRIPS, 2):
        modes_a, modes_b = _modes(N_STRIPS - 1 - t), _modes(N_STRIPS - 2 - t)
        carry = pair(t, carry, modes_a, modes_b, pend_a, pend_b)
        pend_a = tuple(p or m != SKIP for p, m in zip(pend_a, modes_a))
        pend_b = tuple(p or m != SKIP for p, m in zip(pend_b, modes_b))
    for s in range(N_STRIPS):
        if not pend_a[s]:
            w_a[s] = jnp.zeros((STRIP, STRIP), BF16)
        if not pend_b[s]:
            w_b[s] = jnp.zeros((STRIP, STRIP), BF16)
    full = (FULL,) * N_STRIPS

    def body(state):
        i, _, cr = state
        cr = pair(N_STRIPS + 2 * i, cr, full, full)
        return i + 1, alive(cr), cr

    n_pairs = base // 2
    i_end, _, carry = lax.while_loop(lambda st: jnp.logical_and(st[0] < n_pairs, st[1]), body,
                                     (jnp.int32(0), alive(carry), carry))
    last = N_STRIPS + 2 * i_end - 1
    carry = pv(blk(last - 1), w_a, carry, 0, all_strips)
    carry = pv(blk(last), w_b, carry, 1, all_strips)
    for s in range(N_STRIPS):
        o_ref[0, 0, :, s * STRIP:(s + 1) * STRIP] = carry[s][1].astype(BF16)


def _fox_prompt_kernel(qT_ref, k_ref, vT_ref, kn2_ref, o_ref, z_a, z_b, p_a, p_b):
    qk, clip, row, col = _attn_parts(qT_ref, k_ref, vT_ref)
    base = pl.program_id(2) * N_STRIPS
    diag = row <= col
    blk = lambda t: clip(base + N_STRIPS - 1 - t)

    def pv(j, p_scr, carry, which, pending):
        vT = vT_ref[0, 0, j]
        return tuple(((cr[0], cr[1], cr[3 + which] * cr[2] + _dot(vT, p_scr[s])) + cr[3:]) if pending[s] else cr
                     for s, cr in enumerate(carry))

    def valu(z_scr, p_scr, carry, modes, which):
        out = []
        for s, mode in enumerate(modes):
            m, l, acc, aa, ab = carry[s]
            if mode == SKIP:
                out.append(carry[s])
                continue

            def scores():
                sc = z_scr[s]
                return jnp.where(diag, sc, NEG_BIG) if mode == DIAG else sc

            m_new = jnp.maximum(m, jnp.max(scores(), axis=0, keepdims=True))
            alpha = jnp.exp2(m - m_new)
            p = jnp.exp2(scores() - m_new)
            p_scr[s] = p.astype(BF16)
            l = alpha * l + jnp.sum(p, axis=0, keepdims=True)
            out.append((m_new, l, acc, aa, alpha) if which else (m_new, l, acc, alpha, ab))
        return tuple(out)

    all_strips = (True,) * N_STRIPS

    def pair(t, carry, modes_a, modes_b, pend_a=all_strips, pend_b=all_strips):
        carry = pv(blk(t - 2), p_a, carry, 0, pend_a)
        carry = pv(blk(t - 1), p_b, carry, 1, pend_b)
        qk(blk(t + 1), z_b)
        carry = valu(z_a, p_a, carry, modes_a, 0)
        qk(blk(t + 2), z_a)
        return valu(z_b, p_b, carry, modes_b, 1)

    q_all = qT_ref[0, 0]
    q_f = q_all[0:HEAD_DIM, :].astype(F32)
    q_norm = jnp.sqrt(jnp.sum(q_f * q_f, axis=0, keepdims=True))
    f_rows = q_all[HEAD_DIM:HEAD_DIM + 16, :].astype(F32)
    f_q = f_rows[3:4, :] + f_rows[4:5, :] + f_rows[5:6, :]
    k_norm = jnp.sqrt(jnp.max(kn2_ref[0, 0], axis=1, keepdims=True))
    bound_q = NORM_SLACK * q_norm * k_norm + f_q

    def alive(carry, j_next):
        rows = k_ref[0, 0, pl.ds(pl.multiple_of(clip(j_next) * STRIP + STRIP - 16, 16), 16), :].astype(F32)
        f_end = -(rows[15:16, HEAD_DIM:HEAD_DIM + 1] + rows[15:16, HEAD_DIM + 1:HEAD_DIM + 2]
                  + rows[15:16, HEAD_DIM + 2:HEAD_DIM + 3])
        gaps = [bound_q[:, s * STRIP:(s + 1) * STRIP] - f_end - carry[s][0] for s in range(N_STRIPS)]
        return jnp.max(functools.reduce(jnp.maximum, gaps)) > DEAD_LOG2

    qk(blk(0), z_a)
    one = jnp.ones((1, STRIP), F32)
    carry = ((jnp.full((1, STRIP), NEG_BIG, F32), jnp.zeros((1, STRIP), F32),
              jnp.zeros((HEAD_DIM, STRIP), F32), one, one),) * N_STRIPS
    pend_a = pend_b = (False,) * N_STRIPS
    for t in range(0, N_STRIPS, 2):
        modes_a, modes_b = _modes(N_STRIPS - 1 - t), _modes(N_STRIPS - 2 - t)
        carry = pair(t, carry, modes_a, modes_b, pend_a, pend_b)
        pend_a = tuple(p or m != SKIP for p, m in zip(pend_a, modes_a))
        pend_b = tuple(p or m != SKIP for p, m in zip(pend_b, modes_b))
    for s in range(N_STRIPS):
        if not pend_a[s]:
            p_a[s] = jnp.zeros((STRIP, STRIP), BF16)
        if not pend_b[s]:
            p_b[s] = jnp.zeros((STRIP, STRIP), BF16)
    full = (FULL,) * N_STRIPS

    def body(state):
        i, _, cr = state
        cr = pair(N_STRIPS + 2 * i, cr, full, full)
        return i + 1, alive(cr, base - 3 - 2 * i), cr

    n_pairs = base // 2
    i_end, _, carry = lax.while_loop(lambda st: jnp.logical_and(st[0] < n_pairs, st[1]), body,
                                     (jnp.int32(0), alive(carry, base - 1), carry))
    last = N_STRIPS + 2 * i_end - 1
    carry = pv(blk(last - 1), p_a, carry, 0, all_strips)
    carry = pv(blk(last), p_b, carry, 1, all_strips)
    for s in range(N_STRIPS):
        _, l, acc, _, _ = carry[s]
        o_ref[0, 0, :, s * STRIP:(s + 1) * STRIP] = (acc / l).astype(BF16)


def _attn_prompt(kernel, qT, k, vT, extra=()):
    b, h, kd, t = qT.shape
    nk, tk = vT.shape[2], vT.shape[4]
    tq = Q_TILE
    in_specs = [
        pl.BlockSpec((1, 1, kd, tq), lambda bi, hi, qi: (bi, hi, 0, qi)),
        pl.BlockSpec((1, 1, t, kd), lambda bi, hi, qi: (bi, hi, 0, 0)),
        pl.BlockSpec((1, 1, nk, HEAD_DIM, tk), lambda bi, hi, qi: (bi, hi, 0, 0, 0)),
    ] + [pl.BlockSpec(a.shape, lambda bi, hi, qi: (0, 0)) if a.ndim == 2 else
         pl.BlockSpec((1, 1) + a.shape[2:], lambda bi, hi, qi: (bi, hi, 0, 0)) for a in extra]
    scores = pltpu.VMEM((N_STRIPS, STRIP, STRIP), F32)
    probs = pltpu.VMEM((N_STRIPS, STRIP, STRIP), BF16)
    return pl.pallas_call(
        kernel,
        grid=(b, h, t // tq),
        in_specs=in_specs,
        out_specs=pl.BlockSpec((1, 1, HEAD_DIM, tq), lambda bi, hi, qi: (bi, hi, 0, qi)),
        out_shape=jax.ShapeDtypeStruct((b, h, HEAD_DIM, t), BF16),
        scratch_shapes=[scores, scores, probs, probs],
        compiler_params=_cparams(("parallel", "parallel", "arbitrary")),
        name=kernel.__name__.strip("_"),
    )(qT, k, vT, *extra)


def _sample_kv(ck_ref, kn_ref, cv_ref, vn_ref, h):
    kT = jnp.concatenate([ck_ref[0, h], kn_ref[0, h]], axis=1).astype(BF16)
    vT = jnp.concatenate([cv_ref[0, h], vn_ref[0, h]], axis=1).astype(BF16)
    return kT, vT


def _pv(p, vT):
    return lax.dot_general(p, vT, (((1,), (1,)), ((), ())), preferred_element_type=F32)


def _sb_sample_kernel(q_ref, ck_ref, kn_ref, cv_ref, vn_ref, tri_ref, o_ref):
    n_heads, tq = q_ref.shape[1], q_ref.shape[2]
    p_len = ck_ref.shape[3]
    nk = p_len + kn_ref.shape[3]
    row = lax.broadcasted_iota(jnp.int32, (tq, nk), 0)
    col = lax.broadcasted_iota(jnp.int32, (tq, nk), 1)
    mask = col < row + p_len
    tri = tri_ref[...]
    for h in range(n_heads):
        kT, vT = _sample_kv(ck_ref, kn_ref, cv_ref, vn_ref, h)
        q = (q_ref[0, h] * HEAD_DIM ** -0.5).astype(BF16)
        z = _dot(q, kT)
        ls_full = -(jnp.maximum(z, 0.0) + jnp.log1p(jnp.exp(-jnp.abs(z))))
        ls = jnp.where(mask, ls_full, 0.0)
        later = _dot(ls.astype(BF16), tri)
        wgt = jnp.where(mask, jnp.exp(z + ls_full + later), 0.0)
        o_ref[0, :, h * HEAD_DIM:(h + 1) * HEAD_DIM] = _pv(wgt.astype(BF16), vT)


def _fox_sample_kernel(q_ref, ck_ref, kn_ref, cv_ref, vn_ref, lf_ref, o_ref):
    n_heads, tq = q_ref.shape[1], q_ref.shape[2]
    p_len = ck_ref.shape[3]
    nk = p_len + kn_ref.shape[3]
    row = lax.broadcasted_iota(jnp.int32, (tq, nk), 0)
    col = lax.broadcasted_iota(jnp.int32, (tq, nk), 1)
    mask = col <= row + p_len
    diag = col == row + p_len
    f_all = _lane_cumsum(lf_ref[0])
    for h in range(n_heads):
        kT, vT = _sample_kv(ck_ref, kn_ref, cv_ref, vn_ref, h)
        q = (q_ref[0, h] * HEAD_DIM ** -0.5).astype(BF16)
        fk = f_all[h:h + 1, :]
        fq = jnp.sum(jnp.where(diag, fk, 0.0), axis=1, keepdims=True)
        z = _dot(q, kT) + fq - fk
        z = jnp.where(mask, z, NEG_BIG)
        p = jnp.exp(z - jnp.max(z, axis=1, keepdims=True))
        l = jnp.sum(p, axis=1, keepdims=True)
        o_ref[0, :, h * HEAD_DIM:(h + 1) * HEAD_DIM] = _pv(p.astype(BF16), vT) / l


def _attn_sample(kernel, q, ck, kn, cv, vn, extra, extra_spec):
    b, h, tq, hd = q.shape
    per_b = lambda a: pl.BlockSpec((1,) + a.shape[1:], lambda bi: (bi,) + (0,) * (a.ndim - 1))
    return pl.pallas_call(
        kernel,
        grid=(b,),
        in_specs=[per_b(a) for a in (q, ck, kn, cv, vn)] + [extra_spec],
        out_specs=pl.BlockSpec((1, tq, h * hd), lambda bi: (bi, 0, 0)),
        out_shape=jax.ShapeDtypeStruct((b, tq, h * hd), F32),
        compiler_params=_cparams(("parallel",)),
        name=kernel.__name__.strip("_"),
    )(q, ck, kn, cv, vn, extra)


def _route(lt, n_groups, epg):
    n = lt.shape[1]
    g = [lt[i:i + 1, :] for i in range(n_groups)]
    gmax = functools.reduce(jnp.maximum, g)
    g_w = 1.0 / functools.reduce(jnp.add, [jnp.exp(gi - gmax) for gi in g])
    is_g, taken = [], None
    for gi in g:
        hit = gi >= gmax
        if taken is not None:
            hit = jnp.logical_and(hit, jnp.logical_not(taken))
        taken = hit if taken is None else jnp.logical_or(taken, hit)
        is_g.append(hit)
    le = [lt[n_groups + i:n_groups + i + 1, :] for i in range(n_groups * epg)]
    e_sel = []
    for i in range(epg):
        v = le[(n_groups - 1) * epg + i]
        for gi in reversed(range(n_groups - 1)):
            v = jnp.where(is_g[gi], le[gi * epg + i], v)
        e_sel.append(v)
    emax = functools.reduce(jnp.maximum, e_sel)
    pe = [jnp.exp(v - emax) for v in e_sel]
    pden = functools.reduce(jnp.add, pe)
    prob = [p / pden for p in pe]

    def first_argmax(vals):
        vmax = functools.reduce(jnp.maximum, vals)
        hits, tk = [], None
        for v in vals:
            hit = v >= vmax
            if tk is not None:
                hit = jnp.logical_and(hit, jnp.logical_not(tk))
            tk = hit if tk is None else jnp.logical_or(tk, hit)
            hits.append(hit)
        return vmax, hits

    p1, t1 = first_argmax(prob)
    p2, t2 = first_argmax([jnp.where(t, -1.0, p) for t, p in zip(t1, prob)])
    tot = p1 + p2
    w1 = g_w * (p1 / tot)
    w2 = g_w * (p2 / tot)
    rows = lax.broadcasted_iota(jnp.int32, (LANES, n), 0)
    comb = jnp.zeros((LANES, n), F32)
    for gi in range(n_groups):
        for i in range(epg):
            val = jnp.where(is_g[gi], jnp.where(t1[i], w1, 0.0) + jnp.where(t2[i], w2, 0.0), 0.0)
            comb = jnp.where(rows == gi * epg + i, jnp.broadcast_to(val, (LANES, n)), comb)
    return comb


def _post_kernel(yaT_ref, ybT_ref, gates_ref, x_ref, mods_ref, wba_ref, wbb_ref, wout_ref,
                 gmoe_ref, wr2_ref, wrhi_ref, br_ref, x1_ref, h2_ref, comb_ref,
                 *, per_token, n_groups, epg):
    d = x_ref.shape[2]
    tdot = lambda aT, w: lax.dot_general(aT, w, (((0,), (0,)), ((), ())), preferred_element_type=F32)
    ua = tdot(yaT_ref[0], wba_ref[...])
    ub = tdot(ybT_ref[0], wbb_ref[...])
    gates = gates_ref[0].astype(F32)
    mix = gates[:, :d] * ua + gates[:, d:] * ub
    x1 = x_ref[0] + _mod(mods_ref, 2, per_token) * _dot(mix.astype(BF16), wout_ref[...])
    x1_ref[0] = x1
    h2 = _rms_mod(x1, gmoe_ref[...], _mod(mods_ref, 4, per_token), _mod(mods_ref, 3, per_token))
    h2_hi = h2.astype(BF16)
    h2_ref[0] = h2_hi
    h2_lo = (h2 - h2_hi.astype(F32)).astype(BF16)
    a = _dot(h2_hi, wr2_ref[...])
    logits = a[:, :LANES] + a[:, LANES:] + _dot(h2_lo, wrhi_ref[...]) + br_ref[...]
    comb_ref[0] = _route(logits.T, n_groups, epg).T


def _post(yaT, ybT, gates, x, mods, w_ba, w_bb, w_out, g_moe, wr2, wrhi, br, per_token, n_groups, epg):
    b, t, d = x.shape
    tm = min(POST_TILE, t)
    w = yaT.shape[1]
    const2 = lambda bi, ti: (0, 0)
    tok = lambda last: pl.BlockSpec((1, tm, last), lambda bi, ti: (bi, ti, 0))
    chan = pl.BlockSpec((1, w, tm), lambda bi, ti: (bi, 0, ti))
    if per_token:
        mods_spec = pl.BlockSpec(mods.shape, lambda bi, ti: (0, 0, 0))
    else:
        mods_spec = pl.BlockSpec((1, N_MOD, d), lambda bi, ti: (bi, 0, 0))
    in_specs = [chan, chan, tok(2 * d), tok(d), mods_spec,
                pl.BlockSpec(w_ba.shape, const2), pl.BlockSpec(w_bb.shape, const2),
                pl.BlockSpec(w_out.shape, const2), pl.BlockSpec((1, d), const2),
                pl.BlockSpec(wr2.shape, const2), pl.BlockSpec(wrhi.shape, const2),
                pl.BlockSpec((1, LANES), const2)]
    out_shape = (jax.ShapeDtypeStruct((b, t, d), F32),
                 jax.ShapeDtypeStruct((b, t, d), BF16),
                 jax.ShapeDtypeStruct((b, t, LANES), F32))
    return pl.pallas_call(
        functools.partial(_post_kernel, per_token=per_token, n_groups=n_groups, epg=epg),
        grid=(b, t // tm),
        in_specs=in_specs,
        out_specs=(tok(d), tok(d), tok(LANES)),
        out_shape=out_shape,
        compiler_params=_cparams(("parallel", "parallel")),
        name="post_sample" if per_token else "post_prompt",
    )(yaT, ybT, gates, x, mods, w_ba, w_bb, w_out, g_moe, wr2, wrhi, br)


def _moe_kernel(h2_ref, comb_ref, x1_ref, mods_ref, w13_ref, w2_ref, gfin_ref, y_ref, acc_ref,
                *, per_token):
    e = pl.program_id(1)
    hid = w2_ref.shape[1]

    @pl.when(e == 0)
    def _():
        acc_ref[...] = jnp.zeros_like(acc_ref)

    a = _dot(h2_ref[...], w13_ref[0])
    a1 = a[:, :hid]
    act = (a1 * jax.nn.sigmoid(a1)) * a[:, hid:]
    comb = comb_ref[...]
    lane = lax.broadcasted_iota(jnp.int32, comb.shape, 1)
    cw = jnp.sum(jnp.where(lane == e, comb, 0.0), axis=1, keepdims=True)
    acc_ref[...] += _dot((act * cw).astype(BF16), w2_ref[0])

    @pl.when(e == pl.num_programs(1) - 1)
    def _():
        x2 = x1_ref[...] + _mod(mods_ref, 5, per_token) * acc_ref[...]
        ms = jnp.mean(x2 * x2, axis=-1, keepdims=True)
        y_ref[...] = (x2 * lax.rsqrt(ms + RMS_EPS)) * gfin_ref[...]


def _moe(h2, comb, x1, mods, w13, w2, g_final, per_token, tokens_per_batch):
    n, d = x1.shape
    tm = min(MOE_TILE, n)
    n_exp = w13.shape[0]
    tiles_per_batch = tokens_per_batch // tm if not per_token else 1
    tok = lambda last: pl.BlockSpec((tm, last), lambda i, e: (i, 0))
    if per_token:
        mods_spec = pl.BlockSpec(mods.shape, lambda i, e: (0, 0, 0))
    else:
        mods_spec = pl.BlockSpec((1, N_MOD, d), lambda i, e: (i // tiles_per_batch, 0, 0))
    return pl.pallas_call(
        functools.partial(_moe_kernel, per_token=per_token),
        grid=(n // tm, n_exp),
        in_specs=[tok(d), tok(LANES), tok(d), mods_spec,
                  pl.BlockSpec((1,) + w13.shape[1:], lambda i, e: (e, 0, 0)),
                  pl.BlockSpec((1,) + w2.shape[1:], lambda i, e: (e, 0, 0)),
                  pl.BlockSpec((1, d), lambda i, e: (0, 0))],
        out_specs=tok(d),
        out_shape=jax.ShapeDtypeStruct((n, d), F32),
        scratch_shapes=[pltpu.VMEM((tm, d), F32)],
        compiler_params=_cparams(("parallel", "arbitrary")),
        name="moe_sample" if per_token else "moe_prompt",
    )(h2, comb, x1, mods, w13, w2, g_final)


def kernel(x_prompt, x_sample, cache_sb_k, cache_sb_v, cache_fox_k, cache_fox_v, cache_fox_logf,
           c_prompt, c_sample, w_ada, b_ada, g_mix, w_in, b_f, w_ba, w_bb, w_gate, b_gate, w_out,
           g_moe, w_rg, b_rg, w_re, b_re, w1, w3, w2, g_final):
    depth = w_ada.shape[0]
    assert depth == 1, "single-layer trunk"
    bp, t, d = x_prompt.shape
    bs, ts, _ = x_sample.shape
    n_heads = cache_sb_k.shape[2]
    p_len = cache_sb_k.shape[3]
    assert cache_fox_k.shape[2] == n_heads and n_heads * HEAD_DIM * 6 + n_heads == w_in.shape[2]
    assert n_heads == 8 and t % Q_TILE == 0 and Q_TILE % PRE_TILE == 0
    n_groups = w_rg.shape[2]
    n_exp = w_re.shape[2]
    epg = n_exp // n_groups
    assert n_groups + n_exp <= LANES
    w = n_heads * HEAD_DIM
    ns = bs * ts

    w_in_p = jnp.pad(w_in[0], ((0, 0), (0, LANES - n_heads))).astype(BF16)
    b_f_p = jnp.pad(b_f[0], (0, LANES - n_heads)).reshape(1, LANES)
    w_gate_b = w_gate[0].astype(BF16)
    b_gate_r = b_gate[0].reshape(1, -1)
    g_mix_r = g_mix[0].reshape(1, d)
    g_moe_r = g_moe[0].reshape(1, d)
    g_fin_r = g_final.reshape(1, d)
    w_ba_b, w_bb_b, w_out_b = w_ba[0].astype(BF16), w_bb[0].astype(BF16), w_out[0].astype(BF16)
    w_r = jnp.pad(jnp.concatenate([w_rg[0], w_re[0]], axis=1), ((0, 0), (0, LANES - n_groups - n_exp)))
    w_r_hi = w_r.astype(BF16)
    w_r_lo = (w_r - w_r_hi.astype(F32)).astype(BF16)
    wr2 = jnp.concatenate([w_r_hi, w_r_lo], axis=1)
    b_r = jnp.pad(jnp.concatenate([b_rg[0], b_re[0]]), (0, LANES - n_groups - n_exp)).reshape(1, LANES)
    w13 = jnp.concatenate([w1[0], w3[0]], axis=2).astype(BF16)
    w2_b = w2[0].astype(BF16)

    ada = _ada(jnp.concatenate([c_prompt, c_sample], axis=0), w_ada[0], b_ada[0])
    mods_p = ada[:bp].reshape(bp, N_MOD, d)
    mods_s = jnp.repeat(ada[bp:].reshape(bs, N_MOD, d).transpose(1, 0, 2), ts, axis=1)

    hsel = jnp.repeat(jnp.eye(n_heads, dtype=BF16), HEAD_DIM, axis=1)
    (qsT, ks, vsT, qfT, kf, vfT, ksl, vsl, kfl, vfl, logf_p, kn2, gates_p) = _pre_prompt(
        x_prompt, mods_p, g_mix_r, w_in_p, b_f_p, w_gate_b, b_gate_r, hsel, n_heads)
    tk = PRE_TILE
    ids = jnp.arange(tk)
    ntri_p = -(ids[None, :] > ids[:, None]).astype(BF16)
    yaT = _attn_prompt(_sb_prompt_kernel, qsT, ks, vsT, (ntri_p,)).reshape(bp, w, t)
    ybT = _attn_prompt(_fox_prompt_kernel, qfT, kf, vfT, (kn2[:, :, None, :],)).reshape(bp, w, t)
    x1_p, h2_p, comb_p = _post(yaT, ybT, gates_p, x_prompt, mods_p, w_ba_b, w_bb_b, w_out_b,
                               g_moe_r, wr2, w_r_hi, b_r, False, n_groups, epg)
    y_prompt = _moe(h2_p.reshape(bp * t, d), comb_p.reshape(bp * t, LANES), x1_p.reshape(bp * t, d),
                    mods_p, w13, w2_b, g_fin_r, False, t).reshape(bp, t, d)

    proj_s, logf_s, gates_s = _pre_sample(x_sample.reshape(ns, d), mods_s, g_mix_r, w_in_p, b_f_p,
                                          w_gate_b, b_gate_r, n_heads)
    heads = lambda i: proj_s[:, i * w:(i + 1) * w].reshape(bs, ts, n_heads, HEAD_DIM).transpose(0, 2, 1, 3)
    qa_s, ka_s, va_s, qb_s, kb_s, vb_s = [heads(i) for i in range(6)]
    lf_s = logf_s[:, :n_heads].reshape(bs, ts, n_heads).transpose(0, 2, 1)
    pad_k = lambda a: jnp.pad(jnp.swapaxes(a, 2, 3), ((0, 0), (0, 0), (0, 0), (0, LANES - ts)))
    chan = lambda cache: jnp.swapaxes(cache[0], 2, 3)
    nk = p_len + LANES
    ids = jnp.arange(nk)
    tri_s = (ids[:, None] > ids[None, :]).astype(BF16)
    lf_all = jnp.concatenate([cache_fox_logf[0], jnp.pad(lf_s, ((0, 0), (0, 0), (0, LANES - ts)))], axis=2)
    ya_s = _attn_sample(_sb_sample_kernel, qa_s, chan(cache_sb_k), pad_k(ka_s), chan(cache_sb_v), pad_k(va_s),
                        tri_s, pl.BlockSpec(tri_s.shape, lambda bi: (0, 0)))
    yb_s = _attn_sample(_fox_sample_kernel, qb_s, chan(cache_fox_k), pad_k(kb_s), chan(cache_fox_v), pad_k(vb_s),
                        lf_all, pl.BlockSpec((1, n_heads, nk), lambda bi: (bi, 0, 0)))
    to_chan = lambda y: y.reshape(ns, w).T.astype(BF16)[None]
    x1_s, h2_s, comb_s = _post(to_chan(ya_s), to_chan(yb_s), gates_s[None], x_sample.reshape(1, ns, d),
                               mods_s, w_ba_b, w_bb_b, w_out_b, g_moe_r, wr2, w_r_hi, b_r,
                               True, n_groups, epg)
    y_sample = _moe(h2_s[0], comb_s[0], x1_s[0], mods_s, w13, w2_b, g_fin_r, True, ns).reshape(bs, ts, d)

    lead = lambda a: a[None]
    tok_major = lambda a: jnp.swapaxes(a, 2, 3)[None]
    return (y_prompt, y_sample,
            tok_major(ksl), tok_major(vsl), tok_major(kfl), tok_major(vfl), lead(logf_p),
            lead(ka_s), lead(va_s), lead(kb_s), lead(vb_s), lead(lf_s))
```

```python
import functools

import jax
import jax.numpy as jnp
from jax import lax
from jax.experimental import pallas as pl
from jax.experimental.pallas import tpu as pltpu

F32 = jnp.float32
BF16 = jnp.bfloat16

HEAD_DIM = 64
RMS_EPS = 1e-6
N_MOD = 6
LANES = 128
NEG_BIG = -1e30

PRE_TILE = 256
Q_TILE = 1024
STRIP = 256
N_STRIPS = Q_TILE // STRIP
LOG2E = 1.4426950408889634
DEAD_LOG2 = -128.0
NORM_SLACK = 1.02
POST_TILE = 512
MOE_TILE = 1024
VMEM_LIMIT = 56 * 1024 * 1024


def _cparams(sem):
    return pltpu.CompilerParams(dimension_semantics=sem, vmem_limit_bytes=VMEM_LIMIT)


def _log_sigmoid(x):
    return jnp.minimum(x, 0.0) - jnp.log1p(jnp.exp(-jnp.abs(x)))


def _rms_mod(x, g, scale, shift):
    ms = jnp.mean(x * x, axis=-1, keepdims=True)
    y = x * lax.rsqrt(ms + RMS_EPS)
    return (y * g) * (1.0 + scale) + shift


def _mod(mods_ref, i, per_token):
    return mods_ref[i] if per_token else mods_ref[0, i:i + 1, :]


def _dot(a, b):
    return jnp.dot(a, b, preferred_element_type=F32)


def _split3(f):
    hi = f.astype(BF16).astype(F32)
    r = f - hi
    mid = r.astype(BF16).astype(F32)
    lo = (r - mid).astype(BF16).astype(F32)
    return hi, mid, lo


def _lane_cumsum(x):
    n = x.shape[1]
    lane = lax.broadcasted_iota(jnp.int32, x.shape, 1)
    d = 1
    while d < n:
        x = x + jnp.where(lane >= d, pltpu.roll(x, d, axis=1), 0.0)
        d *= 2
    return x


def _ada_kernel(c_ref, w_ref, b_ref, o_ref):
    c = c_ref[...]
    s = c * jax.nn.sigmoid(c)
    o_ref[...] = jnp.dot(s, w_ref[...], preferred_element_type=F32,
                         precision=lax.Precision.HIGHEST) + b_ref[...]


def _ada(c_all, w_ada, b_ada):
    n, d = c_all.shape
    nout = w_ada.shape[1]
    tn = 1024
    return pl.pallas_call(
        _ada_kernel,
        grid=(nout // tn,),
        in_specs=[pl.BlockSpec((n, d), lambda j: (0, 0)),
                  pl.BlockSpec((d, tn), lambda j: (0, j)),
                  pl.BlockSpec((1, tn), lambda j: (0, j))],
        out_specs=pl.BlockSpec((n, tn), lambda j: (0, j)),
        out_shape=jax.ShapeDtypeStruct((n, nout), F32),
        compiler_params=_cparams(("arbitrary",)),
        name="ada",
    )(c_all, w_ada, b_ada.reshape(1, nout))


def _pre_core(x, mods_ref, gmix_ref, win_ref, wg_ref, bg_ref, per_token):
    h = _rms_mod(x, gmix_ref[...], _mod(mods_ref, 1, per_token), _mod(mods_ref, 0, per_token))
    hb = h.astype(BF16)
    proj = _dot(hb, win_ref[...])
    gates = jax.nn.sigmoid(_dot(hb, wg_ref[...]) + bg_ref[...])
    return proj, gates.astype(BF16)


def _pre_prompt_kernel(x_ref, mods_ref, gmix_ref, win_ref, bf_ref, wg_ref, bg_ref, hsel_ref,
                       qsT_ref, ks_ref, vsT_ref, qfT_ref, kf_ref, vfT_ref,
                       ksl_ref, vsl_ref, kfl_ref, vfl_ref, logf_ref, kn2_ref, gates_ref,
                       carry_ref, *, n_heads):
    tm = x_ref.shape[1]
    w = n_heads * HEAD_DIM

    @pl.when(pl.program_id(1) == 0)
    def _():
        carry_ref[...] = jnp.zeros_like(carry_ref)

    proj, gates = _pre_core(x_ref[0], mods_ref, gmix_ref, win_ref, wg_ref, bg_ref, False)
    gates_ref[0] = gates
    qa, ka, va, qb, kb, vb = [proj[:, i * w:(i + 1) * w] for i in range(6)]
    fg = proj[:, 6 * w:6 * w + LANES]
    scale = HEAD_DIM ** -0.5 * LOG2E
    qaT = (qa * scale).T
    kaT = ka.T
    vaT = va.T
    qbT = (qb * scale).T
    kbT = kb.T
    vbT = vb.T
    kn2_ref[0] = _dot(hsel_ref[...], (kbT * kbT).astype(BF16))

    logfT = _log_sigmoid(fg + bf_ref[...]).T[0:n_heads, :]
    logf_ref[0] = logfT
    f = _lane_cumsum(logfT) + carry_ref[:, 0:1]
    carry_ref[...] = jnp.broadcast_to(f[:, tm - 1:tm], carry_ref.shape)
    f_hi, f_mid, f_lo = _split3(f * LOG2E)

    row64 = lax.broadcasted_iota(jnp.int32, (HEAD_DIM, tm), 0)
    row8 = lax.broadcasted_iota(jnp.int32, (8, tm), 0)
    zeros64 = jnp.zeros((HEAD_DIM, tm), BF16)
    ke_parts = []
    for h in range(n_heads):
        def bc(a, n):
            return jnp.broadcast_to(a[h:h + 1, :], (n, tm))
        qe = jnp.where(row64 < 3, 1.0,
                       jnp.where(row64 == 3, bc(f_hi, HEAD_DIM),
                                 jnp.where(row64 == 4, bc(f_mid, HEAD_DIM),
                                           jnp.where(row64 == 5, bc(f_lo, HEAD_DIM), 0.0))))
        sl = slice(h * HEAD_DIM, (h + 1) * HEAD_DIM)
        qfT_ref[0, h, 0:HEAD_DIM, :] = qbT[sl, :].astype(BF16)
        qfT_ref[0, h, HEAD_DIM:2 * HEAD_DIM, :] = qe.astype(BF16)
        qsT_ref[0, h, 0:HEAD_DIM, :] = qaT[sl, :].astype(BF16)
        qsT_ref[0, h, HEAD_DIM:2 * HEAD_DIM, :] = zeros64
        vsT_ref[0, h, 0] = vaT[sl, :].astype(BF16)
        vfT_ref[0, h, 0] = vbT[sl, :].astype(BF16)
        ksl_ref[0, h] = kaT[sl, :]
        vsl_ref[0, h] = vaT[sl, :]
        kfl_ref[0, h] = kbT[sl, :]
        vfl_ref[0, h] = vbT[sl, :]
        ke_parts.append(
            jnp.where(row8 == 0, -bc(f_hi, 8),
                      jnp.where(row8 == 1, -bc(f_mid, 8),
                                jnp.where(row8 == 2, -bc(f_lo, 8),
                                          jnp.where(row8 < 6, 1.0, 0.0)))))
    ke_parts.append(jnp.zeros((LANES - 8 * n_heads, tm), F32))
    ke = jnp.concatenate(ke_parts, axis=0).T

    lane = lax.broadcasted_iota(jnp.int32, (tm, LANES), 1)
    for h in range(n_heads):
        base = (h // 2) * LANES
        ka_slab = ka[:, base:base + LANES]
        kb_slab = kb[:, base:base + LANES]
        if h % 2:
            ka_slab = pltpu.roll(ka_slab, HEAD_DIM, axis=1)
            kb_slab = pltpu.roll(kb_slab, HEAD_DIM, axis=1)
        ext = pltpu.roll(ke, HEAD_DIM - 8 * h, axis=1)
        ks_ref[0, h] = jnp.where(lane < HEAD_DIM, ka_slab, 0.0).astype(BF16)
        kf_ref[0, h] = jnp.where(lane < HEAD_DIM, kb_slab,
                                 jnp.where(lane < HEAD_DIM + 8, ext, 0.0)).astype(BF16)


def _pre_prompt(x, mods, g_mix, w_in_p, b_f_p, w_gate, b_gate, hsel, n_heads):
    b, t, d = x.shape
    tm = PRE_TILE
    nt = t // tm
    h, hd = n_heads, HEAD_DIM
    const2 = lambda bi, ti: (0, 0)
    head_t = lambda bi, ti: (bi, 0, 0, ti)
    head_s = lambda bi, ti: (bi, 0, ti, 0)
    blk_t = lambda bi, ti: (bi, 0, ti, 0, 0)
    out_shape = (
        jax.ShapeDtypeStruct((b, h, 2 * hd, t), BF16),
        jax.ShapeDtypeStruct((b, h, t, 2 * hd), BF16),
        jax.ShapeDtypeStruct((b, h, nt, hd, tm), BF16),
        jax.ShapeDtypeStruct((b, h, 2 * hd, t), BF16),
        jax.ShapeDtypeStruct((b, h, t, 2 * hd), BF16),
        jax.ShapeDtypeStruct((b, h, nt, hd, tm), BF16),
        jax.ShapeDtypeStruct((b, h, hd, t), F32),
        jax.ShapeDtypeStruct((b, h, hd, t), F32),
        jax.ShapeDtypeStruct((b, h, hd, t), F32),
        jax.ShapeDtypeStruct((b, h, hd, t), F32),
        jax.ShapeDtypeStruct((b, h, t), F32),
        jax.ShapeDtypeStruct((b, h, t), F32),
        jax.ShapeDtypeStruct((b, t, w_gate.shape[1]), BF16),
    )
    out_specs = (
        pl.BlockSpec((1, h, 2 * hd, tm), head_t),
        pl.BlockSpec((1, h, tm, 2 * hd), head_s),
        pl.BlockSpec((1, h, 1, hd, tm), blk_t),
        pl.BlockSpec((1, h, 2 * hd, tm), head_t),
        pl.BlockSpec((1, h, tm, 2 * hd), head_s),
        pl.BlockSpec((1, h, 1, hd, tm), blk_t),
        pl.BlockSpec((1, h, hd, tm), head_t),
        pl.BlockSpec((1, h, hd, tm), head_t),
        pl.BlockSpec((1, h, hd, tm), head_t),
        pl.BlockSpec((1, h, hd, tm), head_t),
        pl.BlockSpec((1, h, tm), lambda bi, ti: (bi, 0, ti)),
        pl.BlockSpec((1, h, tm), lambda bi, ti: (bi, 0, ti)),
        pl.BlockSpec((1, tm, w_gate.shape[1]), lambda bi, ti: (bi, ti, 0)),
    )
    in_specs = [
        pl.BlockSpec((1, tm, d), lambda bi, ti: (bi, ti, 0)),
        pl.BlockSpec((1, N_MOD, d), lambda bi, ti: (bi, 0, 0)),
        pl.BlockSpec((1, d), const2),
        pl.BlockSpec(w_in_p.shape, const2),
        pl.BlockSpec((1, LANES), const2),
        pl.BlockSpec(w_gate.shape, const2),
        pl.BlockSpec((1, w_gate.shape[1]), const2),
        pl.BlockSpec(hsel.shape, const2),
    ]
    return pl.pallas_call(
        functools.partial(_pre_prompt_kernel, n_heads=n_heads),
        grid=(b, nt),
        in_specs=in_specs,
        out_specs=out_specs,
        out_shape=out_shape,
        scratch_shapes=[pltpu.VMEM((h, LANES), F32)],
        compiler_params=_cparams(("arbitrary", "arbitrary")),
        name="pre_prompt",
    )(x, mods, g_mix, w_in_p, b_f_p, w_gate, b_gate, hsel)


def _pre_sample_kernel(x_ref, mods_ref, gmix_ref, win_ref, bf_ref, wg_ref, bg_ref,
                       proj_ref, logf_ref, gates_ref, *, n_heads):
    w = n_heads * HEAD_DIM
    proj, gates = _pre_core(x_ref[...], mods_ref, gmix_ref, win_ref, wg_ref, bg_ref, True)
    proj_ref[...] = proj
    gates_ref[...] = gates
    logf_ref[...] = _log_sigmoid(proj[:, 6 * w:6 * w + LANES] + bf_ref[...])


def _pre_sample(x, mods_tok, g_mix, w_in_p, b_f_p, w_gate, b_gate, n_heads):
    n, d = x.shape
    full = lambda a: pl.BlockSpec(a.shape, lambda i: (0,) * a.ndim)
    args = (x, mods_tok, g_mix, w_in_p, b_f_p, w_gate, b_gate)
    out_shape = (jax.ShapeDtypeStruct((n, w_in_p.shape[1]), F32),
                 jax.ShapeDtypeStruct((n, LANES), F32),
                 jax.ShapeDtypeStruct((n, w_gate.shape[1]), BF16))
    return pl.pallas_call(
        functools.partial(_pre_sample_kernel, n_heads=n_heads),
        grid=(1,),
        in_specs=[full(a) for a in args],
        out_specs=tuple(pl.BlockSpec(s.shape, lambda i: (0, 0)) for s in out_shape),
        out_shape=out_shape,
        compiler_params=_cparams(("arbitrary",)),
        name="pre_sample",
    )(*args)


def _attn_parts(qT_ref, k_ref, vT_ref):
    assert qT_ref.shape[3] == N_STRIPS * STRIP and vT_ref.shape[4] == STRIP
    nk = vT_ref.shape[2]
    q_strip = lambda s: qT_ref[0, 0, :, s * STRIP:(s + 1) * STRIP]
    k_block = lambda j: k_ref[0, 0, pl.ds(pl.multiple_of(j * STRIP, STRIP), STRIP), :]

    def qk(j, z_scr):
        k = k_block(j)
        for s in range(N_STRIPS):
            z_scr[s] = _dot(k, q_strip(s))

    clip = lambda j: jnp.clip(j, 0, nk - 1)
    row = lax.broadcasted_iota(jnp.int32, (STRIP, STRIP), 0)
    col = lax.broadcasted_iota(jnp.int32, (STRIP, STRIP), 1)
    diag_subs = [(s, d) for d in range(N_STRIPS) for s in range(d, N_STRIPS)]
    return qk, q_strip, k_block, clip, row, col, diag_subs


def _sb_prompt_kernel(qT_ref, k_ref, vT_ref, ntri_ref, o_ref, z_a, z_b, w_a, w_b):
    qk, q_strip, k_block, clip, row, col, diag_subs = _attn_parts(qT_ref, k_ref, vT_ref)
    base = pl.program_id(2) * N_STRIPS
    diag = row < col
    blk = lambda t: clip(base + N_STRIPS - 1 - t)
    sign = jnp.uint32(0x80000000)

    def softplus2(z):
        neg_abs = lax.bitcast_convert_type(lax.bitcast_convert_type(z, jnp.uint32) | sign, F32)
        return jnp.maximum(z, 0.0) + jnp.log2(1.0 + jnp.exp2(neg_abs))

    def suffix(sp):
        sp_b = sp.astype(BF16)
        later = _dot(ntri_ref[...], sp_b)
        return later, later[0:1, :] - sp_b[0:1, :].astype(F32)

    w_a[...] = jnp.zeros(w_a.shape, BF16)
    w_b[...] = jnp.zeros(w_b.shape, BF16)
    zs = [_dot(k_block(base + s - d), q_strip(s)) for s, d in diag_subs]
    qk(blk(N_STRIPS), z_a)
    c = [jnp.zeros((1, STRIP), F32)] * N_STRIPS
    acc = [jnp.zeros((HEAD_DIM, STRIP), F32)] * N_STRIPS
    mids = []
    for (s, d), z in zip(diag_subs, zs):
        sp_full = softplus2(z)
        later, total = suffix(jnp.where(diag, sp_full, 0.0) if d == 0 else sp_full)
        mids.append((z - sp_full, later, c[s]))
        c[s] = c[s] + total
    for (s, d), (lsig, later, c_before) in zip(diag_subs, mids):
        wgt = jnp.exp2(lsig + later)
        if d == 0:
            wgt = jnp.where(diag, wgt, 0.0)
        acc[s] = acc[s] + _dot(vT_ref[0, 0, base + s - d], wgt.astype(BF16)) * jnp.exp2(c_before)

    def pv(j, w_scr, carry, which):
        vT = vT_ref[0, 0, j]
        return tuple((cr[0], cr[1] + _dot(vT, w_scr[s]) * cr[2 + which]) + cr[2:]
                     for s, cr in enumerate(carry))

    def softplus_phase(z_scr, carry, which):
        carry = list(carry)
        mids = []
        for s in range(N_STRIPS):
            c, acc, sa, sb = carry[s]
            z = z_scr[s]
            sp_full = softplus2(z)
            later, total = suffix(sp_full)
            mids.append((z - sp_full, later))
            scale = jnp.exp2(c)
            carry[s] = (c + total, acc, sa, scale) if which else (c + total, acc, scale, sb)
        return tuple(carry), mids

    def weight_phase(w_scr, mids):
        for s, (lsig, later) in enumerate(mids):
            w_scr[s] = jnp.exp2(lsig + later).astype(BF16)

    def pair(t, carry):
        carry = pv(blk(t - 2), w_a, carry, 0)
        carry = pv(blk(t - 1), w_b, carry, 1)
        qk(blk(t + 1), z_b)
        carry, mids_a = softplus_phase(z_a, carry, 0)
        qk(blk(t + 2), z_a)
        carry, mids_b = softplus_phase(z_b, carry, 1)
        weight_phase(w_a, mids_a)
        weight_phase(w_b, mids_b)
        return carry

    def alive(carry):
        return jnp.max(functools.reduce(jnp.maximum, [cr[0] for cr in carry])) > DEAD_LOG2

    def body(state):
        i, _, cr = state
        cr = pair(N_STRIPS + 2 * i, cr)
        return i + 1, alive(cr), cr

    one = jnp.ones((1, STRIP), F32)
    carry = tuple((c[s], acc[s], one, one) for s in range(N_STRIPS))
    n_pairs = base // 2
    i_end, _, carry = lax.while_loop(lambda st: jnp.logical_and(st[0] < n_pairs, st[1]), body,
                                     (jnp.int32(0), alive(carry), carry))
    last = N_STRIPS + 2 * i_end - 1
    carry = pv(blk(last - 1), w_a, carry, 0)
    carry = pv(blk(last), w_b, carry, 1)
    for s in range(N_STRIPS):
        o_ref[0, 0, :, s * STRIP:(s + 1) * STRIP] = carry[s][1].astype(BF16)


def _fox_prompt_kernel(qT_ref, k_ref, vT_ref, kn2_ref, o_ref, z_a, z_b, p_a, p_b):
    qk, q_strip, k_block, clip, row, col, diag_subs = _attn_parts(qT_ref, k_ref, vT_ref)
    base = pl.program_id(2) * N_STRIPS
    diag = row <= col
    blk = lambda t: clip(base + N_STRIPS - 1 - t)

    p_a[...] = jnp.zeros(p_a.shape, BF16)
    p_b[...] = jnp.zeros(p_b.shape, BF16)
    zs = [_dot(k_block(base + s - d), q_strip(s)) for s, d in diag_subs]
    qk(blk(N_STRIPS), z_a)
    m = [jnp.full((1, STRIP), NEG_BIG, F32)] * N_STRIPS
    l = [jnp.zeros((1, STRIP), F32)] * N_STRIPS
    acc = [jnp.zeros((HEAD_DIM, STRIP), F32)] * N_STRIPS
    for (s, d), sc in zip(diag_subs, zs):
        if d == 0:
            sc = jnp.where(diag, sc, NEG_BIG)
        m_new = jnp.maximum(m[s], jnp.max(sc, axis=0, keepdims=True))
        alpha = jnp.exp2(m[s] - m_new)
        p = jnp.exp2(sc - m_new)
        l[s] = alpha * l[s] + jnp.sum(p, axis=0, keepdims=True)
        acc[s] = alpha * acc[s] + _dot(vT_ref[0, 0, base + s - d], p.astype(BF16))
        m[s] = m_new

    def pv(j, p_scr, carry, which):
        vT = vT_ref[0, 0, j]
        return tuple((cr[0], cr[1], cr[3 + which] * cr[2] + _dot(vT, p_scr[s])) + cr[3:]
                     for s, cr in enumerate(carry))

    def valu(z_scr, p_scr, carry, which):
        out = []
        for s in range(N_STRIPS):
            m, l, acc, aa, ab = carry[s]
            m_new = jnp.maximum(m, jnp.max(z_scr[s], axis=0, keepdims=True))
            alpha = jnp.exp2(m - m_new)
            p = jnp.exp2(z_scr[s] - m_new)
            p_scr[s] = p.astype(BF16)
            l = alpha * l + jnp.sum(p, axis=0, keepdims=True)
            out.append((m_new, l, acc, aa, alpha) if which else (m_new, l, acc, alpha, ab))
        return tuple(out)

    def pair(t, carry):
        carry = pv(blk(t - 2), p_a, carry, 0)
        carry = pv(blk(t - 1), p_b, carry, 1)
        qk(blk(t + 1), z_b)
        carry = valu(z_a, p_a, carry, 0)
        qk(blk(t + 2), z_a)
        return valu(z_b, p_b, carry, 1)

    q_all = qT_ref[0, 0]
    q_f = q_all[0:HEAD_DIM, :].astype(F32)
    q_norm = jnp.sqrt(jnp.sum(q_f * q_f, axis=0, keepdims=True))
    f_rows = q_all[HEAD_DIM:HEAD_DIM + 16, :].astype(F32)
    f_q = f_rows[3:4, :] + f_rows[4:5, :] + f_rows[5:6, :]
    k_norm = jnp.sqrt(jnp.max(kn2_ref[0, 0], axis=1, keepdims=True))
    bound_q = NORM_SLACK * q_norm * k_norm + f_q

    def alive(carry, j_next):
        rows = k_ref[0, 0, pl.ds(pl.multiple_of(clip(j_next) * STRIP + STRIP - 16, 16), 16), :].astype(F32)
        f_end = -(rows[15:16, HEAD_DIM:HEAD_DIM + 1] + rows[15:16, HEAD_DIM + 1:HEAD_DIM + 2]
                  + rows[15:16, HEAD_DIM + 2:HEAD_DIM + 3])
        gaps = [bound_q[:, s * STRIP:(s + 1) * STRIP] - f_end - carry[s][0] for s in range(N_STRIPS)]
        return jnp.max(functools.reduce(jnp.maximum, gaps)) > DEAD_LOG2

    def body(state):
        i, _, cr = state
        cr = pair(N_STRIPS + 2 * i, cr)
        return i + 1, alive(cr, base - 3 - 2 * i), cr

    one = jnp.ones((1, STRIP), F32)
    carry = tuple((m[s], l[s], acc[s], one, one) for s in range(N_STRIPS))
    n_pairs = base // 2
    i_end, _, carry = lax.while_loop(lambda st: jnp.logical_and(st[0] < n_pairs, st[1]), body,
                                     (jnp.int32(0), alive(carry, base - 1), carry))
    last = N_STRIPS + 2 * i_end - 1
    carry = pv(blk(last - 1), p_a, carry, 0)
    carry = pv(blk(last), p_b, carry, 1)
    for s in range(N_STRIPS):
        _, l, acc, _, _ = carry[s]
        o_ref[0, 0, :, s * STRIP:(s + 1) * STRIP] = (acc / l).astype(BF16)


def _attn_prompt(kernel, qT, k, vT, extra=()):
    b, h, kd, t = qT.shape
    nk, tk = vT.shape[2], vT.shape[4]
    tq = Q_TILE
    in_specs = [
        pl.BlockSpec((1, 1, kd, tq), lambda bi, hi, qi: (bi, hi, 0, qi)),
        pl.BlockSpec((1, 1, t, kd), lambda bi, hi, qi: (bi, hi, 0, 0)),
        pl.BlockSpec((1, 1, nk, HEAD_DIM, tk), lambda bi, hi, qi: (bi, hi, 0, 0, 0)),
    ] + [pl.BlockSpec(a.shape, lambda bi, hi, qi: (0, 0)) if a.ndim == 2 else
         pl.BlockSpec((1, 1) + a.shape[2:], lambda bi, hi, qi: (bi, hi, 0, 0)) for a in extra]
    scores = pltpu.VMEM((N_STRIPS, STRIP, STRIP), F32)
    probs = pltpu.VMEM((N_STRIPS, STRIP, STRIP), BF16)
    return pl.pallas_call(
        kernel,
        grid=(b, h, t // tq),
        in_specs=in_specs,
        out_specs=pl.BlockSpec((1, 1, HEAD_DIM, tq), lambda bi, hi, qi: (bi, hi, 0, qi)),
        out_shape=jax.ShapeDtypeStruct((b, h, HEAD_DIM, t), BF16),
        scratch_shapes=[scores, scores, probs, probs],
        compiler_params=_cparams(("parallel", "parallel", "arbitrary")),
        name=kernel.__name__.strip("_"),
    )(qT, k, vT, *extra)


def _sample_kv(ck_ref, kn_ref, cv_ref, vn_ref, h):
    kT = jnp.concatenate([ck_ref[0, h], kn_ref[0, h]], axis=1).astype(BF16)
    vT = jnp.concatenate([cv_ref[0, h], vn_ref[0, h]], axis=1).astype(BF16)
    return kT, vT


def _pv(p, vT):
    return lax.dot_general(p, vT, (((1,), (1,)), ((), ())), preferred_element_type=F32)


def _sb_sample_kernel(q_ref, ck_ref, kn_ref, cv_ref, vn_ref, tri_ref, o_ref):
    n_heads, tq = q_ref.shape[1], q_ref.shape[2]
    p_len = ck_ref.shape[3]
    nk = p_len + kn_ref.shape[3]
    row = lax.broadcasted_iota(jnp.int32, (tq, nk), 0)
    col = lax.broadcasted_iota(jnp.int32, (tq, nk), 1)
    mask = col < row + p_len
    tri = tri_ref[...]
    for h in range(n_heads):
        kT, vT = _sample_kv(ck_ref, kn_ref, cv_ref, vn_ref, h)
        q = (q_ref[0, h] * HEAD_DIM ** -0.5).astype(BF16)
        z = _dot(q, kT)
        ls_full = -(jnp.maximum(z, 0.0) + jnp.log1p(jnp.exp(-jnp.abs(z))))
        ls = jnp.where(mask, ls_full, 0.0)
        later = _dot(ls.astype(BF16), tri)
        wgt = jnp.where(mask, jnp.exp(z + ls_full + later), 0.0)
        o_ref[0, :, h * HEAD_DIM:(h + 1) * HEAD_DIM] = _pv(wgt.astype(BF16), vT)


def _fox_sample_kernel(q_ref, ck_ref, kn_ref, cv_ref, vn_ref, lf_ref, o_ref):
    n_heads, tq = q_ref.shape[1], q_ref.shape[2]
    p_len = ck_ref.shape[3]
    nk = p_len + kn_ref.shape[3]
    row = lax.broadcasted_iota(jnp.int32, (tq, nk), 0)
    col = lax.broadcasted_iota(jnp.int32, (tq, nk), 1)
    mask = col <= row + p_len
    diag = col == row + p_len
    f_all = _lane_cumsum(lf_ref[0])
    for h in range(n_heads):
        kT, vT = _sample_kv(ck_ref, kn_ref, cv_ref, vn_ref, h)
        q = (q_ref[0, h] * HEAD_DIM ** -0.5).astype(BF16)
        fk = f_all[h:h + 1, :]
        fq = jnp.sum(jnp.where(diag, fk, 0.0), axis=1, keepdims=True)
        z = _dot(q, kT) + fq - fk
        z = jnp.where(mask, z, NEG_BIG)
        p = jnp.exp(z - jnp.max(z, axis=1, keepdims=True))
        l = jnp.sum(p, axis=1, keepdims=True)
        o_ref[0, :, h * HEAD_DIM:(h + 1) * HEAD_DIM] = _pv(p.astype(BF16), vT) / l


def _attn_sample(kernel, q, ck, kn, cv, vn, extra, extra_spec):
    b, h, tq, hd = q.shape
    per_b = lambda a: pl.BlockSpec((1,) + a.shape[1:], lambda bi: (bi,) + (0,) * (a.ndim - 1))
    return pl.pallas_call(
        kernel,
        grid=(b,),
        in_specs=[per_b(a) for a in (q, ck, kn, cv, vn)] + [extra_spec],
        out_specs=pl.BlockSpec((1, tq, h * hd), lambda bi: (bi, 0, 0)),
        out_shape=jax.ShapeDtypeStruct((b, tq, h * hd), F32),
        compiler_params=_cparams(("parallel",)),
        name=kernel.__name__.strip("_"),
    )(q, ck, kn, cv, vn, extra)


def _route(lt, n_groups, epg):
    n = lt.shape[1]
    g = [lt[i:i + 1, :] for i in range(n_groups)]
    gmax = functools.reduce(jnp.maximum, g)
    g_w = 1.0 / functools.reduce(jnp.add, [jnp.exp(gi - gmax) for gi in g])
    is_g, taken = [], None
    for gi in g:
        hit = gi >= gmax
        if taken is not None:
            hit = jnp.logical_and(hit, jnp.logical_not(taken))
        taken = hit if taken is None else jnp.logical_or(taken, hit)
        is_g.append(hit)
    le = [lt[n_groups + i:n_groups + i + 1, :] for i in range(n_groups * epg)]
    e_sel = []
    for i in range(epg):
        v = le[(n_groups - 1) * epg + i]
        for gi in reversed(range(n_groups - 1)):
            v = jnp.where(is_g[gi], le[gi * epg + i], v)
        e_sel.append(v)
    emax = functools.reduce(jnp.maximum, e_sel)
    pe = [jnp.exp(v - emax) for v in e_sel]
    pden = functools.reduce(jnp.add, pe)
    prob = [p / pden for p in pe]

    def first_argmax(vals):
        vmax = functools.reduce(jnp.maximum, vals)
        hits, tk = [], None
        for v in vals:
            hit = v >= vmax
            if tk is not None:
                hit = jnp.logical_and(hit, jnp.logical_not(tk))
            tk = hit if tk is None else jnp.logical_or(tk, hit)
            hits.append(hit)
        return vmax, hits

    p1, t1 = first_argmax(prob)
    p2, t2 = first_argmax([jnp.where(t, -1.0, p) for t, p in zip(t1, prob)])
    tot = p1 + p2
    w1 = g_w * (p1 / tot)
    w2 = g_w * (p2 / tot)
    rows = lax.broadcasted_iota(jnp.int32, (LANES, n), 0)
    comb = jnp.zeros((LANES, n), F32)
    for gi in range(n_groups):
        for i in range(epg):
            val = jnp.where(is_g[gi], jnp.where(t1[i], w1, 0.0) + jnp.where(t2[i], w2, 0.0), 0.0)
            comb = jnp.where(rows == gi * epg + i, jnp.broadcast_to(val, (LANES, n)), comb)
    return comb


def _post_kernel(yaT_ref, ybT_ref, gates_ref, x_ref, mods_ref, wba_ref, wbb_ref, wout_ref,
                 gmoe_ref, wr2_ref, wrhi_ref, br_ref, x1_ref, h2_ref, comb_ref,
                 *, per_token, n_groups, epg):
    d = x_ref.shape[2]
    tdot = lambda aT, w: lax.dot_general(aT, w, (((0,), (0,)), ((), ())), preferred_element_type=F32)
    ua = tdot(yaT_ref[0], wba_ref[...])
    ub = tdot(ybT_ref[0], wbb_ref[...])
    gates = gates_ref[0].astype(F32)
    mix = gates[:, :d] * ua + gates[:, d:] * ub
    x1 = x_ref[0] + _mod(mods_ref, 2, per_token) * _dot(mix.astype(BF16), wout_ref[...])
    x1_ref[0] = x1
    h2 = _rms_mod(x1, gmoe_ref[...], _mod(mods_ref, 4, per_token), _mod(mods_ref, 3, per_token))
    h2_hi = h2.astype(BF16)
    h2_ref[0] = h2_hi
    h2_lo = (h2 - h2_hi.astype(F32)).astype(BF16)
    a = _dot(h2_hi, wr2_ref[...])
    logits = a[:, :LANES] + a[:, LANES:] + _dot(h2_lo, wrhi_ref[...]) + br_ref[...]
    comb_ref[0] = _route(logits.T, n_groups, epg).T


def _post(yaT, ybT, gates, x, mods, w_ba, w_bb, w_out, g_moe, wr2, wrhi, br, per_token, n_groups, epg):
    b, t, d = x.shape
    tm = min(POST_TILE, t)
    w = yaT.shape[1]
    const2 = lambda bi, ti: (0, 0)
    tok = lambda last: pl.BlockSpec((1, tm, last), lambda bi, ti: (bi, ti, 0))
    chan = pl.BlockSpec((1, w, tm), lambda bi, ti: (bi, 0, ti))
    if per_token:
        mods_spec = pl.BlockSpec(mods.shape, lambda bi, ti: (0, 0, 0))
    else:
        mods_spec = pl.BlockSpec((1, N_MOD, d), lambda bi, ti: (bi, 0, 0))
    in_specs = [chan, chan, tok(2 * d), tok(d), mods_spec,
                pl.BlockSpec(w_ba.shape, const2), pl.BlockSpec(w_bb.shape, const2),
                pl.BlockSpec(w_out.shape, const2), pl.BlockSpec((1, d), const2),
                pl.BlockSpec(wr2.shape, const2), pl.BlockSpec(wrhi.shape, const2),
                pl.BlockSpec((1, LANES), const2)]
    out_shape = (jax.ShapeDtypeStruct((b, t, d), F32),
                 jax.ShapeDtypeStruct((b, t, d), BF16),
                 jax.ShapeDtypeStruct((b, t, LANES), F32))
    return pl.pallas_call(
        functools.partial(_post_kernel, per_token=per_token, n_groups=n_groups, epg=epg),
        grid=(b, t // tm),
        in_specs=in_specs,
        out_specs=(tok(d), tok(d), tok(LANES)),
        out_shape=out_shape,
        compiler_params=_cparams(("parallel", "parallel")),
        name="post_sample" if per_token else "post_prompt",
    )(yaT, ybT, gates, x, mods, w_ba, w_bb, w_out, g_moe, wr2, wrhi, br)


def _moe_kernel(h2_ref, comb_ref, x1_ref, mods_ref, w13_ref, w2_ref, gfin_ref, y_ref, acc_ref,
                *, per_token, epg, hid):
    g = pl.program_id(1)

    @pl.when(g == 0)
    def _():
        acc_ref[...] = jnp.zeros_like(acc_ref)

    h2 = h2_ref[...]
    comb = comb_ref[...]
    lane = lax.broadcasted_iota(jnp.int32, comb.shape, 1)
    acts = []
    for c in range(0, epg, 2):
        a = _dot(h2, w13_ref[0, :, c * 2 * hid:(c + 2) * 2 * hid])
        for i in range(c, c + 2):
            a1 = a[:, (i - c) * 2 * hid:(i - c) * 2 * hid + hid]
            a3 = a[:, (i - c) * 2 * hid + hid:(i - c + 1) * 2 * hid]
            cw = jnp.sum(jnp.where(lane == g * epg + i, comb, 0.0), axis=1, keepdims=True)
            acts.append(((a1 * jax.nn.sigmoid(a1)) * a3 * cw).astype(BF16))
    acc_ref[...] += _dot(jnp.concatenate(acts, axis=1), w2_ref[0])

    @pl.when(g == pl.num_programs(1) - 1)
    def _():
        x2 = x1_ref[...] + _mod(mods_ref, 5, per_token) * acc_ref[...]
        ms = jnp.mean(x2 * x2, axis=-1, keepdims=True)
        y_ref[...] = (x2 * lax.rsqrt(ms + RMS_EPS)) * gfin_ref[...]


def _moe(h2, comb, x1, mods, w13, w2, g_final, per_token, tokens_per_batch, epg):
    n, d = x1.shape
    tm = min(MOE_TILE, n)
    n_groups = w13.shape[0]
    hid = w2.shape[1] // epg
    assert epg % 2 == 0
    tiles_per_batch = tokens_per_batch // tm if not per_token else 1
    tok = lambda last: pl.BlockSpec((tm, last), lambda i, g: (i, 0))
    if per_token:
        mods_spec = pl.BlockSpec(mods.shape, lambda i, g: (0, 0, 0))
    else:
        mods_spec = pl.BlockSpec((1, N_MOD, d), lambda i, g: (i // tiles_per_batch, 0, 0))
    return pl.pallas_call(
        functools.partial(_moe_kernel, per_token=per_token, epg=epg, hid=hid),
        grid=(n // tm, n_groups),
        in_specs=[tok(d), tok(LANES), tok(d), mods_spec,
                  pl.BlockSpec((1,) + w13.shape[1:], lambda i, g: (g, 0, 0)),
                  pl.BlockSpec((1,) + w2.shape[1:], lambda i, g: (g, 0, 0)),
                  pl.BlockSpec((1, d), lambda i, g: (0, 0))],
        out_specs=tok(d),
        out_shape=jax.ShapeDtypeStruct((n, d), F32),
        scratch_shapes=[pltpu.VMEM((tm, d), F32)],
        compiler_params=_cparams(("parallel", "arbitrary")),
        name="moe_sample" if per_token else "moe_prompt",
    )(h2, comb, x1, mods, w13, w2, g_final)


def kernel(x_prompt, x_sample, cache_sb_k, cache_sb_v, cache_fox_k, cache_fox_v, cache_fox_logf,
           c_prompt, c_sample, w_ada, b_ada, g_mix, w_in, b_f, w_ba, w_bb, w_gate, b_gate, w_out,
           g_moe, w_rg, b_rg, w_re, b_re, w1, w3, w2, g_final):
    depth = w_ada.shape[0]
    assert depth == 1, "single-layer trunk"
    bp, t, d = x_prompt.shape
    bs, ts, _ = x_sample.shape
    n_heads = cache_sb_k.shape[2]
    p_len = cache_sb_k.shape[3]
    assert cache_fox_k.shape[2] == n_heads and n_heads * HEAD_DIM * 6 + n_heads == w_in.shape[2]
    assert n_heads == 8 and t % Q_TILE == 0 and Q_TILE % PRE_TILE == 0
    n_groups = w_rg.shape[2]
    n_exp = w_re.shape[2]
    epg = n_exp // n_groups
    assert n_groups + n_exp <= LANES
    w = n_heads * HEAD_DIM
    ns = bs * ts

    w_in_p = jnp.pad(w_in[0], ((0, 0), (0, LANES - n_heads))).astype(BF16)
    b_f_p = jnp.pad(b_f[0], (0, LANES - n_heads)).reshape(1, LANES)
    w_gate_b = w_gate[0].astype(BF16)
    b_gate_r = b_gate[0].reshape(1, -1)
    g_mix_r = g_mix[0].reshape(1, d)
    g_moe_r = g_moe[0].reshape(1, d)
    g_fin_r = g_final.reshape(1, d)
    w_ba_b, w_bb_b, w_out_b = w_ba[0].astype(BF16), w_bb[0].astype(BF16), w_out[0].astype(BF16)
    w_r = jnp.pad(jnp.concatenate([w_rg[0], w_re[0]], axis=1), ((0, 0), (0, LANES - n_groups - n_exp)))
    w_r_hi = w_r.astype(BF16)
    w_r_lo = (w_r - w_r_hi.astype(F32)).astype(BF16)
    wr2 = jnp.concatenate([w_r_hi, w_r_lo], axis=1)
    b_r = jnp.pad(jnp.concatenate([b_rg[0], b_re[0]]), (0, LANES - n_groups - n_exp)).reshape(1, LANES)
    hid = w1.shape[3]
    w13 = jnp.concatenate([w1[0], w3[0]], axis=2).astype(BF16).reshape(n_groups, epg, d, 2 * hid)
    w13 = w13.transpose(0, 2, 1, 3).reshape(n_groups, d, epg * 2 * hid)
    w2_b = w2[0].astype(BF16).reshape(n_groups, epg * hid, d)

    ada = _ada(jnp.concatenate([c_prompt, c_sample], axis=0), w_ada[0], b_ada[0])
    mods_p = ada[:bp].reshape(bp, N_MOD, d)
    mods_s = jnp.repeat(ada[bp:].reshape(bs, N_MOD, d).transpose(1, 0, 2), ts, axis=1)

    hsel = jnp.repeat(jnp.eye(n_heads, dtype=BF16), HEAD_DIM, axis=1)
    (qsT, ks, vsT, qfT, kf, vfT, ksl, vsl, kfl, vfl, logf_p, kn2, gates_p) = _pre_prompt(
        x_prompt, mods_p, g_mix_r, w_in_p, b_f_p, w_gate_b, b_gate_r, hsel, n_heads)
    tk = PRE_TILE
    ids = jnp.arange(tk)
    ntri_p = -(ids[None, :] > ids[:, None]).astype(BF16)
    yaT = _attn_prompt(_sb_prompt_kernel, qsT, ks, vsT, (ntri_p,)).reshape(bp, w, t)
    ybT = _attn_prompt(_fox_prompt_kernel, qfT, kf, vfT, (kn2[:, :, None, :],)).reshape(bp, w, t)
    x1_p, h2_p, comb_p = _post(yaT, ybT, gates_p, x_prompt, mods_p, w_ba_b, w_bb_b, w_out_b,
                               g_moe_r, wr2, w_r_hi, b_r, False, n_groups, epg)
    y_prompt = _moe(h2_p.reshape(bp * t, d), comb_p.reshape(bp * t, LANES), x1_p.reshape(bp * t, d),
                    mods_p, w13, w2_b, g_fin_r, False, t, epg).reshape(bp, t, d)

    proj_s, logf_s, gates_s = _pre_sample(x_sample.reshape(ns, d), mods_s, g_mix_r, w_in_p, b_f_p,
                                          w_gate_b, b_gate_r, n_heads)
    heads = lambda i: proj_s[:, i * w:(i + 1) * w].reshape(bs, ts, n_heads, HEAD_DIM).transpose(0, 2, 1, 3)
    qa_s, ka_s, va_s, qb_s, kb_s, vb_s = [heads(i) for i in range(6)]
    lf_s = logf_s[:, :n_heads].reshape(bs, ts, n_heads).transpose(0, 2, 1)
    pad_k = lambda a: jnp.pad(jnp.swapaxes(a, 2, 3), ((0, 0), (0, 0), (0, 0), (0, LANES - ts)))
    chan = lambda cache: jnp.swapaxes(cache[0], 2, 3)
    nk = p_len + LANES
    ids = jnp.arange(nk)
    tri_s = (ids[:, None] > ids[None, :]).astype(BF16)
    lf_all = jnp.concatenate([cache_fox_logf[0], jnp.pad(lf_s, ((0, 0), (0, 0), (0, LANES - ts)))], axis=2)
    ya_s = _attn_sample(_sb_sample_kernel, qa_s, chan(cache_sb_k), pad_k(ka_s), chan(cache_sb_v), pad_k(va_s),
                        tri_s, pl.BlockSpec(tri_s.shape, lambda bi: (0, 0)))
    yb_s = _attn_sample(_fox_sample_kernel, qb_s, chan(cache_fox_k), pad_k(kb_s), chan(cache_fox_v), pad_k(vb_s),
                        lf_all, pl.BlockSpec((1, n_heads, nk), lambda bi: (bi, 0, 0)))
    to_chan = lambda y: y.reshape(ns, w).T.astype(BF16)[None]
    x1_s, h2_s, comb_s = _post(to_chan(ya_s), to_chan(yb_s), gates_s[None], x_sample.reshape(1, ns, d),
                               mods_s, w_ba_b, w_bb_b, w_out_b, g_moe_r, wr2, w_r_hi, b_r,
                               True, n_groups, epg)
    y_sample = _moe(h2_s[0], comb_s[0], x1_s[0], mods_s, w13, w2_b, g_fin_r, True, ns, epg).reshape(bs, ts, d)

    lead = lambda a: a[None]
    tok_major = lambda a: jnp.swapaxes(a, 2, 3)[None]
    return (y_prompt, y_sample,
            tok_major(ksl), tok_major(vsl), tok_major(kfl), tok_major(vfl), lead(logf_p),
            lead(ka_s), lead(va_s), lead(kb_s), lead(vb_s), lead(lf_s))
```

```python
import functools

import jax
import jax.numpy as jnp
from jax import lax
from jax.experimental import pallas as pl
from jax.experimental.pallas import tpu as pltpu

F32 = jnp.float32
BF16 = jnp.bfloat16

HEAD_DIM = 64
RMS_EPS = 1e-6
N_MOD = 6
LANES = 128
NEG_BIG = -1e30

PRE_TILE = 256
Q_TILE = 1024
STRIP = 256
N_STRIPS = Q_TILE // STRIP
LOG2E = 1.4426950408889634
DEAD_LOG2 = -128.0
NORM_SLACK = 1.02
POST_TILE = 512
MOE_TILE = 1024
VMEM_LIMIT = 56 * 1024 * 1024


def _cparams(sem):
    return pltpu.CompilerParams(dimension_semantics=sem, vmem_limit_bytes=VMEM_LIMIT)


def _log_sigmoid(x):
    return jnp.minimum(x, 0.0) - jnp.log1p(jnp.exp(-jnp.abs(x)))


def _rms_mod(x, g, scale, shift):
    ms = jnp.mean(x * x, axis=-1, keepdims=True)
    y = x * lax.rsqrt(ms + RMS_EPS)
    return (y * g) * (1.0 + scale) + shift


def _mod(mods_ref, i, per_token):
    return mods_ref[i] if per_token else mods_ref[0, i:i + 1, :]


def _dot(a, b):
    return jnp.dot(a, b, preferred_element_type=F32)


def _split3(f):
    hi = f.astype(BF16).astype(F32)
    r = f - hi
    mid = r.astype(BF16).astype(F32)
    lo = (r - mid).astype(BF16).astype(F32)
    return hi, mid, lo


def _lane_cumsum(x):
    n = x.shape[1]
    lane = lax.broadcasted_iota(jnp.int32, x.shape, 1)
    d = 1
    while d < n:
        x = x + jnp.where(lane >= d, pltpu.roll(x, d, axis=1), 0.0)
        d *= 2
    return x


def _ada_kernel(c_ref, w_ref, b_ref, o_ref):
    c = c_ref[...]
    s = c * jax.nn.sigmoid(c)
    o_ref[...] = jnp.dot(s, w_ref[...], preferred_element_type=F32,
                         precision=lax.Precision.HIGHEST) + b_ref[...]


def _ada(c_all, w_ada, b_ada):
    n, d = c_all.shape
    nout = w_ada.shape[1]
    tn = 1024
    return pl.pallas_call(
        _ada_kernel,
        grid=(nout // tn,),
        in_specs=[pl.BlockSpec((n, d), lambda j: (0, 0)),
                  pl.BlockSpec((d, tn), lambda j: (0, j)),
                  pl.BlockSpec((1, tn), lambda j: (0, j))],
        out_specs=pl.BlockSpec((n, tn), lambda j: (0, j)),
        out_shape=jax.ShapeDtypeStruct((n, nout), F32),
        compiler_params=_cparams(("arbitrary",)),
        name="ada",
    )(c_all, w_ada, b_ada.reshape(1, nout))


def _pre_core(x, mods_ref, gmix_ref, win_ref, wg_ref, bg_ref, per_token):
    h = _rms_mod(x, gmix_ref[...], _mod(mods_ref, 1, per_token), _mod(mods_ref, 0, per_token))
    hb = h.astype(BF16)
    proj = _dot(hb, win_ref[...])
    gates = jax.nn.sigmoid(_dot(hb, wg_ref[...]) + bg_ref[...])
    return proj, gates.astype(BF16)


def _pre_prompt_kernel(x_ref, mods_ref, gmix_ref, win_ref, bf_ref, wg_ref, bg_ref, hsel_ref,
                       qsT_ref, ks_ref, vsT_ref, qfT_ref, kf_ref, vfT_ref,
                       ksl_ref, vsl_ref, kfl_ref, vfl_ref, logf_ref, kn2_ref, gates_ref,
                       carry_ref, *, n_heads):
    tm = x_ref.shape[1]
    w = n_heads * HEAD_DIM

    @pl.when(pl.program_id(1) == 0)
    def _():
        carry_ref[...] = jnp.zeros_like(carry_ref)

    proj, gates = _pre_core(x_ref[0], mods_ref, gmix_ref, win_ref, wg_ref, bg_ref, False)
    gates_ref[0] = gates
    qa, ka, va, qb, kb, vb = [proj[:, i * w:(i + 1) * w] for i in range(6)]
    fg = proj[:, 6 * w:6 * w + LANES]
    scale = HEAD_DIM ** -0.5 * LOG2E
    qaT = (qa * scale).T
    kaT = ka.T
    vaT = va.T
    qbT = (qb * scale).T
    kbT = kb.T
    vbT = vb.T
    kn2_ref[0] = _dot(hsel_ref[...], (kbT * kbT).astype(BF16))

    logfT = _log_sigmoid(fg + bf_ref[...]).T[0:n_heads, :]
    logf_ref[0] = logfT
    f = _lane_cumsum(logfT) + carry_ref[:, 0:1]
    carry_ref[...] = jnp.broadcast_to(f[:, tm - 1:tm], carry_ref.shape)
    f_hi, f_mid, f_lo = _split3(f * LOG2E)

    row64 = lax.broadcasted_iota(jnp.int32, (HEAD_DIM, tm), 0)
    row8 = lax.broadcasted_iota(jnp.int32, (8, tm), 0)
    zeros64 = jnp.zeros((HEAD_DIM, tm), BF16)
    ke_parts = []
    for h in range(n_heads):
        def bc(a, n):
            return jnp.broadcast_to(a[h:h + 1, :], (n, tm))
        qe = jnp.where(row64 < 3, 1.0,
                       jnp.where(row64 == 3, bc(f_hi, HEAD_DIM),
                                 jnp.where(row64 == 4, bc(f_mid, HEAD_DIM),
                                           jnp.where(row64 == 5, bc(f_lo, HEAD_DIM), 0.0))))
        sl = slice(h * HEAD_DIM, (h + 1) * HEAD_DIM)
        qfT_ref[0, h, 0:HEAD_DIM, :] = qbT[sl, :].astype(BF16)
        qfT_ref[0, h, HEAD_DIM:2 * HEAD_DIM, :] = qe.astype(BF16)
        qsT_ref[0, h, 0:HEAD_DIM, :] = qaT[sl, :].astype(BF16)
        qsT_ref[0, h, HEAD_DIM:2 * HEAD_DIM, :] = zeros64
        vsT_ref[0, h, 0] = vaT[sl, :].astype(BF16)
        vfT_ref[0, h, 0] = vbT[sl, :].astype(BF16)
        ksl_ref[0, h] = kaT[sl, :]
        vsl_ref[0, h] = vaT[sl, :]
        kfl_ref[0, h] = kbT[sl, :]
        vfl_ref[0, h] = vbT[sl, :]
        ke_parts.append(
            jnp.where(row8 == 0, -bc(f_hi, 8),
                      jnp.where(row8 == 1, -bc(f_mid, 8),
                                jnp.where(row8 == 2, -bc(f_lo, 8),
                                          jnp.where(row8 < 6, 1.0, 0.0)))))
    ke_parts.append(jnp.zeros((LANES - 8 * n_heads, tm), F32))
    ke = jnp.concatenate(ke_parts, axis=0).T

    lane = lax.broadcasted_iota(jnp.int32, (tm, LANES), 1)
    for h in range(n_heads):
        base = (h // 2) * LANES
        ka_slab = ka[:, base:base + LANES]
        kb_slab = kb[:, base:base + LANES]
        if h % 2:
            ka_slab = pltpu.roll(ka_slab, HEAD_DIM, axis=1)
            kb_slab = pltpu.roll(kb_slab, HEAD_DIM, axis=1)
        ext = pltpu.roll(ke, HEAD_DIM - 8 * h, axis=1)
        ks_ref[0, h] = jnp.where(lane < HEAD_DIM, ka_slab, 0.0).astype(BF16)
        kf_ref[0, h] = jnp.where(lane < HEAD_DIM, kb_slab,
                                 jnp.where(lane < HEAD_DIM + 8, ext, 0.0)).astype(BF16)


def _pre_prompt(x, mods, g_mix, w_in_p, b_f_p, w_gate, b_gate, hsel, n_heads):
    b, t, d = x.shape
    tm = PRE_TILE
    nt = t // tm
    h, hd = n_heads, HEAD_DIM
    const2 = lambda bi, ti: (0, 0)
    head_t = lambda bi, ti: (bi, 0, 0, ti)
    head_s = lambda bi, ti: (bi, 0, ti, 0)
    blk_t = lambda bi, ti: (bi, 0, ti, 0, 0)
    out_shape = (
        jax.ShapeDtypeStruct((b, h, 2 * hd, t), BF16),
        jax.ShapeDtypeStruct((b, h, t, 2 * hd), BF16),
        jax.ShapeDtypeStruct((b, h, nt, hd, tm), BF16),
        jax.ShapeDtypeStruct((b, h, 2 * hd, t), BF16),
        jax.ShapeDtypeStruct((b, h, t, 2 * hd), BF16),
        jax.ShapeDtypeStruct((b, h, nt, hd, tm), BF16),
        jax.ShapeDtypeStruct((b, h, hd, t), F32),
        jax.ShapeDtypeStruct((b, h, hd, t), F32),
        jax.ShapeDtypeStruct((b, h, hd, t), F32),
        jax.ShapeDtypeStruct((b, h, hd, t), F32),
        jax.ShapeDtypeStruct((b, h, t), F32),
        jax.ShapeDtypeStruct((b, h, t), F32),
        jax.ShapeDtypeStruct((b, t, w_gate.shape[1]), BF16),
    )
    out_specs = (
        pl.BlockSpec((1, h, 2 * hd, tm), head_t),
        pl.BlockSpec((1, h, tm, 2 * hd), head_s),
        pl.BlockSpec((1, h, 1, hd, tm), blk_t),
        pl.BlockSpec((1, h, 2 * hd, tm), head_t),
        pl.BlockSpec((1, h, tm, 2 * hd), head_s),
        pl.BlockSpec((1, h, 1, hd, tm), blk_t),
        pl.BlockSpec((1, h, hd, tm), head_t),
        pl.BlockSpec((1, h, hd, tm), head_t),
        pl.BlockSpec((1, h, hd, tm), head_t),
        pl.BlockSpec((1, h, hd, tm), head_t),
        pl.BlockSpec((1, h, tm), lambda bi, ti: (bi, 0, ti)),
        pl.BlockSpec((1, h, tm), lambda bi, ti: (bi, 0, ti)),
        pl.BlockSpec((1, tm, w_gate.shape[1]), lambda bi, ti: (bi, ti, 0)),
    )
    in_specs = [
        pl.BlockSpec((1, tm, d), lambda bi, ti: (bi, ti, 0)),
        pl.BlockSpec((1, N_MOD, d), lambda bi, ti: (bi, 0, 0)),
        pl.BlockSpec((1, d), const2),
        pl.BlockSpec(w_in_p.shape, const2),
        pl.BlockSpec((1, LANES), const2),
        pl.BlockSpec(w_gate.shape, const2),
        pl.BlockSpec((1, w_gate.shape[1]), const2),
        pl.BlockSpec(hsel.shape, const2),
    ]
    return pl.pallas_call(
        functools.partial(_pre_prompt_kernel, n_heads=n_heads),
        grid=(b, nt),
        in_specs=in_specs,
        out_specs=out_specs,
        out_shape=out_shape,
        scratch_shapes=[pltpu.VMEM((h, LANES), F32)],
        compiler_params=_cparams(("arbitrary", "arbitrary")),
        name="pre_prompt",
    )(x, mods, g_mix, w_in_p, b_f_p, w_gate, b_gate, hsel)


def _pre_sample_kernel(x_ref, mods_ref, gmix_ref, win_ref, bf_ref, wg_ref, bg_ref,
                       proj_ref, logf_ref, gates_ref, *, n_heads):
    w = n_heads * HEAD_DIM
    proj, gates = _pre_core(x_ref[...], mods_ref, gmix_ref, win_ref, wg_ref, bg_ref, True)
    proj_ref[...] = proj
    gates_ref[...] = gates
    logf_ref[...] = _log_sigmoid(proj[:, 6 * w:6 * w + LANES] + bf_ref[...])


def _pre_sample(x, mods_tok, g_mix, w_in_p, b_f_p, w_gate, b_gate, n_heads):
    n, d = x.shape
    full = lambda a: pl.BlockSpec(a.shape, lambda i: (0,) * a.ndim)
    args = (x, mods_tok, g_mix, w_in_p, b_f_p, w_gate, b_gate)
    out_shape = (jax.ShapeDtypeStruct((n, w_in_p.shape[1]), F32),
                 jax.ShapeDtypeStruct((n, LANES), F32),
                 jax.ShapeDtypeStruct((n, w_gate.shape[1]), BF16))
    return pl.pallas_call(
        functools.partial(_pre_sample_kernel, n_heads=n_heads),
        grid=(1,),
        in_specs=[full(a) for a in args],
        out_specs=tuple(pl.BlockSpec(s.shape, lambda i: (0, 0)) for s in out_shape),
        out_shape=out_shape,
        compiler_params=_cparams(("arbitrary",)),
        name="pre_sample",
    )(*args)


STATIC_DEPTHS = 2


def _attn_parts(qT_ref, k_ref, vT_ref):
    assert qT_ref.shape[3] == N_STRIPS * STRIP and vT_ref.shape[4] == STRIP
    nk = vT_ref.shape[2]
    base = pl.program_id(2) * N_STRIPS
    q_strip = lambda s: qT_ref[0, 0, :, s * STRIP:(s + 1) * STRIP]
    k_block = lambda j: k_ref[0, 0, pl.ds(pl.multiple_of(j * STRIP, STRIP), STRIP), :]
    v_block = lambda j: vT_ref[0, 0, j]
    block = lambda s, u: base + s - u
    clip = lambda j: jnp.clip(j, 0, nk - 1)

    def qk(u, z_scr):
        for s in range(N_STRIPS):
            z_scr[s] = _dot(k_block(clip(block(s, u))), q_strip(s))

    row = lax.broadcasted_iota(jnp.int32, (STRIP, STRIP), 0)
    col = lax.broadcasted_iota(jnp.int32, (STRIP, STRIP), 1)
    n_pairs = (base + N_STRIPS - STATIC_DEPTHS + 1) // 2
    return base, q_strip, k_block, v_block, block, clip, qk, row, col, n_pairs


def _sb_prompt_kernel(qT_ref, k_ref, vT_ref, ntri_ref, o_ref, z_a, z_b, w_a, w_b):
    base, q_strip, k_block, v_block, block, clip, qk, row, col, n_pairs = _attn_parts(qT_ref, k_ref, vT_ref)
    diag = row < col
    sign = jnp.uint32(0x80000000)

    def softplus2(z):
        neg_abs = lax.bitcast_convert_type(lax.bitcast_convert_type(z, jnp.uint32) | sign, F32)
        return jnp.maximum(z, 0.0) + jnp.log2(1.0 + jnp.exp2(neg_abs))

    def suffix(sp):
        sp_b = sp.astype(BF16)
        later = _dot(ntri_ref[...], sp_b)
        return later, later[0:1, :] - sp_b[0:1, :].astype(F32)

    def masked(x, s, u):
        if isinstance(u, int) and u == 0:
            return jnp.where(diag, x, 0.0)
        if isinstance(u, int) and u <= s:
            return x
        return jnp.where(block(s, u) >= 0, x, 0.0)

    w_a[...] = jnp.zeros(w_a.shape, BF16)
    w_b[...] = jnp.zeros(w_b.shape, BF16)
    subs = [(s, u) for u in range(STATIC_DEPTHS) for s in range(N_STRIPS)]
    zs = [_dot(k_block(clip(block(s, u))), q_strip(s)) for s, u in subs]
    qk(STATIC_DEPTHS, z_a)
    c = [jnp.zeros((1, STRIP), F32)] * N_STRIPS
    acc = [jnp.zeros((HEAD_DIM, STRIP), F32)] * N_STRIPS
    mids = []
    for (s, u), z in zip(subs, zs):
        sp_full = softplus2(z)
        later, total = suffix(masked(sp_full, s, u))
        mids.append((z - sp_full, later, c[s]))
        c[s] = c[s] + total
    for (s, u), (lsig, later, c_before) in zip(subs, mids):
        wgt = masked(jnp.exp2(lsig + later), s, u)
        acc[s] = acc[s] + _dot(v_block(clip(block(s, u))), wgt.astype(BF16)) * jnp.exp2(c_before)

    def pv(u, w_scr, carry, which):
        return tuple((cr[0], cr[1] + _dot(v_block(clip(block(s, u))), w_scr[s]) * cr[2 + which]) + cr[2:]
                     for s, cr in enumerate(carry))

    def softplus_phase(z_scr, carry, which, u):
        carry = list(carry)
        mids = []
        for s in range(N_STRIPS):
            c, acc, sa, sb = carry[s]
            z = z_scr[s]
            sp_full = softplus2(z)
            later, total = suffix(masked(sp_full, s, u))
            mids.append((z - sp_full, later))
            scale = jnp.exp2(c)
            carry[s] = (c + total, acc, sa, scale) if which else (c + total, acc, scale, sb)
        return tuple(carry), mids

    def weight_phase(w_scr, mids, u):
        for s, (lsig, later) in enumerate(mids):
            w_scr[s] = masked(jnp.exp2(lsig + later), s, u).astype(BF16)

    def pair(u, carry):
        carry = pv(u - 2, w_a, carry, 0)
        carry = pv(u - 1, w_b, carry, 1)
        qk(u + 1, z_b)
        carry, mids_a = softplus_phase(z_a, carry, 0, u)
        qk(u + 2, z_a)
        carry, mids_b = softplus_phase(z_b, carry, 1, u + 1)
        weight_phase(w_a, mids_a, u)
        weight_phase(w_b, mids_b, u + 1)
        return carry

    def alive(carry, u_next):
        left = [jnp.where(block(s, u_next) >= 0, cr[0], NEG_BIG) for s, cr in enumerate(carry)]
        return jnp.max(functools.reduce(jnp.maximum, left)) > DEAD_LOG2

    def body(state):
        i, _, cr = state
        u = STATIC_DEPTHS + 2 * i
        cr = pair(u, cr)
        return i + 1, alive(cr, u + 2), cr

    one = jnp.ones((1, STRIP), F32)
    carry = tuple((c[s], acc[s], one, one) for s in range(N_STRIPS))
    i_end, _, carry = lax.while_loop(lambda st: jnp.logical_and(st[0] < n_pairs, st[1]), body,
                                     (jnp.int32(0), alive(carry, STATIC_DEPTHS), carry))
    last = STATIC_DEPTHS + 2 * i_end - 1
    carry = pv(last - 1, w_a, carry, 0)
    carry = pv(last, w_b, carry, 1)
    for s in range(N_STRIPS):
        o_ref[0, 0, :, s * STRIP:(s + 1) * STRIP] = carry[s][1].astype(BF16)


def _fox_prompt_kernel(qT_ref, k_ref, vT_ref, kn2_ref, o_ref, z_a, z_b, p_a, p_b):
    base, q_strip, k_block, v_block, block, clip, qk, row, col, n_pairs = _attn_parts(qT_ref, k_ref, vT_ref)
    diag = row <= col

    def masked(sc, s, u):
        if isinstance(u, int) and u == 0:
            return jnp.where(diag, sc, NEG_BIG)
        if isinstance(u, int) and u <= s:
            return sc
        return jnp.where(block(s, u) >= 0, sc, NEG_BIG)

    p_a[...] = jnp.zeros(p_a.shape, BF16)
    p_b[...] = jnp.zeros(p_b.shape, BF16)
    subs = [(s, u) for u in range(STATIC_DEPTHS) for s in range(N_STRIPS)]
    zs = [_dot(k_block(clip(block(s, u))), q_strip(s)) for s, u in subs]
    qk(STATIC_DEPTHS, z_a)
    m = [jnp.full((1, STRIP), NEG_BIG, F32)] * N_STRIPS
    l = [jnp.zeros((1, STRIP), F32)] * N_STRIPS
    acc = [jnp.zeros((HEAD_DIM, STRIP), F32)] * N_STRIPS
    for (s, u), sc in zip(subs, zs):
        sc = masked(sc, s, u)
        m_new = jnp.maximum(m[s], jnp.max(sc, axis=0, keepdims=True))
        alpha = jnp.exp2(m[s] - m_new)
        p = jnp.exp2(sc - m_new)
        l[s] = alpha * l[s] + jnp.sum(p, axis=0, keepdims=True)
        acc[s] = alpha * acc[s] + _dot(v_block(clip(block(s, u))), p.astype(BF16))
        m[s] = m_new

    def pv(u, p_scr, carry, which):
        return tuple((cr[0], cr[1], cr[3 + which] * cr[2] + _dot(v_block(clip(block(s, u))), p_scr[s])) + cr[3:]
                     for s, cr in enumerate(carry))

    def valu(z_scr, p_scr, carry, which, u):
        out = []
        for s in range(N_STRIPS):
            m, l, acc, aa, ab = carry[s]
            m_new = jnp.maximum(m, jnp.max(masked(z_scr[s], s, u), axis=0, keepdims=True))
            alpha = jnp.exp2(m - m_new)
            p = jnp.exp2(masked(z_scr[s], s, u) - m_new)
            p_scr[s] = p.astype(BF16)
            l = alpha * l + jnp.sum(p, axis=0, keepdims=True)
            out.append((m_new, l, acc, aa, alpha) if which else (m_new, l, acc, alpha, ab))
        return tuple(out)

    def pair(u, carry):
        carry = pv(u - 2, p_a, carry, 0)
        carry = pv(u - 1, p_b, carry, 1)
        qk(u + 1, z_b)
        carry = valu(z_a, p_a, carry, 0, u)
        qk(u + 2, z_a)
        return valu(z_b, p_b, carry, 1, u + 1)

    q_all = qT_ref[0, 0]
    q_f = q_all[0:HEAD_DIM, :].astype(F32)
    q_norm = jnp.sqrt(jnp.sum(q_f * q_f, axis=0, keepdims=True))
    f_rows = q_all[HEAD_DIM:HEAD_DIM + 16, :].astype(F32)
    f_q = f_rows[3:4, :] + f_rows[4:5, :] + f_rows[5:6, :]
    k_norm = jnp.sqrt(jnp.max(kn2_ref[0, 0], axis=1, keepdims=True))
    bound_q = NORM_SLACK * q_norm * k_norm + f_q

    def alive(carry, u_next):
        left = []
        for s in range(N_STRIPS):
            j = block(s, u_next)
            rows = k_ref[0, 0, pl.ds(pl.multiple_of(clip(j) * STRIP + STRIP - 16, 16), 16), :].astype(F32)
            f_end = -(rows[15:16, HEAD_DIM:HEAD_DIM + 1] + rows[15:16, HEAD_DIM + 1:HEAD_DIM + 2]
                      + rows[15:16, HEAD_DIM + 2:HEAD_DIM + 3])
            gap = bound_q[:, s * STRIP:(s + 1) * STRIP] - f_end - carry[s][0]
            left.append(jnp.where(j >= 0, gap, NEG_BIG))
        return jnp.max(functools.reduce(jnp.maximum, left)) > DEAD_LOG2

    def body(state):
        i, _, cr = state
        u = STATIC_DEPTHS + 2 * i
        cr = pair(u, cr)
        return i + 1, alive(cr, u + 2), cr

    one = jnp.ones((1, STRIP), F32)
    carry = tuple((m[s], l[s], acc[s], one, one) for s in range(N_STRIPS))
    i_end, _, carry = lax.while_loop(lambda st: jnp.logical_and(st[0] < n_pairs, st[1]), body,
                                     (jnp.int32(0), alive(carry, STATIC_DEPTHS), carry))
    last = STATIC_DEPTHS + 2 * i_end - 1
    carry = pv(last - 1, p_a, carry, 0)
    carry = pv(last, p_b, carry, 1)
    for s in range(N_STRIPS):
        _, l, acc, _, _ = carry[s]
        o_ref[0, 0, :, s * STRIP:(s + 1) * STRIP] = (acc / l).astype(BF16)


def _attn_prompt(kernel, qT, k, vT, extra=()):
    b, h, kd, t = qT.shape
    nk, tk = vT.shape[2], vT.shape[4]
    tq = Q_TILE
    in_specs = [
        pl.BlockSpec((1, 1, kd, tq), lambda bi, hi, qi: (bi, hi, 0, qi)),
        pl.BlockSpec((1, 1, t, kd), lambda bi, hi, qi: (bi, hi, 0, 0)),
        pl.BlockSpec((1, 1, nk, HEAD_DIM, tk), lambda bi, hi, qi: (bi, hi, 0, 0, 0)),
    ] + [pl.BlockSpec(a.shape, lambda bi, hi, qi: (0, 0)) if a.ndim == 2 else
         pl.BlockSpec((1, 1) + a.shape[2:], lambda bi, hi, qi: (bi, hi, 0, 0)) for a in extra]
    scores = pltpu.VMEM((N_STRIPS, STRIP, STRIP), F32)
    probs = pltpu.VMEM((N_STRIPS, STRIP, STRIP), BF16)
    return pl.pallas_call(
        kernel,
        grid=(b, h, t // tq),
        in_specs=in_specs,
        out_specs=pl.BlockSpec((1, 1, HEAD_DIM, tq), lambda bi, hi, qi: (bi, hi, 0, qi)),
        out_shape=jax.ShapeDtypeStruct((b, h, HEAD_DIM, t), BF16),
        scratch_shapes=[scores, scores, probs, probs],
        compiler_params=_cparams(("parallel", "parallel", "arbitrary")),
        name=kernel.__name__.strip("_"),
    )(qT, k, vT, *extra)


def _sample_kv(ck_ref, kn_ref, cv_ref, vn_ref, h):
    kT = jnp.concatenate([ck_ref[0, h], kn_ref[0, h]], axis=1).astype(BF16)
    vT = jnp.concatenate([cv_ref[0, h], vn_ref[0, h]], axis=1).astype(BF16)
    return kT, vT


def _pv(p, vT):
    return lax.dot_general(p, vT, (((1,), (1,)), ((), ())), preferred_element_type=F32)


def _sb_sample_kernel(q_ref, ck_ref, kn_ref, cv_ref, vn_ref, tri_ref, o_ref):
    n_heads, tq = q_ref.shape[1], q_ref.shape[2]
    p_len = ck_ref.shape[3]
    nk = p_len + kn_ref.shape[3]
    row = lax.broadcasted_iota(jnp.int32, (tq, nk), 0)
    col = lax.broadcasted_iota(jnp.int32, (tq, nk), 1)
    mask = col < row + p_len
    tri = tri_ref[...]
    for h in range(n_heads):
        kT, vT = _sample_kv(ck_ref, kn_ref, cv_ref, vn_ref, h)
        q = (q_ref[0, h] * HEAD_DIM ** -0.5).astype(BF16)
        z = _dot(q, kT)
        ls_full = -(jnp.maximum(z, 0.0) + jnp.log1p(jnp.exp(-jnp.abs(z))))
        ls = jnp.where(mask, ls_full, 0.0)
        later = _dot(ls.astype(BF16), tri)
        wgt = jnp.where(mask, jnp.exp(z + ls_full + later), 0.0)
        o_ref[0, :, h * HEAD_DIM:(h + 1) * HEAD_DIM] = _pv(wgt.astype(BF16), vT)


def _fox_sample_kernel(q_ref, ck_ref, kn_ref, cv_ref, vn_ref, lf_ref, o_ref):
    n_heads, tq = q_ref.shape[1], q_ref.shape[2]
    p_len = ck_ref.shape[3]
    nk = p_len + kn_ref.shape[3]
    row = lax.broadcasted_iota(jnp.int32, (tq, nk), 0)
    col = lax.broadcasted_iota(jnp.int32, (tq, nk), 1)
    mask = col <= row + p_len
    diag = col == row + p_len
    f_all = _lane_cumsum(lf_ref[0])
    for h in range(n_heads):
        kT, vT = _sample_kv(ck_ref, kn_ref, cv_ref, vn_ref, h)
        q = (q_ref[0, h] * HEAD_DIM ** -0.5).astype(BF16)
        fk = f_all[h:h + 1, :]
        fq = jnp.sum(jnp.where(diag, fk, 0.0), axis=1, keepdims=True)
        z = _dot(q, kT) + fq - fk
        z = jnp.where(mask, z, NEG_BIG)
        p = jnp.exp(z - jnp.max(z, axis=1, keepdims=True))
        l = jnp.sum(p, axis=1, keepdims=True)
        o_ref[0, :, h * HEAD_DIM:(h + 1) * HEAD_DIM] = _pv(p.astype(BF16), vT) / l


def _attn_sample(kernel, q, ck, kn, cv, vn, extra, extra_spec):
    b, h, tq, hd = q.shape
    per_b = lambda a: pl.BlockSpec((1,) + a.shape[1:], lambda bi: (bi,) + (0,) * (a.ndim - 1))
    return pl.pallas_call(
        kernel,
        grid=(b,),
        in_specs=[per_b(a) for a in (q, ck, kn, cv, vn)] + [extra_spec],
        out_specs=pl.BlockSpec((1, tq, h * hd), lambda bi: (bi, 0, 0)),
        out_shape=jax.ShapeDtypeStruct((b, tq, h * hd), F32),
        compiler_params=_cparams(("parallel",)),
        name=kernel.__name__.strip("_"),
    )(q, ck, kn, cv, vn, extra)


def _route(lt, n_groups, epg):
    n = lt.shape[1]
    g = [lt[i:i + 1, :] for i in range(n_groups)]
    gmax = functools.reduce(jnp.maximum, g)
    g_w = 1.0 / functools.reduce(jnp.add, [jnp.exp(gi - gmax) for gi in g])
    is_g, taken = [], None
    for gi in g:
        hit = gi >= gmax
        if taken is not None:
            hit = jnp.logical_and(hit, jnp.logical_not(taken))
        taken = hit if taken is None else jnp.logical_or(taken, hit)
        is_g.append(hit)
    le = [lt[n_groups + i:n_groups + i + 1, :] for i in range(n_groups * epg)]
    e_sel = []
    for i in range(epg):
        v = le[(n_groups - 1) * epg + i]
        for gi in reversed(range(n_groups - 1)):
            v = jnp.where(is_g[gi], le[gi * epg + i], v)
        e_sel.append(v)
    emax = functools.reduce(jnp.maximum, e_sel)
    pe = [jnp.exp(v - emax) for v in e_sel]
    pden = functools.reduce(jnp.add, pe)
    prob = [p / pden for p in pe]

    def first_argmax(vals):
        vmax = functools.reduce(jnp.maximum, vals)
        hits, tk = [], None
        for v in vals:
            hit = v >= vmax
            if tk is not None:
                hit = jnp.logical_and(hit, jnp.logical_not(tk))
            tk = hit if tk is None else jnp.logical_or(tk, hit)
            hits.append(hit)
        return vmax, hits

    p1, t1 = first_argmax(prob)
    p2, t2 = first_argmax([jnp.where(t, -1.0, p) for t, p in zip(t1, prob)])
    tot = p1 + p2
    w1 = g_w * (p1 / tot)
    w2 = g_w * (p2 / tot)
    rows = lax.broadcasted_iota(jnp.int32, (LANES, n), 0)
    comb = jnp.zeros((LANES, n), F32)
    for gi in range(n_groups):
        for i in range(epg):
            val = jnp.where(is_g[gi], jnp.where(t1[i], w1, 0.0) + jnp.where(t2[i], w2, 0.0), 0.0)
            comb = jnp.where(rows == gi * epg + i, jnp.broadcast_to(val, (LANES, n)), comb)
    return comb


def _post_kernel(yaT_ref, ybT_ref, gates_ref, x_ref, mods_ref, wba_ref, wbb_ref, wout_ref,
                 gmoe_ref, wr2_ref, wrhi_ref, br_ref, x1_ref, h2_ref, comb_ref,
                 *, per_token, n_groups, epg):
    d = x_ref.shape[2]
    tdot = lambda aT, w: lax.dot_general(aT, w, (((0,), (0,)), ((), ())), preferred_element_type=F32)
    ua = tdot(yaT_ref[0], wba_ref[...])
    ub = tdot(ybT_ref[0], wbb_ref[...])
    gates = gates_ref[0].astype(F32)
    mix = gates[:, :d] * ua + gates[:, d:] * ub
    x1 = x_ref[0] + _mod(mods_ref, 2, per_token) * _dot(mix.astype(BF16), wout_ref[...])
    x1_ref[0] = x1
    h2 = _rms_mod(x1, gmoe_ref[...], _mod(mods_ref, 4, per_token), _mod(mods_ref, 3, per_token))
    h2_hi = h2.astype(BF16)
    h2_ref[0] = h2_hi
    h2_lo = (h2 - h2_hi.astype(F32)).astype(BF16)
    a = _dot(h2_hi, wr2_ref[...])
    logits = a[:, :LANES] + a[:, LANES:] + _dot(h2_lo, wrhi_ref[...]) + br_ref[...]
    comb_ref[0] = _route(logits.T, n_groups, epg).T


def _post(yaT, ybT, gates, x, mods, w_ba, w_bb, w_out, g_moe, wr2, wrhi, br, per_token, n_groups, epg):
    b, t, d = x.shape
    tm = min(POST_TILE, t)
    w = yaT.shape[1]
    const2 = lambda bi, ti: (0, 0)
    tok = lambda last: pl.BlockSpec((1, tm, last), lambda bi, ti: (bi, ti, 0))
    chan = pl.BlockSpec((1, w, tm), lambda bi, ti: (bi, 0, ti))
    if per_token:
        mods_spec = pl.BlockSpec(mods.shape, lambda bi, ti: (0, 0, 0))
    else:
        mods_spec = pl.BlockSpec((1, N_MOD, d), lambda bi, ti: (bi, 0, 0))
    in_specs = [chan, chan, tok(2 * d), tok(d), mods_spec,
                pl.BlockSpec(w_ba.shape, const2), pl.BlockSpec(w_bb.shape, const2),
                pl.BlockSpec(w_out.shape, const2), pl.BlockSpec((1, d), const2),
                pl.BlockSpec(wr2.shape, const2), pl.BlockSpec(wrhi.shape, const2),
                pl.BlockSpec((1, LANES), const2)]
    out_shape = (jax.ShapeDtypeStruct((b, t, d), F32),
                 jax.ShapeDtypeStruct((b, t, d), BF16),
                 jax.ShapeDtypeStruct((b, t, LANES), F32))
    return pl.pallas_call(
        functools.partial(_post_kernel, per_token=per_token, n_groups=n_groups, epg=epg),
        grid=(b, t // tm),
        in_specs=in_specs,
        out_specs=(tok(d), tok(d), tok(LANES)),
        out_shape=out_shape,
        compiler_params=_cparams(("parallel", "parallel")),
        name="post_sample" if per_token else "post_prompt",
    )(yaT, ybT, gates, x, mods, w_ba, w_bb, w_out, g_moe, wr2, wrhi, br)


def _moe_kernel(h2_ref, comb_ref, x1_ref, mods_ref, w13_ref, w2_ref, gfin_ref, y_ref, acc_ref,
                *, per_token, epg, hid):
    g = pl.program_id(1)

    @pl.when(g == 0)
    def _():
        acc_ref[...] = jnp.zeros_like(acc_ref)

    h2 = h2_ref[...]
    comb = comb_ref[...]
    lane = lax.broadcasted_iota(jnp.int32, comb.shape, 1)
    acts = []
    for c in range(0, epg, 2):
        a = _dot(h2, w13_ref[0, :, c * 2 * hid:(c + 2) * 2 * hid])
        for i in range(c, c + 2):
            a1 = a[:, (i - c) * 2 * hid:(i - c) * 2 * hid + hid]
            a3 = a[:, (i - c) * 2 * hid + hid:(i - c + 1) * 2 * hid]
            cw = jnp.sum(jnp.where(lane == g * epg + i, comb, 0.0), axis=1, keepdims=True)
            acts.append(((a1 * jax.nn.sigmoid(a1)) * a3 * cw).astype(BF16))
    acc_ref[...] += _dot(jnp.concatenate(acts, axis=1), w2_ref[0])

    @pl.when(g == pl.num_programs(1) - 1)
    def _():
        x2 = x1_ref[...] + _mod(mods_ref, 5, per_token) * acc_ref[...]
        ms = jnp.mean(x2 * x2, axis=-1, keepdims=True)
        y_ref[...] = (x2 * lax.rsqrt(ms + RMS_EPS)) * gfin_ref[...]


def _moe(h2, comb, x1, mods, w13, w2, g_final, per_token, tokens_per_batch, epg):
    n, d = x1.shape
    tm = min(MOE_TILE, n)
    n_groups = w13.shape[0]
    hid = w2.shape[1] // epg
    assert epg % 2 == 0
    tiles_per_batch = tokens_per_batch // tm if not per_token else 1
    tok = lambda last: pl.BlockSpec((tm, last), lambda i, g: (i, 0))
    if per_token:
        mods_spec = pl.BlockSpec(mods.shape, lambda i, g: (0, 0, 0))
    else:
        mods_spec = pl.BlockSpec((1, N_MOD, d), lambda i, g: (i // tiles_per_batch, 0, 0))
    return pl.pallas_call(
        functools.partial(_moe_kernel, per_token=per_token, epg=epg, hid=hid),
        grid=(n // tm, n_groups),
        in_specs=[tok(d), tok(LANES), tok(d), mods_spec,
                  pl.BlockSpec((1,) + w13.shape[1:], lambda i, g: (g, 0, 0)),
                  pl.BlockSpec((1,) + w2.shape[1:], lambda i, g: (g, 0, 0)),
                  pl.BlockSpec((1, d), lambda i, g: (0, 0))],
        out_specs=tok(d),
        out_shape=jax.ShapeDtypeStruct((n, d), F32),
        scratch_shapes=[pltpu.VMEM((tm, d), F32)],
        compiler_params=_cparams(("parallel", "arbitrary")),
        name="moe_sample" if per_token else "moe_prompt",
    )(h2, comb, x1, mods, w13, w2, g_final)


def kernel(x_prompt, x_sample, cache_sb_k, cache_sb_v, cache_fox_k, cache_fox_v, cache_fox_logf,
           c_prompt, c_sample, w_ada, b_ada, g_mix, w_in, b_f, w_ba, w_bb, w_gate, b_gate, w_out,
           g_moe, w_rg, b_rg, w_re, b_re, w1, w3, w2, g_final):
    depth = w_ada.shape[0]
    assert depth == 1, "single-layer trunk"
    bp, t, d = x_prompt.shape
    bs, ts, _ = x_sample.shape
    n_heads = cache_sb_k.shape[2]
    p_len = cache_sb_k.shape[3]
    assert cache_fox_k.shape[2] == n_heads and n_heads * HEAD_DIM * 6 + n_heads == w_in.shape[2]
    assert n_heads == 8 and t % Q_TILE == 0 and Q_TILE % PRE_TILE == 0
    n_groups = w_rg.shape[2]
    n_exp = w_re.shape[2]
    epg = n_exp // n_groups
    assert n_groups + n_exp <= LANES
    w = n_heads * HEAD_DIM
    ns = bs * ts

    w_in_p = jnp.pad(w_in[0], ((0, 0), (0, LANES - n_heads))).astype(BF16)
    b_f_p = jnp.pad(b_f[0], (0, LANES - n_heads)).reshape(1, LANES)
    w_gate_b = w_gate[0].astype(BF16)
    b_gate_r = b_gate[0].reshape(1, -1)
    g_mix_r = g_mix[0].reshape(1, d)
    g_moe_r = g_moe[0].reshape(1, d)
    g_fin_r = g_final.reshape(1, d)
    w_ba_b, w_bb_b, w_out_b = w_ba[0].astype(BF16), w_bb[0].astype(BF16), w_out[0].astype(BF16)
    w_r = jnp.pad(jnp.concatenate([w_rg[0], w_re[0]], axis=1), ((0, 0), (0, LANES - n_groups - n_exp)))
    w_r_hi = w_r.astype(BF16)
    w_r_lo = (w_r - w_r_hi.astype(F32)).astype(BF16)
    wr2 = jnp.concatenate([w_r_hi, w_r_lo], axis=1)
    b_r = jnp.pad(jnp.concatenate([b_rg[0], b_re[0]]), (0, LANES - n_groups - n_exp)).reshape(1, LANES)
    hid = w1.shape[3]
    w13 = jnp.concatenate([w1[0], w3[0]], axis=2).astype(BF16).reshape(n_groups, epg, d, 2 * hid)
    w13 = w13.transpose(0, 2, 1, 3).reshape(n_groups, d, epg * 2 * hid)
    w2_b = w2[0].astype(BF16).reshape(n_groups, epg * hid, d)

    ada = _ada(jnp.concatenate([c_prompt, c_sample], axis=0), w_ada[0], b_ada[0])
    mods_p = ada[:bp].reshape(bp, N_MOD, d)
    mods_s = jnp.repeat(ada[bp:].reshape(bs, N_MOD, d).transpose(1, 0, 2), ts, axis=1)

    hsel = jnp.repeat(jnp.eye(n_heads, dtype=BF16), HEAD_DIM, axis=1)
    (qsT, ks, vsT, qfT, kf, vfT, ksl, vsl, kfl, vfl, logf_p, kn2, gates_p) = _pre_prompt(
        x_prompt, mods_p, g_mix_r, w_in_p, b_f_p, w_gate_b, b_gate_r, hsel, n_heads)
    tk = PRE_TILE
    ids = jnp.arange(tk)
    ntri_p = -(ids[None, :] > ids[:, None]).astype(BF16)
    yaT = _attn_prompt(_sb_prompt_kernel, qsT, ks, vsT, (ntri_p,)).reshape(bp, w, t)
    ybT = _attn_prompt(_fox_prompt_kernel, qfT, kf, vfT, (kn2[:, :, None, :],)).reshape(bp, w, t)
    x1_p, h2_p, comb_p = _post(yaT, ybT, gates_p, x_prompt, mods_p, w_ba_b, w_bb_b, w_out_b,
                               g_moe_r, wr2, w_r_hi, b_r, False, n_groups, epg)
    y_prompt = _moe(h2_p.reshape(bp * t, d), comb_p.reshape(bp * t, LANES), x1_p.reshape(bp * t, d),
                    mods_p, w13, w2_b, g_fin_r, False, t, epg).reshape(bp, t, d)

    proj_s, logf_s, gates_s = _pre_sample(x_sample.reshape(ns, d), mods_s, g_mix_r, w_in_p, b_f_p,
                                          w_gate_b, b_gate_r, n_heads)
    heads = lambda i: proj_s[:, i * w:(i + 1) * w].reshape(bs, ts, n_heads, HEAD_DIM).transpose(0, 2, 1, 3)
    qa_s, ka_s, va_s, qb_s, kb_s, vb_s = [heads(i) for i in range(6)]
    lf_s = logf_s[:, :n_heads].reshape(bs, ts, n_heads).transpose(0, 2, 1)
    pad_k = lambda a: jnp.pad(jnp.swapaxes(a, 2, 3), ((0, 0), (0, 0), (0, 0), (0, LANES - ts)))
    chan = lambda cache: jnp.swapaxes(cache[0], 2, 3)
    nk = p_len + LANES
    ids = jnp.arange(nk)
    tri_s = (ids[:, None] > ids[None, :]).astype(BF16)
    lf_all = jnp.concatenate([cache_fox_logf[0], jnp.pad(lf_s, ((0, 0), (0, 0), (0, LANES - ts)))], axis=2)
    ya_s = _attn_sample(_sb_sample_kernel, qa_s, chan(cache_sb_k), pad_k(ka_s), chan(cache_sb_v), pad_k(va_s),
                        tri_s, pl.BlockSpec(tri_s.shape, lambda bi: (0, 0)))
    yb_s = _attn_sample(_fox_sample_kernel, qb_s, chan(cache_fox_k), pad_k(kb_s), chan(cache_fox_v), pad_k(vb_s),
                        lf_all, pl.BlockSpec((1, n_heads, nk), lambda bi: (bi, 0, 0)))
    to_chan = lambda y: y.reshape(ns, w).T.astype(BF16)[None]
    x1_s, h2_s, comb_s = _post(to_chan(ya_s), to_chan(yb_s), gates_s[None], x_sample.reshape(1, ns, d),
                               mods_s, w_ba_b, w_bb_b, w_out_b, g_moe_r, wr2, w_r_hi, b_r,
                               True, n_groups, epg)
    y_sample = _moe(h2_s[0], comb_s[0], x1_s[0], mods_s, w13, w2_b, g_fin_r, True, ns, epg).reshape(bs, ts, d)

    lead = lambda a: a[None]
    tok_major = lambda a: jnp.swapaxes(a, 2, 3)[None]
    return (y_prompt, y_sample,
            tok_major(ksl), tok_major(vsl), tok_major(kfl), tok_major(vfl), lead(logf_p),
            lead(ka_s), lead(va_s), lead(kb_s), lead(vb_s), lead(lf_s))
```

```python
import functools

import jax
import jax.numpy as jnp
from jax import lax
from jax.experimental import pallas as pl
from jax.experimental.pallas import tpu as pltpu

F32 = jnp.float32
BF16 = jnp.bfloat16

HEAD_DIM = 64
RMS_EPS = 1e-6
N_MOD = 6
LANES = 128
NEG_BIG = -1e30

PRE_TILE = 256
Q_TILE = 1024
STRIP = 256
N_STRIPS = Q_TILE // STRIP
LOG2E = 1.4426950408889634
DEAD_LOG2 = -128.0
NORM_SLACK = 1.02
POST_TILE = 512
MOE_TILE = 1024
VMEM_LIMIT = 56 * 1024 * 1024


def _cparams(sem):
    return pltpu.CompilerParams(dimension_semantics=sem, vmem_limit_bytes=VMEM_LIMIT)


def _log_sigmoid(x):
    return jnp.minimum(x, 0.0) - jnp.log1p(jnp.exp(-jnp.abs(x)))


def _rms_mod(x, g, scale, shift):
    ms = jnp.mean(x * x, axis=-1, keepdims=True)
    y = x * lax.rsqrt(ms + RMS_EPS)
    return (y * g) * (1.0 + scale) + shift


def _mod(mods_ref, i, per_token):
    return mods_ref[i] if per_token else mods_ref[0, i:i + 1, :]


def _dot(a, b):
    return jnp.dot(a, b, preferred_element_type=F32)


def _split3(f):
    hi = f.astype(BF16).astype(F32)
    r = f - hi
    mid = r.astype(BF16).astype(F32)
    lo = (r - mid).astype(BF16).astype(F32)
    return hi, mid, lo


def _lane_cumsum(x):
    n = x.shape[1]
    lane = lax.broadcasted_iota(jnp.int32, x.shape, 1)
    d = 1
    while d < n:
        x = x + jnp.where(lane >= d, pltpu.roll(x, d, axis=1), 0.0)
        d *= 2
    return x


def _ada_kernel(c_ref, w_ref, b_ref, o_ref):
    c = c_ref[...]
    s = c * jax.nn.sigmoid(c)
    o_ref[...] = jnp.dot(s, w_ref[...], preferred_element_type=F32,
                         precision=lax.Precision.HIGHEST) + b_ref[...]


def _ada(c_all, w_ada, b_ada):
    n, d = c_all.shape
    nout = w_ada.shape[1]
    tn = 1024
    return pl.pallas_call(
        _ada_kernel,
        grid=(nout // tn,),
        in_specs=[pl.BlockSpec((n, d), lambda j: (0, 0)),
                  pl.BlockSpec((d, tn), lambda j: (0, j)),
                  pl.BlockSpec((1, tn), lambda j: (0, j))],
        out_specs=pl.BlockSpec((n, tn), lambda j: (0, j)),
        out_shape=jax.ShapeDtypeStruct((n, nout), F32),
        compiler_params=_cparams(("arbitrary",)),
        name="ada",
    )(c_all, w_ada, b_ada.reshape(1, nout))


def _pre_core(x, mods_ref, gmix_ref, win_ref, wg_ref, bg_ref, per_token):
    h = _rms_mod(x, gmix_ref[...], _mod(mods_ref, 1, per_token), _mod(mods_ref, 0, per_token))
    hb = h.astype(BF16)
    proj = _dot(hb, win_ref[...])
    gates = jax.nn.sigmoid(_dot(hb, wg_ref[...]) + bg_ref[...])
    return proj, gates.astype(BF16)


def _pre_prompt_kernel(x_ref, mods_ref, gmix_ref, win_ref, bf_ref, wg_ref, bg_ref, hsel_ref,
                       qsT_ref, ks_ref, vsT_ref, qfT_ref, kf_ref, vfT_ref,
                       ksl_ref, vsl_ref, kfl_ref, vfl_ref, logf_ref, kn2_ref, gates_ref,
                       carry_ref, *, n_heads):
    tm = x_ref.shape[1]
    w = n_heads * HEAD_DIM

    @pl.when(pl.program_id(1) == 0)
    def _():
        carry_ref[...] = jnp.zeros_like(carry_ref)

    proj, gates = _pre_core(x_ref[0], mods_ref, gmix_ref, win_ref, wg_ref, bg_ref, False)
    gates_ref[0] = gates
    qa, ka, va, qb, kb, vb = [proj[:, i * w:(i + 1) * w] for i in range(6)]
    fg = proj[:, 6 * w:6 * w + LANES]
    scale = HEAD_DIM ** -0.5 * LOG2E
    qaT = (qa * scale).T
    kaT = ka.T
    vaT = va.T
    qbT = (qb * scale).T
    kbT = kb.T
    vbT = vb.T
    kn2_ref[0] = _dot(hsel_ref[...], (kbT * kbT).astype(BF16))

    logfT = _log_sigmoid(fg + bf_ref[...]).T[0:n_heads, :]
    logf_ref[0] = logfT
    f = _lane_cumsum(logfT) + carry_ref[:, 0:1]
    carry_ref[...] = jnp.broadcast_to(f[:, tm - 1:tm], carry_ref.shape)
    f_hi, f_mid, f_lo = _split3(f * LOG2E)

    row64 = lax.broadcasted_iota(jnp.int32, (HEAD_DIM, tm), 0)
    row8 = lax.broadcasted_iota(jnp.int32, (8, tm), 0)
    zeros64 = jnp.zeros((HEAD_DIM, tm), BF16)
    ke_parts = []
    for h in range(n_heads):
        def bc(a, n):
            return jnp.broadcast_to(a[h:h + 1, :], (n, tm))
        qe = jnp.where(row64 < 3, 1.0,
                       jnp.where(row64 == 3, bc(f_hi, HEAD_DIM),
                                 jnp.where(row64 == 4, bc(f_mid, HEAD_DIM),
                                           jnp.where(row64 == 5, bc(f_lo, HEAD_DIM), 0.0))))
        sl = slice(h * HEAD_DIM, (h + 1) * HEAD_DIM)
        qfT_ref[0, h, 0:HEAD_DIM, :] = qbT[sl, :].astype(BF16)
        qfT_ref[0, h, HEAD_DIM:2 * HEAD_DIM, :] = qe.astype(BF16)
        qsT_ref[0, h, 0:HEAD_DIM, :] = qaT[sl, :].astype(BF16)
        qsT_ref[0, h, HEAD_DIM:2 * HEAD_DIM, :] = zeros64
        vsT_ref[0, h, 0] = vaT[sl, :].astype(BF16)
        vfT_ref[0, h, 0] = vbT[sl, :].astype(BF16)
        ksl_ref[0, h] = kaT[sl, :]
        vsl_ref[0, h] = vaT[sl, :]
        kfl_ref[0, h] = kbT[sl, :]
        vfl_ref[0, h] = vbT[sl, :]
        ke_parts.append(
            jnp.where(row8 == 0, -bc(f_hi, 8),
                      jnp.where(row8 == 1, -bc(f_mid, 8),
                                jnp.where(row8 == 2, -bc(f_lo, 8),
                                          jnp.where(row8 < 6, 1.0, 0.0)))))
    ke_parts.append(jnp.zeros((LANES - 8 * n_heads, tm), F32))
    ke = jnp.concatenate(ke_parts, axis=0).T

    lane = lax.broadcasted_iota(jnp.int32, (tm, LANES), 1)
    for h in range(n_heads):
        base = (h // 2) * LANES
        ka_slab = ka[:, base:base + LANES]
        kb_slab = kb[:, base:base + LANES]
        if h % 2:
            ka_slab = pltpu.roll(ka_slab, HEAD_DIM, axis=1)
            kb_slab = pltpu.roll(kb_slab, HEAD_DIM, axis=1)
        ext = pltpu.roll(ke, HEAD_DIM - 8 * h, axis=1)
        ks_ref[0, h] = jnp.where(lane < HEAD_DIM, ka_slab, 0.0).astype(BF16)
        kf_ref[0, h] = jnp.where(lane < HEAD_DIM, kb_slab,
                                 jnp.where(lane < HEAD_DIM + 8, ext, 0.0)).astype(BF16)


def _pre_prompt(x, mods, g_mix, w_in_p, b_f_p, w_gate, b_gate, hsel, n_heads):
    b, t, d = x.shape
    tm = PRE_TILE
    nt = t // tm
    h, hd = n_heads, HEAD_DIM
    const2 = lambda bi, ti: (0, 0)
    head_t = lambda bi, ti: (bi, 0, 0, ti)
    head_s = lambda bi, ti: (bi, 0, ti, 0)
    blk_t = lambda bi, ti: (bi, 0, ti, 0, 0)
    out_shape = (
        jax.ShapeDtypeStruct((b, h, 2 * hd, t), BF16),
        jax.ShapeDtypeStruct((b, h, t, 2 * hd), BF16),
        jax.ShapeDtypeStruct((b, h, nt, hd, tm), BF16),
        jax.ShapeDtypeStruct((b, h, 2 * hd, t), BF16),
        jax.ShapeDtypeStruct((b, h, t, 2 * hd), BF16),
        jax.ShapeDtypeStruct((b, h, nt, hd, tm), BF16),
        jax.ShapeDtypeStruct((b, h, hd, t), F32),
        jax.ShapeDtypeStruct((b, h, hd, t), F32),
        jax.ShapeDtypeStruct((b, h, hd, t), F32),
        jax.ShapeDtypeStruct((b, h, hd, t), F32),
        jax.ShapeDtypeStruct((b, h, t), F32),
        jax.ShapeDtypeStruct((b, h, t), F32),
        jax.ShapeDtypeStruct((b, t, w_gate.shape[1]), BF16),
    )
    out_specs = (
        pl.BlockSpec((1, h, 2 * hd, tm), head_t),
        pl.BlockSpec((1, h, tm, 2 * hd), head_s),
        pl.BlockSpec((1, h, 1, hd, tm), blk_t),
        pl.BlockSpec((1, h, 2 * hd, tm), head_t),
        pl.BlockSpec((1, h, tm, 2 * hd), head_s),
        pl.BlockSpec((1, h, 1, hd, tm), blk_t),
        pl.BlockSpec((1, h, hd, tm), head_t),
        pl.BlockSpec((1, h, hd, tm), head_t),
        pl.BlockSpec((1, h, hd, tm), head_t),
        pl.BlockSpec((1, h, hd, tm), head_t),
        pl.BlockSpec((1, h, tm), lambda bi, ti: (bi, 0, ti)),
        pl.BlockSpec((1, h, tm), lambda bi, ti: (bi, 0, ti)),
        pl.BlockSpec((1, tm, w_gate.shape[1]), lambda bi, ti: (bi, ti, 0)),
    )
    in_specs = [
        pl.BlockSpec((1, tm, d), lambda bi, ti: (bi, ti, 0)),
        pl.BlockSpec((1, N_MOD, d), lambda bi, ti: (bi, 0, 0)),
        pl.BlockSpec((1, d), const2),
        pl.BlockSpec(w_in_p.shape, const2),
        pl.BlockSpec((1, LANES), const2),
        pl.BlockSpec(w_gate.shape, const2),
        pl.BlockSpec((1, w_gate.shape[1]), const2),
        pl.BlockSpec(hsel.shape, const2),
    ]
    return pl.pallas_call(
        functools.partial(_pre_prompt_kernel, n_heads=n_heads),
        grid=(b, nt),
        in_specs=in_specs,
        out_specs=out_specs,
        out_shape=out_shape,
        scratch_shapes=[pltpu.VMEM((h, LANES), F32)],
        compiler_params=_cparams(("arbitrary", "arbitrary")),
        name="pre_prompt",
    )(x, mods, g_mix, w_in_p, b_f_p, w_gate, b_gate, hsel)


def _pre_sample_kernel(x_ref, mods_ref, gmix_ref, win_ref, bf_ref, wg_ref, bg_ref,
                       proj_ref, logf_ref, gates_ref, *, n_heads):
    w = n_heads * HEAD_DIM
    proj, gates = _pre_core(x_ref[...], mods_ref, gmix_ref, win_ref, wg_ref, bg_ref, True)
    proj_ref[...] = proj
    gates_ref[...] = gates
    logf_ref[...] = _log_sigmoid(proj[:, 6 * w:6 * w + LANES] + bf_ref[...])


def _pre_sample(x, mods_tok, g_mix, w_in_p, b_f_p, w_gate, b_gate, n_heads):
    n, d = x.shape
    full = lambda a: pl.BlockSpec(a.shape, lambda i: (0,) * a.ndim)
    args = (x, mods_tok, g_mix, w_in_p, b_f_p, w_gate, b_gate)
    out_shape = (jax.ShapeDtypeStruct((n, w_in_p.shape[1]), F32),
                 jax.ShapeDtypeStruct((n, LANES), F32),
                 jax.ShapeDtypeStruct((n, w_gate.shape[1]), BF16))
    return pl.pallas_call(
        functools.partial(_pre_sample_kernel, n_heads=n_heads),
        grid=(1,),
        in_specs=[full(a) for a in args],
        out_specs=tuple(pl.BlockSpec(s.shape, lambda i: (0, 0)) for s in out_shape),
        out_shape=out_shape,
        compiler_params=_cparams(("arbitrary",)),
        name="pre_sample",
    )(*args)


SB_DEPTHS = 3
FOX_DEPTHS = 2


def _attn_parts(qT_ref, k_ref, vT_ref, static_depths):
    assert qT_ref.shape[3] == N_STRIPS * STRIP and vT_ref.shape[4] == STRIP
    nk = vT_ref.shape[2]
    base = pl.program_id(2) * N_STRIPS
    q_strip = lambda s: qT_ref[0, 0, :, s * STRIP:(s + 1) * STRIP]
    k_block = lambda j: k_ref[0, 0, pl.ds(pl.multiple_of(j * STRIP, STRIP), STRIP), :]
    v_block = lambda j: vT_ref[0, 0, j]
    block = lambda s, u: base + s - u
    clip = lambda j: jnp.clip(j, 0, nk - 1)

    def qk(u, z_scr, strips=range(N_STRIPS)):
        for s in strips:
            z_scr[s] = _dot(k_block(clip(block(s, u))), q_strip(s))

    row = lax.broadcasted_iota(jnp.int32, (STRIP, STRIP), 0)
    col = lax.broadcasted_iota(jnp.int32, (STRIP, STRIP), 1)
    n_pairs = (base + N_STRIPS - static_depths + 1) // 2
    return base, q_strip, k_block, v_block, block, clip, qk, row, col, n_pairs


def _sb_prompt_kernel(qT_ref, k_ref, vT_ref, ntri_ref, o_ref, z_a, z_b, w_a, w_b):
    STATIC_DEPTHS = SB_DEPTHS
    base, q_strip, k_block, v_block, block, clip, qk, row, col, n_pairs = _attn_parts(
        qT_ref, k_ref, vT_ref, STATIC_DEPTHS)
    diag = row < col
    sign = jnp.uint32(0x80000000)

    def softplus2(z):
        neg_abs = lax.bitcast_convert_type(lax.bitcast_convert_type(z, jnp.uint32) | sign, F32)
        return jnp.maximum(z, 0.0) + jnp.log2(1.0 + jnp.exp2(neg_abs))

    def suffix(sp):
        sp_b = sp.astype(BF16)
        later = _dot(ntri_ref[...], sp_b)
        return later, later[0:1, :] - sp_b[0:1, :].astype(F32)

    def masked(x, s, u):
        if isinstance(u, int) and u == 0:
            return jnp.where(diag, x, 0.0)
        if isinstance(u, int) and u <= s:
            return x
        return jnp.where(block(s, u) >= 0, x, 0.0)

    w_a[...] = jnp.zeros(w_a.shape, BF16)
    w_b[...] = jnp.zeros(w_b.shape, BF16)
    subs = [(s, u) for u in range(STATIC_DEPTHS) for s in range(N_STRIPS)]
    zs = [_dot(k_block(clip(block(s, u))), q_strip(s)) for s, u in subs]
    qk(STATIC_DEPTHS, z_a)
    c = [jnp.zeros((1, STRIP), F32)] * N_STRIPS
    acc = [jnp.zeros((HEAD_DIM, STRIP), F32)] * N_STRIPS
    mids = []
    for (s, u), z in zip(subs, zs):
        sp_full = softplus2(z)
        later, total = suffix(masked(sp_full, s, u))
        mids.append((z - sp_full, later, c[s]))
        c[s] = c[s] + total
    for (s, u), (lsig, later, c_before) in zip(subs, mids):
        wgt = masked(jnp.exp2(lsig + later), s, u)
        acc[s] = acc[s] + _dot(v_block(clip(block(s, u))), wgt.astype(BF16)) * jnp.exp2(c_before)

    def pv(u, w_scr, carry, which, strips=range(N_STRIPS)):
        return tuple((cr[0], cr[1] + _dot(v_block(clip(block(s, u))), w_scr[s]) * cr[2 + which]) + cr[2:]
                     if s in strips else cr for s, cr in enumerate(carry))

    def softplus_phase(z_scr, carry, which, u):
        carry = list(carry)
        mids = []
        for s in range(N_STRIPS):
            c, acc, sa, sb = carry[s]
            z = z_scr[s]
            sp_full = softplus2(z)
            later, total = suffix(masked(sp_full, s, u))
            mids.append((z - sp_full, later))
            scale = jnp.exp2(c)
            carry[s] = (c + total, acc, sa, scale) if which else (c + total, acc, scale, sb)
        return tuple(carry), mids

    def weight_phase(w_scr, mids, u):
        for s, (lsig, later) in enumerate(mids):
            w_scr[s] = masked(jnp.exp2(lsig + later), s, u).astype(BF16)

    def pair(u, carry):
        carry = pv(u - 2, w_a, carry, 0)
        for s in range(N_STRIPS):
            carry = pv(u - 1, w_b, carry, 1, (s,))
            qk(u + 1, z_b, (s,))
        carry, mids_a = softplus_phase(z_a, carry, 0, u)
        qk(u + 2, z_a)
        carry, mids_b = softplus_phase(z_b, carry, 1, u + 1)
        weight_phase(w_a, mids_a, u)
        weight_phase(w_b, mids_b, u + 1)
        return carry

    def alive(carry, u_next):
        left = [jnp.where(block(s, u_next) >= 0, cr[0], NEG_BIG) for s, cr in enumerate(carry)]
        return jnp.max(functools.reduce(jnp.maximum, left)) > DEAD_LOG2

    def body(state):
        i, _, cr = state
        u = STATIC_DEPTHS + 2 * i
        cr = pair(u, cr)
        return i + 1, alive(cr, u + 2), cr

    one = jnp.ones((1, STRIP), F32)
    carry = tuple((c[s], acc[s], one, one) for s in range(N_STRIPS))
    i_end, _, carry = lax.while_loop(lambda st: jnp.logical_and(st[0] < n_pairs, st[1]), body,
                                     (jnp.int32(0), alive(carry, STATIC_DEPTHS), carry))
    last = STATIC_DEPTHS + 2 * i_end - 1
    carry = pv(last - 1, w_a, carry, 0)
    carry = pv(last, w_b, carry, 1)
    for s in range(N_STRIPS):
        o_ref[0, 0, :, s * STRIP:(s + 1) * STRIP] = carry[s][1].astype(BF16)


def _fox_prompt_kernel(qT_ref, k_ref, vT_ref, kn2_ref, o_ref, z_a, z_b, p_a, p_b):
    STATIC_DEPTHS = FOX_DEPTHS
    base, q_strip, k_block, v_block, block, clip, qk, row, col, n_pairs = _attn_parts(
        qT_ref, k_ref, vT_ref, STATIC_DEPTHS)
    diag = row <= col

    def masked(sc, s, u):
        if isinstance(u, int) and u == 0:
            return jnp.where(diag, sc, NEG_BIG)
        if isinstance(u, int) and u <= s:
            return sc
        return jnp.where(block(s, u) >= 0, sc, NEG_BIG)

    p_a[...] = jnp.zeros(p_a.shape, BF16)
    p_b[...] = jnp.zeros(p_b.shape, BF16)
    subs = [(s, u) for u in range(STATIC_DEPTHS) for s in range(N_STRIPS)]
    zs = [_dot(k_block(clip(block(s, u))), q_strip(s)) for s, u in subs]
    qk(STATIC_DEPTHS, z_a)
    m = [jnp.full((1, STRIP), NEG_BIG, F32)] * N_STRIPS
    l = [jnp.zeros((1, STRIP), F32)] * N_STRIPS
    acc = [jnp.zeros((HEAD_DIM, STRIP), F32)] * N_STRIPS
    for (s, u), sc in zip(subs, zs):
        sc = masked(sc, s, u)
        m_new = jnp.maximum(m[s], jnp.max(sc, axis=0, keepdims=True))
        alpha = jnp.exp2(m[s] - m_new)
        p = jnp.exp2(sc - m_new)
        l[s] = alpha * l[s] + jnp.sum(p, axis=0, keepdims=True)
        acc[s] = alpha * acc[s] + _dot(v_block(clip(block(s, u))), p.astype(BF16))
        m[s] = m_new

    def pv(u, p_scr, carry, which, strips=range(N_STRIPS)):
        return tuple((cr[0], cr[1], cr[3 + which] * cr[2] + _dot(v_block(clip(block(s, u))), p_scr[s])) + cr[3:]
                     if s in strips else cr for s, cr in enumerate(carry))

    def valu(z_scr, p_scr, carry, which, u):
        out = []
        for s in range(N_STRIPS):
            m, l, acc, aa, ab = carry[s]
            m_new = jnp.maximum(m, jnp.max(masked(z_scr[s], s, u), axis=0, keepdims=True))
            alpha = jnp.exp2(m - m_new)
            p = jnp.exp2(masked(z_scr[s], s, u) - m_new)
            p_scr[s] = p.astype(BF16)
            l = alpha * l + jnp.sum(p, axis=0, keepdims=True)
            out.append((m_new, l, acc, aa, alpha) if which else (m_new, l, acc, alpha, ab))
        return tuple(out)

    def pair(u, carry):
        carry = pv(u - 2, p_a, carry, 0)
        carry = pv(u - 1, p_b, carry, 1)
        qk(u + 1, z_b)
        carry = valu(z_a, p_a, carry, 0, u)
        qk(u + 2, z_a)
        return valu(z_b, p_b, carry, 1, u + 1)

    q_all = qT_ref[0, 0]
    q_f = q_all[0:HEAD_DIM, :].astype(F32)
    q_norm = jnp.sqrt(jnp.sum(q_f * q_f, axis=0, keepdims=True))
    f_rows = q_all[HEAD_DIM:HEAD_DIM + 16, :].astype(F32)
    f_q = f_rows[3:4, :] + f_rows[4:5, :] + f_rows[5:6, :]
    k_norm = jnp.sqrt(jnp.max(kn2_ref[0, 0], axis=1, keepdims=True))
    bound_q = NORM_SLACK * q_norm * k_norm + f_q

    def alive(carry, u_next):
        left = []
        for s in range(N_STRIPS):
            j = block(s, u_next)
            rows = k_ref[0, 0, pl.ds(pl.multiple_of(clip(j) * STRIP + STRIP - 16, 16), 16), :].astype(F32)
            f_end = -(rows[15:16, HEAD_DIM:HEAD_DIM + 1] + rows[15:16, HEAD_DIM + 1:HEAD_DIM + 2]
                      + rows[15:16, HEAD_DIM + 2:HEAD_DIM + 3])
            gap = bound_q[:, s * STRIP:(s + 1) * STRIP] - f_end - carry[s][0]
            left.append(jnp.where(j >= 0, gap, NEG_BIG))
        return jnp.max(functools.reduce(jnp.maximum, left)) > DEAD_LOG2

    def body(state):
        i, _, cr = state
        u = STATIC_DEPTHS + 2 * i
        cr = pair(u, cr)
        return i + 1, alive(cr, u + 2), cr

    one = jnp.ones((1, STRIP), F32)
    carry = tuple((m[s], l[s], acc[s], one, one) for s in range(N_STRIPS))
    i_end, _, carry = lax.while_loop(lambda st: jnp.logical_and(st[0] < n_pairs, st[1]), body,
                                     (jnp.int32(0), alive(carry, STATIC_DEPTHS), carry))
    last = STATIC_DEPTHS + 2 * i_end - 1
    carry = pv(last - 1, p_a, carry, 0)
    carry = pv(last, p_b, carry, 1)
    for s in range(N_STRIPS):
        _, l, acc, _, _ = carry[s]
        o_ref[0, 0, :, s * STRIP:(s + 1) * STRIP] = (acc / l).astype(BF16)


def _attn_prompt(kernel, qT, k, vT, extra=()):
    b, h, kd, t = qT.shape
    nk, tk = vT.shape[2], vT.shape[4]
    tq = Q_TILE
    in_specs = [
        pl.BlockSpec((1, 1, kd, tq), lambda bi, hi, qi: (bi, hi, 0, qi)),
        pl.BlockSpec((1, 1, t, kd), lambda bi, hi, qi: (bi, hi, 0, 0)),
        pl.BlockSpec((1, 1, nk, HEAD_DIM, tk), lambda bi, hi, qi: (bi, hi, 0, 0, 0)),
    ] + [pl.BlockSpec(a.shape, lambda bi, hi, qi: (0, 0)) if a.ndim == 2 else
         pl.BlockSpec((1, 1) + a.shape[2:], lambda bi, hi, qi: (bi, hi, 0, 0)) for a in extra]
    scores = pltpu.VMEM((N_STRIPS, STRIP, STRIP), F32)
    probs = pltpu.VMEM((N_STRIPS, STRIP, STRIP), BF16)
    return pl.pallas_call(
        kernel,
        grid=(b, h, t // tq),
        in_specs=in_specs,
        out_specs=pl.BlockSpec((1, 1, HEAD_DIM, tq), lambda bi, hi, qi: (bi, hi, 0, qi)),
        out_shape=jax.ShapeDtypeStruct((b, h, HEAD_DIM, t), BF16),
        scratch_shapes=[scores, scores, probs, probs],
        compiler_params=_cparams(("parallel", "parallel", "arbitrary")),
        name=kernel.__name__.strip("_"),
    )(qT, k, vT, *extra)


def _sample_kv(ck_ref, kn_ref, cv_ref, vn_ref, h):
    kT = jnp.concatenate([ck_ref[0, h], kn_ref[0, h]], axis=1).astype(BF16)
    vT = jnp.concatenate([cv_ref[0, h], vn_ref[0, h]], axis=1).astype(BF16)
    return kT, vT


def _pv(p, vT):
    return lax.dot_general(p, vT, (((1,), (1,)), ((), ())), preferred_element_type=F32)


def _sb_sample_kernel(q_ref, ck_ref, kn_ref, cv_ref, vn_ref, tri_ref, o_ref):
    n_heads, tq = q_ref.shape[1], q_ref.shape[2]
    p_len = ck_ref.shape[3]
    nk = p_len + kn_ref.shape[3]
    row = lax.broadcasted_iota(jnp.int32, (tq, nk), 0)
    col = lax.broadcasted_iota(jnp.int32, (tq, nk), 1)
    mask = col < row + p_len
    heads = []
    for h in range(n_heads):
        kT, vT = _sample_kv(ck_ref, kn_ref, cv_ref, vn_ref, h)
        q = (q_ref[0, h] * HEAD_DIM ** -0.5).astype(BF16)
        z = _dot(q, kT)
        ls_full = -(jnp.maximum(z, 0.0) + jnp.log1p(jnp.exp(-jnp.abs(z))))
        heads.append((z + ls_full, jnp.where(mask, ls_full, 0.0).astype(BF16), vT))
    later = _dot(jnp.concatenate([ls for _, ls, _ in heads], axis=0), tri_ref[...])
    for h, (lsig, _, vT) in enumerate(heads):
        wgt = jnp.where(mask, jnp.exp(lsig + later[h * tq:(h + 1) * tq, :]), 0.0)
        o_ref[0, :, h * HEAD_DIM:(h + 1) * HEAD_DIM] = _pv(wgt.astype(BF16), vT)


def _fox_sample_kernel(q_ref, ck_ref, kn_ref, cv_ref, vn_ref, lf_ref, o_ref):
    n_heads, tq = q_ref.shape[1], q_ref.shape[2]
    p_len = ck_ref.shape[3]
    nk = p_len + kn_ref.shape[3]
    row = lax.broadcasted_iota(jnp.int32, (tq, nk), 0)
    col = lax.broadcasted_iota(jnp.int32, (tq, nk), 1)
    mask = col <= row + p_len
    diag = col == row + p_len
    f_all = _lane_cumsum(lf_ref[0])
    for h in range(n_heads):
        kT, vT = _sample_kv(ck_ref, kn_ref, cv_ref, vn_ref, h)
        q = (q_ref[0, h] * HEAD_DIM ** -0.5).astype(BF16)
        fk = f_all[h:h + 1, :]
        fq = jnp.sum(jnp.where(diag, fk, 0.0), axis=1, keepdims=True)
        z = _dot(q, kT) + fq - fk
        z = jnp.where(mask, z, NEG_BIG)
        p = jnp.exp(z - jnp.max(z, axis=1, keepdims=True))
        l = jnp.sum(p, axis=1, keepdims=True)
        o_ref[0, :, h * HEAD_DIM:(h + 1) * HEAD_DIM] = _pv(p.astype(BF16), vT) / l


def _attn_sample(kernel, q, ck, kn, cv, vn, extra, extra_spec):
    b, h, tq, hd = q.shape
    per_b = lambda a: pl.BlockSpec((1,) + a.shape[1:], lambda bi: (bi,) + (0,) * (a.ndim - 1))
    return pl.pallas_call(
        kernel,
        grid=(b,),
        in_specs=[per_b(a) for a in (q, ck, kn, cv, vn)] + [extra_spec],
        out_specs=pl.BlockSpec((1, tq, h * hd), lambda bi: (bi, 0, 0)),
        out_shape=jax.ShapeDtypeStruct((b, tq, h * hd), F32),
        compiler_params=_cparams(("parallel",)),
        name=kernel.__name__.strip("_"),
    )(q, ck, kn, cv, vn, extra)


def _route(lt, n_groups, epg):
    n = lt.shape[1]
    g = [lt[i:i + 1, :] for i in range(n_groups)]
    gmax = functools.reduce(jnp.maximum, g)
    g_w = 1.0 / functools.reduce(jnp.add, [jnp.exp(gi - gmax) for gi in g])
    is_g, taken = [], None
    for gi in g:
        hit = gi >= gmax
        if taken is not None:
            hit = jnp.logical_and(hit, jnp.logical_not(taken))
        taken = hit if taken is None else jnp.logical_or(taken, hit)
        is_g.append(hit)
    le = [lt[n_groups + i:n_groups + i + 1, :] for i in range(n_groups * epg)]
    e_sel = []
    for i in range(epg):
        v = le[(n_groups - 1) * epg + i]
        for gi in reversed(range(n_groups - 1)):
            v = jnp.where(is_g[gi], le[gi * epg + i], v)
        e_sel.append(v)
    emax = functools.reduce(jnp.maximum, e_sel)
    pe = [jnp.exp(v - emax) for v in e_sel]
    pden = functools.reduce(jnp.add, pe)
    prob = [p / pden for p in pe]

    def first_argmax(vals):
        vmax = functools.reduce(jnp.maximum, vals)
        hits, tk = [], None
        for v in vals:
            hit = v >= vmax
            if tk is not None:
                hit = jnp.logical_and(hit, jnp.logical_not(tk))
            tk = hit if tk is None else jnp.logical_or(tk, hit)
            hits.append(hit)
        return vmax, hits

    p1, t1 = first_argmax(prob)
    p2, t2 = first_argmax([jnp.where(t, -1.0, p) for t, p in zip(t1, prob)])
    tot = p1 + p2
    w1 = g_w * (p1 / tot)
    w2 = g_w * (p2 / tot)
    rows = lax.broadcasted_iota(jnp.int32, (LANES, n), 0)
    comb = jnp.zeros((LANES, n), F32)
    for gi in range(n_groups):
        for i in range(epg):
            val = jnp.where(is_g[gi], jnp.where(t1[i], w1, 0.0) + jnp.where(t2[i], w2, 0.0), 0.0)
            comb = jnp.where(rows == gi * epg + i, jnp.broadcast_to(val, (LANES, n)), comb)
    return comb


def _post_kernel(yaT_ref, ybT_ref, gates_ref, x_ref, mods_ref, wba_ref, wbb_ref, wout_ref,
                 gmoe_ref, wr2_ref, wrhi_ref, br_ref, x1_ref, h2_ref, comb_ref,
                 *, per_token, n_groups, epg):
    d = x_ref.shape[2]
    tdot = lambda aT, w: lax.dot_general(aT, w, (((0,), (0,)), ((), ())), preferred_element_type=F32)
    ua = tdot(yaT_ref[0], wba_ref[...])
    ub = tdot(ybT_ref[0], wbb_ref[...])
    gates = gates_ref[0].astype(F32)
    mix = gates[:, :d] * ua + gates[:, d:] * ub
    x1 = x_ref[0] + _mod(mods_ref, 2, per_token) * _dot(mix.astype(BF16), wout_ref[...])
    x1_ref[0] = x1
    h2 = _rms_mod(x1, gmoe_ref[...], _mod(mods_ref, 4, per_token), _mod(mods_ref, 3, per_token))
    h2_hi = h2.astype(BF16)
    h2_ref[0] = h2_hi
    h2_lo = (h2 - h2_hi.astype(F32)).astype(BF16)
    a = _dot(h2_hi, wr2_ref[...])
    logits = a[:, :LANES] + a[:, LANES:] + _dot(h2_lo, wrhi_ref[...]) + br_ref[...]
    comb_ref[0] = _route(logits.T, n_groups, epg).T


def _post(yaT, ybT, gates, x, mods, w_ba, w_bb, w_out, g_moe, wr2, wrhi, br, per_token, n_groups, epg):
    b, t, d = x.shape
    tm = min(POST_TILE, t)
    w = yaT.shape[1]
    const2 = lambda bi, ti: (0, 0)
    tok = lambda last: pl.BlockSpec((1, tm, last), lambda bi, ti: (bi, ti, 0))
    chan = pl.BlockSpec((1, w, tm), lambda bi, ti: (bi, 0, ti))
    if per_token:
        mods_spec = pl.BlockSpec(mods.shape, lambda bi, ti: (0, 0, 0))
    else:
        mods_spec = pl.BlockSpec((1, N_MOD, d), lambda bi, ti: (bi, 0, 0))
    in_specs = [chan, chan, tok(2 * d), tok(d), mods_spec,
                pl.BlockSpec(w_ba.shape, const2), pl.BlockSpec(w_bb.shape, const2),
                pl.BlockSpec(w_out.shape, const2), pl.BlockSpec((1, d), const2),
                pl.BlockSpec(wr2.shape, const2), pl.BlockSpec(wrhi.shape, const2),
                pl.BlockSpec((1, LANES), const2)]
    out_shape = (jax.ShapeDtypeStruct((b, t, d), F32),
                 jax.ShapeDtypeStruct((b, t, d), BF16),
                 jax.ShapeDtypeStruct((b, t, LANES), F32))
    return pl.pallas_call(
        functools.partial(_post_kernel, per_token=per_token, n_groups=n_groups, epg=epg),
        grid=(b, t // tm),
        in_specs=in_specs,
        out_specs=(tok(d), tok(d), tok(LANES)),
        out_shape=out_shape,
        compiler_params=_cparams(("parallel", "parallel")),
        name="post_sample" if per_token else "post_prompt",
    )(yaT, ybT, gates, x, mods, w_ba, w_bb, w_out, g_moe, wr2, wrhi, br)


def _moe_kernel(h2_ref, comb_ref, x1_ref, mods_ref, w13_ref, w2_ref, gfin_ref, y_ref, acc_ref,
                *, per_token, epg, hid):
    g = pl.program_id(1)

    @pl.when(g == 0)
    def _():
        acc_ref[...] = jnp.zeros_like(acc_ref)

    h2 = h2_ref[...]
    comb = comb_ref[...]
    lane = lax.broadcasted_iota(jnp.int32, comb.shape, 1)
    ups = [_dot(h2, w13_ref[0, i]) for i in range(epg)]
    total = None
    for c in range(0, epg, 2):
        acts = []
        for i in range(c, c + 2):
            a1 = ups[i][:, :hid]
            a3 = ups[i][:, hid:]
            cw = jnp.sum(jnp.where(lane == g * epg + i, comb, 0.0), axis=1, keepdims=True)
            acts.append(((a1 * jax.nn.sigmoid(a1)) * a3 * cw).astype(BF16))
        part = _dot(jnp.concatenate(acts, axis=1), w2_ref[0, c * hid:(c + 2) * hid, :])
        total = part if total is None else total + part
    acc_ref[...] += total

    @pl.when(g == pl.num_programs(1) - 1)
    def _():
        x2 = x1_ref[...] + _mod(mods_ref, 5, per_token) * acc_ref[...]
        ms = jnp.mean(x2 * x2, axis=-1, keepdims=True)
        y_ref[...] = (x2 * lax.rsqrt(ms + RMS_EPS)) * gfin_ref[...]


def _moe(h2, comb, x1, mods, w13, w2, g_final, per_token, tokens_per_batch, epg):
    n, d = x1.shape
    tm = min(MOE_TILE, n)
    n_groups = w13.shape[0]
    hid = w2.shape[1] // epg
    assert epg % 2 == 0
    tiles_per_batch = tokens_per_batch // tm if not per_token else 1
    tok = lambda last: pl.BlockSpec((tm, last), lambda i, g: (i, 0))
    if per_token:
        mods_spec = pl.BlockSpec(mods.shape, lambda i, g: (0, 0, 0))
    else:
        mods_spec = pl.BlockSpec((1, N_MOD, d), lambda i, g: (i // tiles_per_batch, 0, 0))
    return pl.pallas_call(
        functools.partial(_moe_kernel, per_token=per_token, epg=epg, hid=hid),
        grid=(n // tm, n_groups),
        in_specs=[tok(d), tok(LANES), tok(d), mods_spec,
                  pl.BlockSpec((1,) + w13.shape[1:], lambda i, g: (g, 0, 0, 0)),
                  pl.BlockSpec((1,) + w2.shape[1:], lambda i, g: (g, 0, 0)),
                  pl.BlockSpec((1, d), lambda i, g: (0, 0))],
        out_specs=tok(d),
        out_shape=jax.ShapeDtypeStruct((n, d), F32),
        scratch_shapes=[pltpu.VMEM((tm, d), F32)],
        compiler_params=_cparams(("parallel", "arbitrary")),
        name="moe_sample" if per_token else "moe_prompt",
    )(h2, comb, x1, mods, w13, w2, g_final)


def kernel(x_prompt, x_sample, cache_sb_k, cache_sb_v, cache_fox_k, cache_fox_v, cache_fox_logf,
           c_prompt, c_sample, w_ada, b_ada, g_mix, w_in, b_f, w_ba, w_bb, w_gate, b_gate, w_out,
           g_moe, w_rg, b_rg, w_re, b_re, w1, w3, w2, g_final):
    depth = w_ada.shape[0]
    assert depth == 1, "single-layer trunk"
    bp, t, d = x_prompt.shape
    bs, ts, _ = x_sample.shape
    n_heads = cache_sb_k.shape[2]
    p_len = cache_sb_k.shape[3]
    assert cache_fox_k.shape[2] == n_heads and n_heads * HEAD_DIM * 6 + n_heads == w_in.shape[2]
    assert n_heads == 8 and t % Q_TILE == 0 and Q_TILE % PRE_TILE == 0
    n_groups = w_rg.shape[2]
    n_exp = w_re.shape[2]
    epg = n_exp // n_groups
    assert n_groups + n_exp <= LANES
    w = n_heads * HEAD_DIM
    ns = bs * ts

    w_in_p = jnp.pad(w_in[0], ((0, 0), (0, LANES - n_heads))).astype(BF16)
    b_f_p = jnp.pad(b_f[0], (0, LANES - n_heads)).reshape(1, LANES)
    w_gate_b = w_gate[0].astype(BF16)
    b_gate_r = b_gate[0].reshape(1, -1)
    g_mix_r = g_mix[0].reshape(1, d)
    g_moe_r = g_moe[0].reshape(1, d)
    g_fin_r = g_final.reshape(1, d)
    w_ba_b, w_bb_b, w_out_b = w_ba[0].astype(BF16), w_bb[0].astype(BF16), w_out[0].astype(BF16)
    w_r = jnp.pad(jnp.concatenate([w_rg[0], w_re[0]], axis=1), ((0, 0), (0, LANES - n_groups - n_exp)))
    w_r_hi = w_r.astype(BF16)
    w_r_lo = (w_r - w_r_hi.astype(F32)).astype(BF16)
    wr2 = jnp.concatenate([w_r_hi, w_r_lo], axis=1)
    b_r = jnp.pad(jnp.concatenate([b_rg[0], b_re[0]]), (0, LANES - n_groups - n_exp)).reshape(1, LANES)
    hid = w1.shape[3]
    w13 = jnp.concatenate([w1[0], w3[0]], axis=2).astype(BF16).reshape(n_groups, epg, d, 2 * hid)
    w2_b = w2[0].astype(BF16).reshape(n_groups, epg * hid, d)

    ada = _ada(jnp.concatenate([c_prompt, c_sample], axis=0), w_ada[0], b_ada[0])
    mods_p = ada[:bp].reshape(bp, N_MOD, d)
    mods_s = jnp.repeat(ada[bp:].reshape(bs, N_MOD, d).transpose(1, 0, 2), ts, axis=1)

    hsel = jnp.repeat(jnp.eye(n_heads, dtype=BF16), HEAD_DIM, axis=1)
    (qsT, ks, vsT, qfT, kf, vfT, ksl, vsl, kfl, vfl, logf_p, kn2, gates_p) = _pre_prompt(
        x_prompt, mods_p, g_mix_r, w_in_p, b_f_p, w_gate_b, b_gate_r, hsel, n_heads)
    tk = PRE_TILE
    ids = jnp.arange(tk)
    ntri_p = -(ids[None, :] > ids[:, None]).astype(BF16)
    yaT = _attn_prompt(_sb_prompt_kernel, qsT, ks, vsT, (ntri_p,)).reshape(bp, w, t)
    ybT = _attn_prompt(_fox_prompt_kernel, qfT, kf, vfT, (kn2[:, :, None, :],)).reshape(bp, w, t)
    x1_p, h2_p, comb_p = _post(yaT, ybT, gates_p, x_prompt, mods_p, w_ba_b, w_bb_b, w_out_b,
                               g_moe_r, wr2, w_r_hi, b_r, False, n_groups, epg)
    y_prompt = _moe(h2_p.reshape(bp * t, d), comb_p.reshape(bp * t, LANES), x1_p.reshape(bp * t, d),
                    mods_p, w13, w2_b, g_fin_r, False, t, epg).reshape(bp, t, d)

    proj_s, logf_s, gates_s = _pre_sample(x_sample.reshape(ns, d), mods_s, g_mix_r, w_in_p, b_f_p,
                                          w_gate_b, b_gate_r, n_heads)
    heads = lambda i: proj_s[:, i * w:(i + 1) * w].reshape(bs, ts, n_heads, HEAD_DIM).transpose(0, 2, 1, 3)
    qa_s, ka_s, va_s, qb_s, kb_s, vb_s = [heads(i) for i in range(6)]
    lf_s = logf_s[:, :n_heads].reshape(bs, ts, n_heads).transpose(0, 2, 1)
    pad_k = lambda a: jnp.pad(jnp.swapaxes(a, 2, 3), ((0, 0), (0, 0), (0, 0), (0, LANES - ts)))
    chan = lambda cache: jnp.swapaxes(cache[0], 2, 3)
    nk = p_len + LANES
    ids = jnp.arange(nk)
    tri_s = (ids[:, None] > ids[None, :]).astype(BF16)
    lf_all = jnp.concatenate([cache_fox_logf[0], jnp.pad(lf_s, ((0, 0), (0, 0), (0, LANES - ts)))], axis=2)
    ya_s = _attn_sample(_sb_sample_kernel, qa_s, chan(cache_sb_k), pad_k(ka_s), chan(cache_sb_v), pad_k(va_s),
                        tri_s, pl.BlockSpec(tri_s.shape, lambda bi: (0, 0)))
    yb_s = _attn_sample(_fox_sample_kernel, qb_s, chan(cache_fox_k), pad_k(kb_s), chan(cache_fox_v), pad_k(vb_s),
                        lf_all, pl.BlockSpec((1, n_heads, nk), lambda bi: (bi, 0, 0)))
    to_chan = lambda y: y.reshape(ns, w).T.astype(BF16)[None]
    x1_s, h2_s, comb_s = _post(to_chan(ya_s), to_chan(yb_s), gates_s[None], x_sample.reshape(1, ns, d),
                               mods_s, w_ba_b, w_bb_b, w_out_b, g_moe_r, wr2, w_r_hi, b_r,
                               True, n_groups, epg)
    y_sample = _moe(h2_s[0], comb_s[0], x1_s[0], mods_s, w13, w2_b, g_fin_r, True, ns, epg).reshape(bs, ts, d)

    lead = lambda a: a[None]
    tok_major = lambda a: jnp.swapaxes(a, 2, 3)[None]
    return (y_prompt, y_sample,
            tok_major(ksl), tok_major(vsl), tok_major(kfl), tok_major(vfl), lead(logf_p),
            lead(ka_s), lead(va_s), lead(kb_s), lead(vb_s), lead(lf_s))
```

```python
import functools

import jax
import jax.numpy as jnp
from jax import lax
from jax.experimental import pallas as pl
from jax.experimental.pallas import tpu as pltpu

F32 = jnp.float32
BF16 = jnp.bfloat16

HEAD_DIM = 64
RMS_EPS = 1e-6
N_MOD = 6
LANES = 128
NEG_BIG = -1e30

PRE_TILE = 512
Q_TILE = 1024
STRIP = 256
N_STRIPS = Q_TILE // STRIP
LOG2E = 1.4426950408889634
DEAD_LOG2 = -128.0
NORM_SLACK = 1.02
POST_TILE = 512
MOE_TILE = 1024
VMEM_LIMIT = 56 * 1024 * 1024


def _cparams(sem):
    return pltpu.CompilerParams(dimension_semantics=sem, vmem_limit_bytes=VMEM_LIMIT)


def _log_sigmoid(x):
    return jnp.minimum(x, 0.0) - jnp.log1p(jnp.exp(-jnp.abs(x)))


def _rms_mod(x, g, scale, shift):
    ms = jnp.mean(x * x, axis=-1, keepdims=True)
    y = x * lax.rsqrt(ms + RMS_EPS)
    return (y * g) * (1.0 + scale) + shift


def _mod(mods_ref, i, per_token):
    return mods_ref[i] if per_token else mods_ref[0, i:i + 1, :]


def _dot(a, b):
    return jnp.dot(a, b, preferred_element_type=F32)


def _split3(f):
    hi = f.astype(BF16).astype(F32)
    r = f - hi
    mid = r.astype(BF16).astype(F32)
    lo = (r - mid).astype(BF16).astype(F32)
    return hi, mid, lo


def _lane_cumsum(x):
    n = x.shape[1]
    lane = lax.broadcasted_iota(jnp.int32, x.shape, 1)
    d = 1
    while d < n:
        x = x + jnp.where(lane >= d, pltpu.roll(x, d, axis=1), 0.0)
        d *= 2
    return x


def _ada_kernel(c_ref, w_ref, b_ref, o_ref):
    c = c_ref[...]
    s = c * jax.nn.sigmoid(c)
    o_ref[...] = jnp.dot(s, w_ref[...], preferred_element_type=F32,
                         precision=lax.Precision.HIGHEST) + b_ref[...]


def _ada(c_all, w_ada, b_ada):
    n, d = c_all.shape
    nout = w_ada.shape[1]
    tn = 1024
    return pl.pallas_call(
        _ada_kernel,
        grid=(nout // tn,),
        in_specs=[pl.BlockSpec((n, d), lambda j: (0, 0)),
                  pl.BlockSpec((d, tn), lambda j: (0, j)),
                  pl.BlockSpec((1, tn), lambda j: (0, j))],
        out_specs=pl.BlockSpec((n, tn), lambda j: (0, j)),
        out_shape=jax.ShapeDtypeStruct((n, nout), F32),
        compiler_params=_cparams(("arbitrary",)),
        name="ada",
    )(c_all, w_ada, b_ada.reshape(1, nout))


def _pre_core(x, mods_ref, gmix_ref, win_ref, wg_ref, bg_ref, per_token):
    h = _rms_mod(x, gmix_ref[...], _mod(mods_ref, 1, per_token), _mod(mods_ref, 0, per_token))
    hb = h.astype(BF16)
    proj = _dot(hb, win_ref[...])
    gates = jax.nn.sigmoid(_dot(hb, wg_ref[...]) + bg_ref[...])
    return proj, gates.astype(BF16)


def _pre_prompt_kernel(x_ref, mods_ref, gmix_ref, win_ref, bf_ref, wg_ref, bg_ref, hsel_ref,
                       qsT_ref, ks_ref, vsT_ref, qfT_ref, kf_ref, vfT_ref,
                       ksl_ref, vsl_ref, kfl_ref, vfl_ref, logf_ref, kn2_ref, gates_ref,
                       carry_ref, *, n_heads):
    tm = x_ref.shape[1]
    w = n_heads * HEAD_DIM

    @pl.when(pl.program_id(1) == 0)
    def _():
        carry_ref[...] = jnp.zeros_like(carry_ref)

    proj, gates = _pre_core(x_ref[0], mods_ref, gmix_ref, win_ref, wg_ref, bg_ref, False)
    gates_ref[0] = gates
    qa, ka, va, qb, kb, vb = [proj[:, i * w:(i + 1) * w] for i in range(6)]
    fg = proj[:, 6 * w:6 * w + LANES]
    scale = HEAD_DIM ** -0.5 * LOG2E
    qaT = (qa * scale).T
    kaT = ka.T
    vaT = va.T
    qbT = (qb * scale).T
    kbT = kb.T
    vbT = vb.T
    kn2_ref[0] = _dot(hsel_ref[...], (kbT * kbT).astype(BF16))

    logfT = _log_sigmoid(fg + bf_ref[...]).T[0:n_heads, :]
    logf_ref[0] = logfT
    f = _lane_cumsum(logfT) + carry_ref[:, 0:1]
    carry_ref[...] = jnp.broadcast_to(f[:, tm - 1:tm], carry_ref.shape)
    f_hi, f_mid, f_lo = _split3(f * LOG2E)

    row64 = lax.broadcasted_iota(jnp.int32, (HEAD_DIM, tm), 0)
    row8 = lax.broadcasted_iota(jnp.int32, (8, tm), 0)
    zeros64 = jnp.zeros((HEAD_DIM, tm), BF16)
    ke_parts = []
    for h in range(n_heads):
        def bc(a, n):
            return jnp.broadcast_to(a[h:h + 1, :], (n, tm))
        qe = jnp.where(row64 < 3, 1.0,
                       jnp.where(row64 == 3, bc(f_hi, HEAD_DIM),
                                 jnp.where(row64 == 4, bc(f_mid, HEAD_DIM),
                                           jnp.where(row64 == 5, bc(f_lo, HEAD_DIM), 0.0))))
        sl = slice(h * HEAD_DIM, (h + 1) * HEAD_DIM)
        qfT_ref[0, h, 0:HEAD_DIM, :] = qbT[sl, :].astype(BF16)
        qfT_ref[0, h, HEAD_DIM:2 * HEAD_DIM, :] = qe.astype(BF16)
        qsT_ref[0, h, 0:HEAD_DIM, :] = qaT[sl, :].astype(BF16)
        qsT_ref[0, h, HEAD_DIM:2 * HEAD_DIM, :] = zeros64
        for j in range(tm // STRIP):
            vsT_ref[0, h, j] = vaT[sl, j * STRIP:(j + 1) * STRIP].astype(BF16)
            vfT_ref[0, h, j] = vbT[sl, j * STRIP:(j + 1) * STRIP].astype(BF16)
        ksl_ref[0, h] = kaT[sl, :]
        vsl_ref[0, h] = vaT[sl, :]
        kfl_ref[0, h] = kbT[sl, :]
        vfl_ref[0, h] = vbT[sl, :]
        ke_parts.append(
            jnp.where(row8 == 0, -bc(f_hi, 8),
                      jnp.where(row8 == 1, -bc(f_mid, 8),
                                jnp.where(row8 == 2, -bc(f_lo, 8),
                                          jnp.where(row8 < 6, 1.0, 0.0)))))
    ke_parts.append(jnp.zeros((LANES - 8 * n_heads, tm), F32))
    ke = jnp.concatenate(ke_parts, axis=0).T

    lane = lax.broadcasted_iota(jnp.int32, (tm, LANES), 1)
    for h in range(n_heads):
        base = (h // 2) * LANES
        ka_slab = ka[:, base:base + LANES]
        kb_slab = kb[:, base:base + LANES]
        if h % 2:
            ka_slab = pltpu.roll(ka_slab, HEAD_DIM, axis=1)
            kb_slab = pltpu.roll(kb_slab, HEAD_DIM, axis=1)
        ext = pltpu.roll(ke, HEAD_DIM - 8 * h, axis=1)
        ks_ref[0, h] = jnp.where(lane < HEAD_DIM, ka_slab, 0.0).astype(BF16)
        kf_ref[0, h] = jnp.where(lane < HEAD_DIM, kb_slab,
                                 jnp.where(lane < HEAD_DIM + 8, ext, 0.0)).astype(BF16)


def _pre_prompt(x, mods, g_mix, w_in_p, b_f_p, w_gate, b_gate, hsel, n_heads):
    b, t, d = x.shape
    tm = PRE_TILE
    nt = t // tm
    h, hd = n_heads, HEAD_DIM
    const2 = lambda bi, ti: (0, 0)
    head_t = lambda bi, ti: (bi, 0, 0, ti)
    head_s = lambda bi, ti: (bi, 0, ti, 0)
    blk_t = lambda bi, ti: (bi, 0, ti, 0, 0)
    out_shape = (
        jax.ShapeDtypeStruct((b, h, 2 * hd, t), BF16),
        jax.ShapeDtypeStruct((b, h, t, 2 * hd), BF16),
        jax.ShapeDtypeStruct((b, h, t // STRIP, hd, STRIP), BF16),
        jax.ShapeDtypeStruct((b, h, 2 * hd, t), BF16),
        jax.ShapeDtypeStruct((b, h, t, 2 * hd), BF16),
        jax.ShapeDtypeStruct((b, h, t // STRIP, hd, STRIP), BF16),
        jax.ShapeDtypeStruct((b, h, hd, t), F32),
        jax.ShapeDtypeStruct((b, h, hd, t), F32),
        jax.ShapeDtypeStruct((b, h, hd, t), F32),
        jax.ShapeDtypeStruct((b, h, hd, t), F32),
        jax.ShapeDtypeStruct((b, h, t), F32),
        jax.ShapeDtypeStruct((b, h, t), F32),
        jax.ShapeDtypeStruct((b, t, w_gate.shape[1]), BF16),
    )
    out_specs = (
        pl.BlockSpec((1, h, 2 * hd, tm), head_t),
        pl.BlockSpec((1, h, tm, 2 * hd), head_s),
        pl.BlockSpec((1, h, tm // STRIP, hd, STRIP), blk_t),
        pl.BlockSpec((1, h, 2 * hd, tm), head_t),
        pl.BlockSpec((1, h, tm, 2 * hd), head_s),
        pl.BlockSpec((1, h, tm // STRIP, hd, STRIP), blk_t),
        pl.BlockSpec((1, h, hd, tm), head_t),
        pl.BlockSpec((1, h, hd, tm), head_t),
        pl.BlockSpec((1, h, hd, tm), head_t),
        pl.BlockSpec((1, h, hd, tm), head_t),
        pl.BlockSpec((1, h, tm), lambda bi, ti: (bi, 0, ti)),
        pl.BlockSpec((1, h, tm), lambda bi, ti: (bi, 0, ti)),
        pl.BlockSpec((1, tm, w_gate.shape[1]), lambda bi, ti: (bi, ti, 0)),
    )
    in_specs = [
        pl.BlockSpec((1, tm, d), lambda bi, ti: (bi, ti, 0)),
        pl.BlockSpec((1, N_MOD, d), lambda bi, ti: (bi, 0, 0)),
        pl.BlockSpec((1, d), const2),
        pl.BlockSpec(w_in_p.shape, const2, pipeline_mode=pl.Buffered(1)),
        pl.BlockSpec((1, LANES), const2),
        pl.BlockSpec(w_gate.shape, const2, pipeline_mode=pl.Buffered(1)),
        pl.BlockSpec((1, w_gate.shape[1]), const2),
        pl.BlockSpec(hsel.shape, const2),
    ]
    return pl.pallas_call(
        functools.partial(_pre_prompt_kernel, n_heads=n_heads),
        grid=(b, nt),
        in_specs=in_specs,
        out_specs=out_specs,
        out_shape=out_shape,
        scratch_shapes=[pltpu.VMEM((h, LANES), F32)],
        compiler_params=_cparams(("arbitrary", "arbitrary")),
        name="pre_prompt",
    )(x, mods, g_mix, w_in_p, b_f_p, w_gate, b_gate, hsel)


def _pre_sample_kernel(x_ref, mods_ref, gmix_ref, win_ref, bf_ref, wg_ref, bg_ref,
                       proj_ref, logf_ref, gates_ref, *, n_heads):
    w = n_heads * HEAD_DIM
    proj, gates = _pre_core(x_ref[...], mods_ref, gmix_ref, win_ref, wg_ref, bg_ref, True)
    proj_ref[...] = proj
    gates_ref[...] = gates
    logf_ref[...] = _log_sigmoid(proj[:, 6 * w:6 * w + LANES] + bf_ref[...])


def _pre_sample(x, mods_tok, g_mix, w_in_p, b_f_p, w_gate, b_gate, n_heads):
    n, d = x.shape
    full = lambda a: pl.BlockSpec(a.shape, lambda i: (0,) * a.ndim)
    args = (x, mods_tok, g_mix, w_in_p, b_f_p, w_gate, b_gate)
    out_shape = (jax.ShapeDtypeStruct((n, w_in_p.shape[1]), F32),
                 jax.ShapeDtypeStruct((n, LANES), F32),
                 jax.ShapeDtypeStruct((n, w_gate.shape[1]), BF16))
    return pl.pallas_call(
        functools.partial(_pre_sample_kernel, n_heads=n_heads),
        grid=(1,),
        in_specs=[full(a) for a in args],
        out_specs=tuple(pl.BlockSpec(s.shape, lambda i: (0, 0)) for s in out_shape),
        out_shape=out_shape,
        compiler_params=_cparams(("arbitrary",)),
        name="pre_sample",
    )(*args)


SB_DEPTHS = 3
FOX_DEPTHS = 4


def _attn_parts(qT_ref, k_ref, vT_ref, static_depths):
    assert qT_ref.shape[3] == N_STRIPS * STRIP and vT_ref.shape[4] == STRIP
    nk = vT_ref.shape[2]
    base = pl.program_id(2) * N_STRIPS
    q_strip = lambda s: qT_ref[0, 0, :, s * STRIP:(s + 1) * STRIP]
    k_block = lambda j: k_ref[0, 0, pl.ds(pl.multiple_of(j * STRIP, STRIP), STRIP), :]
    v_block = lambda j: vT_ref[0, 0, j]
    block = lambda s, u: base + s - u
    clip = lambda j: jnp.clip(j, 0, nk - 1)

    def qk(u, z_scr, strips=range(N_STRIPS)):
        for s in strips:
            z_scr[s] = _dot(k_block(clip(block(s, u))), q_strip(s))

    row = lax.broadcasted_iota(jnp.int32, (STRIP, STRIP), 0)
    col = lax.broadcasted_iota(jnp.int32, (STRIP, STRIP), 1)
    n_pairs = (base + N_STRIPS - static_depths + 1) // 2
    return base, q_strip, k_block, v_block, block, clip, qk, row, col, n_pairs


def _sb_prompt_kernel(qT_ref, k_ref, vT_ref, ntri_ref, o_ref, z_a, z_b, w_a, w_b):
    STATIC_DEPTHS = SB_DEPTHS
    base, q_strip, k_block, v_block, block, clip, qk, row, col, n_pairs = _attn_parts(
        qT_ref, k_ref, vT_ref, STATIC_DEPTHS)
    diag = row < col
    sign = jnp.uint32(0x80000000)

    def softplus2(z):
        neg_abs = lax.bitcast_convert_type(lax.bitcast_convert_type(z, jnp.uint32) | sign, F32)
        return jnp.maximum(z, 0.0) + jnp.log2(1.0 + jnp.exp2(neg_abs))

    def suffix(sp):
        sp_b = sp.astype(BF16)
        later = _dot(ntri_ref[...], sp_b)
        return later, later[0:1, :] - sp_b[0:1, :].astype(F32)

    def masked(x, s, u):
        if isinstance(u, int) and u == 0:
            return jnp.where(diag, x, 0.0)
        if isinstance(u, int) and u <= s:
            return x
        return jnp.where(block(s, u) >= 0, x, 0.0)

    w_a[...] = jnp.zeros(w_a.shape, BF16)
    w_b[...] = jnp.zeros(w_b.shape, BF16)
    subs = [(s, u) for u in range(STATIC_DEPTHS) for s in range(N_STRIPS)]
    zs = [_dot(k_block(clip(block(s, u))), q_strip(s)) for s, u in subs]
    qk(STATIC_DEPTHS, z_a)
    c = [jnp.zeros((1, STRIP), F32)] * N_STRIPS
    acc = [jnp.zeros((HEAD_DIM, STRIP), F32)] * N_STRIPS
    mids = []
    for (s, u), z in zip(subs, zs):
        sp_full = softplus2(z)
        later, total = suffix(masked(sp_full, s, u))
        mids.append((z - sp_full, later, c[s]))
        c[s] = c[s] + total
    for (s, u), (lsig, later, c_before) in zip(subs, mids):
        wgt = masked(jnp.exp2(lsig + later), s, u)
        acc[s] = acc[s] + _dot(v_block(clip(block(s, u))), wgt.astype(BF16)) * jnp.exp2(c_before)

    def pv(u, w_scr, carry, which, strips=range(N_STRIPS)):
        return tuple((cr[0], cr[1] + _dot(v_block(clip(block(s, u))), w_scr[s]) * cr[2 + which]) + cr[2:]
                     if s in strips else cr for s, cr in enumerate(carry))

    def softplus_phase(z_scr, carry, which, u):
        carry = list(carry)
        mids = []
        for s in range(N_STRIPS):
            c, acc, sa, sb = carry[s]
            z = z_scr[s]
            sp_full = softplus2(z)
            later, total = suffix(masked(sp_full, s, u))
            mids.append((z - sp_full, later))
            scale = jnp.exp2(c)
            carry[s] = (c + total, acc, sa, scale) if which else (c + total, acc, scale, sb)
        return tuple(carry), mids

    def weight_phase(w_scr, mids, u):
        for s, (lsig, later) in enumerate(mids):
            w_scr[s] = masked(jnp.exp2(lsig + later), s, u).astype(BF16)

    def pair(u, carry):
        carry = pv(u - 2, w_a, carry, 0)
        for s in range(N_STRIPS):
            carry = pv(u - 1, w_b, carry, 1, (s,))
            qk(u + 1, z_b, (s,))
        carry, mids_a = softplus_phase(z_a, carry, 0, u)
        qk(u + 2, z_a)
        carry, mids_b = softplus_phase(z_b, carry, 1, u + 1)
        weight_phase(w_a, mids_a, u)
        weight_phase(w_b, mids_b, u + 1)
        return carry

    def alive(carry, u_next):
        left = [jnp.where(block(s, u_next) >= 0, cr[0], NEG_BIG) for s, cr in enumerate(carry)]
        return jnp.max(functools.reduce(jnp.maximum, left)) > DEAD_LOG2

    def body(state):
        i, _, cr = state
        u = STATIC_DEPTHS + 2 * i
        cr = pair(u, cr)
        return i + 1, alive(cr, u + 2), cr

    one = jnp.ones((1, STRIP), F32)
    carry = tuple((c[s], acc[s], one, one) for s in range(N_STRIPS))
    i_end, _, carry = lax.while_loop(lambda st: jnp.logical_and(st[0] < n_pairs, st[1]), body,
                                     (jnp.int32(0), alive(carry, STATIC_DEPTHS), carry))
    last = STATIC_DEPTHS + 2 * i_end - 1
    carry = pv(last - 1, w_a, carry, 0)
    carry = pv(last, w_b, carry, 1)
    for s in range(N_STRIPS):
        o_ref[0, 0, :, s * STRIP:(s + 1) * STRIP] = carry[s][1].astype(BF16)


def _fox_prompt_kernel(qT_ref, k_ref, vT_ref, kn2_ref, o_ref, z_a, z_b, p_a, p_b):
    STATIC_DEPTHS = FOX_DEPTHS
    base, q_strip, k_block, v_block, block, clip, qk, row, col, n_pairs = _attn_parts(
        qT_ref, k_ref, vT_ref, STATIC_DEPTHS)
    diag = row <= col

    def masked(sc, s, u):
        if isinstance(u, int) and u == 0:
            return jnp.where(diag, sc, NEG_BIG)
        if isinstance(u, int) and u <= s:
            return sc
        return jnp.where(block(s, u) >= 0, sc, NEG_BIG)

    p_a[...] = jnp.zeros(p_a.shape, BF16)
    p_b[...] = jnp.zeros(p_b.shape, BF16)
    subs = [(s, u) for u in range(STATIC_DEPTHS) for s in range(N_STRIPS)]
    zs = [_dot(k_block(clip(block(s, u))), q_strip(s)) for s, u in subs]
    qk(STATIC_DEPTHS, z_a)
    m = [jnp.full((1, STRIP), NEG_BIG, F32)] * N_STRIPS
    l = [jnp.zeros((1, STRIP), F32)] * N_STRIPS
    acc = [jnp.zeros((HEAD_DIM, STRIP), F32)] * N_STRIPS
    for (s, u), sc in zip(subs, zs):
        sc = masked(sc, s, u)
        m_new = jnp.maximum(m[s], jnp.max(sc, axis=0, keepdims=True))
        alpha = jnp.exp2(m[s] - m_new)
        p = jnp.exp2(sc - m_new)
        l[s] = alpha * l[s] + jnp.sum(p, axis=0, keepdims=True)
        acc[s] = alpha * acc[s] + _dot(v_block(clip(block(s, u))), p.astype(BF16))
        m[s] = m_new

    def pv(u, p_scr, carry, which, strips=range(N_STRIPS)):
        return tuple((cr[0], cr[1], cr[3 + which] * cr[2] + _dot(v_block(clip(block(s, u))), p_scr[s])) + cr[3:]
                     if s in strips else cr for s, cr in enumerate(carry))

    def valu(z_scr, p_scr, carry, which, u):
        out = []
        for s in range(N_STRIPS):
            m, l, acc, aa, ab = carry[s]
            m_new = jnp.maximum(m, jnp.max(masked(z_scr[s], s, u), axis=0, keepdims=True))
            alpha = jnp.exp2(m - m_new)
            p = jnp.exp2(masked(z_scr[s], s, u) - m_new)
            p_scr[s] = p.astype(BF16)
            l = alpha * l + jnp.sum(p, axis=0, keepdims=True)
            out.append((m_new, l, acc, aa, alpha) if which else (m_new, l, acc, alpha, ab))
        return tuple(out)

    def pair(u, carry):
        carry = pv(u - 2, p_a, carry, 0)
        carry = pv(u - 1, p_b, carry, 1)
        qk(u + 1, z_b)
        carry = valu(z_a, p_a, carry, 0, u)
        qk(u + 2, z_a)
        return valu(z_b, p_b, carry, 1, u + 1)

    q_all = qT_ref[0, 0]
    q_f = q_all[0:HEAD_DIM, :].astype(F32)
    q_norm = jnp.sqrt(jnp.sum(q_f * q_f, axis=0, keepdims=True))
    f_rows = q_all[HEAD_DIM:HEAD_DIM + 16, :].astype(F32)
    f_q = f_rows[3:4, :] + f_rows[4:5, :] + f_rows[5:6, :]
    k_norm = jnp.sqrt(jnp.max(kn2_ref[0, 0], axis=1, keepdims=True))
    bound_q = NORM_SLACK * q_norm * k_norm + f_q

    def alive(carry, u_next):
        left = []
        for s in range(N_STRIPS):
            j = block(s, u_next)
            rows = k_ref[0, 0, pl.ds(pl.multiple_of(clip(j) * STRIP + STRIP - 16, 16), 16), :].astype(F32)
            f_end = -(rows[15:16, HEAD_DIM:HEAD_DIM + 1] + rows[15:16, HEAD_DIM + 1:HEAD_DIM + 2]
                      + rows[15:16, HEAD_DIM + 2:HEAD_DIM + 3])
            gap = bound_q[:, s * STRIP:(s + 1) * STRIP] - f_end - carry[s][0]
            left.append(jnp.where(j >= 0, gap, NEG_BIG))
        return jnp.max(functools.reduce(jnp.maximum, left)) > DEAD_LOG2

    def body(state):
        i, _, cr = state
        u = STATIC_DEPTHS + 2 * i
        cr = pair(u, cr)
        return i + 1, alive(cr, u + 2), cr

    one = jnp.ones((1, STRIP), F32)
    carry = tuple((m[s], l[s], acc[s], one, one) for s in range(N_STRIPS))
    i_end, _, carry = lax.while_loop(lambda st: jnp.logical_and(st[0] < n_pairs, st[1]), body,
                                     (jnp.int32(0), alive(carry, STATIC_DEPTHS), carry))
    last = STATIC_DEPTHS + 2 * i_end - 1
    carry = pv(last - 1, p_a, carry, 0)
    carry = pv(last, p_b, carry, 1)
    for s in range(N_STRIPS):
        _, l, acc, _, _ = carry[s]
        o_ref[0, 0, :, s * STRIP:(s + 1) * STRIP] = (acc / l).astype(BF16)


def _attn_prompt(kernel, qT, k, vT, extra=()):
    b, h, kd, t = qT.shape
    nk, tk = vT.shape[2], vT.shape[4]
    tq = Q_TILE
    in_specs = [
        pl.BlockSpec((1, 1, kd, tq), lambda bi, hi, qi: (bi, hi, 0, qi)),
        pl.BlockSpec((1, 1, t, kd), lambda bi, hi, qi: (bi, hi, 0, 0)),
        pl.BlockSpec((1, 1, nk, HEAD_DIM, tk), lambda bi, hi, qi: (bi, hi, 0, 0, 0)),
    ] + [pl.BlockSpec(a.shape, lambda bi, hi, qi: (0, 0)) if a.ndim == 2 else
         pl.BlockSpec((1, 1) + a.shape[2:], lambda bi, hi, qi: (bi, hi, 0, 0)) for a in extra]
    scores = pltpu.VMEM((N_STRIPS, STRIP, STRIP), F32)
    probs = pltpu.VMEM((N_STRIPS, STRIP, STRIP), BF16)
    return pl.pallas_call(
        kernel,
        grid=(b, h, t // tq),
        in_specs=in_specs,
        out_specs=pl.BlockSpec((1, 1, HEAD_DIM, tq), lambda bi, hi, qi: (bi, hi, 0, qi)),
        out_shape=jax.ShapeDtypeStruct((b, h, HEAD_DIM, t), BF16),
        scratch_shapes=[scores, scores, probs, probs],
        compiler_params=_cparams(("parallel", "parallel", "arbitrary")),
        name=kernel.__name__.strip("_"),
    )(qT, k, vT, *extra)


def _sample_kv(ck_ref, kn_ref, cv_ref, vn_ref, h):
    kT = jnp.concatenate([ck_ref[0, h], kn_ref[0, h]], axis=1).astype(BF16)
    vT = jnp.concatenate([cv_ref[0, h], vn_ref[0, h]], axis=1).astype(BF16)
    return kT, vT


def _pv(p, vT):
    return lax.dot_general(p, vT, (((1,), (1,)), ((), ())), preferred_element_type=F32)


def _sb_sample_kernel(q_ref, ck_ref, kn_ref, cv_ref, vn_ref, tri_ref, o_ref):
    n_heads, tq = q_ref.shape[1], q_ref.shape[2]
    p_len = ck_ref.shape[3]
    nk = p_len + kn_ref.shape[3]
    row = lax.broadcasted_iota(jnp.int32, (tq, nk), 0)
    col = lax.broadcasted_iota(jnp.int32, (tq, nk), 1)
    mask = col < row + p_len
    heads = []
    for h in range(n_heads):
        kT, vT = _sample_kv(ck_ref, kn_ref, cv_ref, vn_ref, h)
        q = (q_ref[0, h] * HEAD_DIM ** -0.5).astype(BF16)
        z = _dot(q, kT)
        ls_full = -(jnp.maximum(z, 0.0) + jnp.log1p(jnp.exp(-jnp.abs(z))))
        heads.append((z + ls_full, jnp.where(mask, ls_full, 0.0).astype(BF16), vT))
    later = _dot(jnp.concatenate([ls for _, ls, _ in heads], axis=0), tri_ref[...])
    for h, (lsig, _, vT) in enumerate(heads):
        wgt = jnp.where(mask, jnp.exp(lsig + later[h * tq:(h + 1) * tq, :]), 0.0)
        o_ref[0, :, h * HEAD_DIM:(h + 1) * HEAD_DIM] = _pv(wgt.astype(BF16), vT)


def _fox_sample_kernel(q_ref, ck_ref, kn_ref, cv_ref, vn_ref, lf_ref, o_ref):
    n_heads, tq = q_ref.shape[1], q_ref.shape[2]
    p_len = ck_ref.shape[3]
    nk = p_len + kn_ref.shape[3]
    row = lax.broadcasted_iota(jnp.int32, (tq, nk), 0)
    col = lax.broadcasted_iota(jnp.int32, (tq, nk), 1)
    mask = col <= row + p_len
    diag = col == row + p_len
    f_all = _lane_cumsum(lf_ref[0])
    kvs = [_sample_kv(ck_ref, kn_ref, cv_ref, vn_ref, h) for h in range(n_heads)]
    zs = [_dot((q_ref[0, h] * HEAD_DIM ** -0.5).astype(BF16), kvs[h][0]) for h in range(n_heads)]
    ps = []
    for h in range(n_heads):
        fk = f_all[h:h + 1, :]
        fq = jnp.sum(jnp.where(diag, fk, 0.0), axis=1, keepdims=True)
        z = jnp.where(mask, zs[h] + fq - fk, NEG_BIG)
        p = jnp.exp(z - jnp.max(z, axis=1, keepdims=True))
        ps.append((p.astype(BF16), jnp.sum(p, axis=1, keepdims=True)))
    for h, (p, l) in enumerate(ps):
        o_ref[0, :, h * HEAD_DIM:(h + 1) * HEAD_DIM] = _pv(p, kvs[h][1]) / l


def _attn_sample(kernel, q, ck, kn, cv, vn, extra, extra_spec):
    b, h, tq, hd = q.shape
    per_b = lambda a: pl.BlockSpec((1,) + a.shape[1:], lambda bi: (bi,) + (0,) * (a.ndim - 1))
    return pl.pallas_call(
        kernel,
        grid=(b,),
        in_specs=[per_b(a) for a in (q, ck, kn, cv, vn)] + [extra_spec],
        out_specs=pl.BlockSpec((1, tq, h * hd), lambda bi: (bi, 0, 0)),
        out_shape=jax.ShapeDtypeStruct((b, tq, h * hd), F32),
        compiler_params=_cparams(("parallel",)),
        name=kernel.__name__.strip("_"),
    )(q, ck, kn, cv, vn, extra)


def _route(lt, n_groups, epg):
    n = lt.shape[1]
    g = [lt[i:i + 1, :] for i in range(n_groups)]
    gmax = functools.reduce(jnp.maximum, g)
    g_w = 1.0 / functools.reduce(jnp.add, [jnp.exp(gi - gmax) for gi in g])
    is_g, taken = [], None
    for gi in g:
        hit = gi >= gmax
        if taken is not None:
            hit = jnp.logical_and(hit, jnp.logical_not(taken))
        taken = hit if taken is None else jnp.logical_or(taken, hit)
        is_g.append(hit)
    le = [lt[n_groups + i:n_groups + i + 1, :] for i in range(n_groups * epg)]
    e_sel = []
    for i in range(epg):
        v = le[(n_groups - 1) * epg + i]
        for gi in reversed(range(n_groups - 1)):
            v = jnp.where(is_g[gi], le[gi * epg + i], v)
        e_sel.append(v)
    emax = functools.reduce(jnp.maximum, e_sel)
    pe = [jnp.exp(v - emax) for v in e_sel]
    pden = functools.reduce(jnp.add, pe)
    prob = [p / pden for p in pe]

    def first_argmax(vals):
        vmax = functools.reduce(jnp.maximum, vals)
        hits, tk = [], None
        for v in vals:
            hit = v >= vmax
            if tk is not None:
                hit = jnp.logical_and(hit, jnp.logical_not(tk))
            tk = hit if tk is None else jnp.logical_or(tk, hit)
            hits.append(hit)
        return vmax, hits

    p1, t1 = first_argmax(prob)
    p2, t2 = first_argmax([jnp.where(t, -1.0, p) for t, p in zip(t1, prob)])
    tot = p1 + p2
    w1 = g_w * (p1 / tot)
    w2 = g_w * (p2 / tot)
    rows = lax.broadcasted_iota(jnp.int32, (LANES, n), 0)
    comb = jnp.zeros((LANES, n), F32)
    for gi in range(n_groups):
        for i in range(epg):
            val = jnp.where(is_g[gi], jnp.where(t1[i], w1, 0.0) + jnp.where(t2[i], w2, 0.0), 0.0)
            comb = jnp.where(rows == gi * epg + i, jnp.broadcast_to(val, (LANES, n)), comb)
    return comb


def _post_kernel(yaT_ref, ybT_ref, gates_ref, x_ref, mods_ref, wba_ref, wbb_ref, wout_ref,
                 gmoe_ref, wr2_ref, wrhi_ref, br_ref, x1_ref, h2_ref, comb_ref,
                 *, per_token, n_groups, epg):
    d = x_ref.shape[2]
    tdot = lambda aT, w: lax.dot_general(aT, w, (((0,), (0,)), ((), ())), preferred_element_type=F32)
    ua = tdot(yaT_ref[0], wba_ref[...])
    ub = tdot(ybT_ref[0], wbb_ref[...])
    gates = gates_ref[0].astype(F32)
    mix = gates[:, :d] * ua + gates[:, d:] * ub
    x1 = x_ref[0] + _mod(mods_ref, 2, per_token) * _dot(mix.astype(BF16), wout_ref[...])
    x1_ref[0] = x1
    h2 = _rms_mod(x1, gmoe_ref[...], _mod(mods_ref, 4, per_token), _mod(mods_ref, 3, per_token))
    h2_hi = h2.astype(BF16)
    h2_ref[0] = h2_hi
    h2_lo = (h2 - h2_hi.astype(F32)).astype(BF16)
    a = _dot(h2_hi, wr2_ref[...])
    logits = a[:, :LANES] + a[:, LANES:] + _dot(h2_lo, wrhi_ref[...]) + br_ref[...]
    comb_ref[0] = _route(logits.T, n_groups, epg).T


def _post(yaT, ybT, gates, x, mods, w_ba, w_bb, w_out, g_moe, wr2, wrhi, br, per_token, n_groups, epg):
    b, t, d = x.shape
    tm = min(POST_TILE, t)
    w = yaT.shape[1]
    const2 = lambda bi, ti: (0, 0)
    tok = lambda last: pl.BlockSpec((1, tm, last), lambda bi, ti: (bi, ti, 0))
    chan = pl.BlockSpec((1, w, tm), lambda bi, ti: (bi, 0, ti))
    if per_token:
        mods_spec = pl.BlockSpec(mods.shape, lambda bi, ti: (0, 0, 0))
    else:
        mods_spec = pl.BlockSpec((1, N_MOD, d), lambda bi, ti: (bi, 0, 0))
    in_specs = [chan, chan, tok(2 * d), tok(d), mods_spec,
                pl.BlockSpec(w_ba.shape, const2), pl.BlockSpec(w_bb.shape, const2),
                pl.BlockSpec(w_out.shape, const2), pl.BlockSpec((1, d), const2),
                pl.BlockSpec(wr2.shape, const2), pl.BlockSpec(wrhi.shape, const2),
                pl.BlockSpec((1, LANES), const2)]
    out_shape = (jax.ShapeDtypeStruct((b, t, d), F32),
                 jax.ShapeDtypeStruct((b, t, d), BF16),
                 jax.ShapeDtypeStruct((b, t, LANES), F32))
    return pl.pallas_call(
        functools.partial(_post_kernel, per_token=per_token, n_groups=n_groups, epg=epg),
        grid=(b, t // tm),
        in_specs=in_specs,
        out_specs=(tok(d), tok(d), tok(LANES)),
        out_shape=out_shape,
        compiler_params=_cparams(("parallel", "parallel")),
        name="post_sample" if per_token else "post_prompt",
    )(yaT, ybT, gates, x, mods, w_ba, w_bb, w_out, g_moe, wr2, wrhi, br)


def _moe_kernel(h2_ref, comb_ref, x1_ref, mods_ref, w13_ref, w2_ref, gfin_ref, y_ref, acc_ref,
                *, per_token, epg, hid):
    g = pl.program_id(1)

    @pl.when(g == 0)
    def _():
        acc_ref[...] = jnp.zeros_like(acc_ref)

    h2 = h2_ref[...]
    comb = comb_ref[...]
    lane = lax.broadcasted_iota(jnp.int32, comb.shape, 1)
    ups = [_dot(h2, w13_ref[0, i]) for i in range(epg)]
    total = None
    for c in range(0, epg, 2):
        acts = []
        for i in range(c, c + 2):
            a1 = ups[i][:, :hid]
            a3 = ups[i][:, hid:]
            cw = jnp.sum(jnp.where(lane == g * epg + i, comb, 0.0), axis=1, keepdims=True)
            acts.append(((a1 * jax.nn.sigmoid(a1)) * a3 * cw).astype(BF16))
        part = _dot(jnp.concatenate(acts, axis=1), w2_ref[0, c * hid:(c + 2) * hid, :])
        total = part if total is None else total + part
    acc_ref[...] += total

    @pl.when(g == pl.num_programs(1) - 1)
    def _():
        x2 = x1_ref[...] + _mod(mods_ref, 5, per_token) * acc_ref[...]
        ms = jnp.mean(x2 * x2, axis=-1, keepdims=True)
        y_ref[...] = (x2 * lax.rsqrt(ms + RMS_EPS)) * gfin_ref[...]


def _moe(h2, comb, x1, mods, w13, w2, g_final, per_token, tokens_per_batch, epg):
    n, d = x1.shape
    tm = min(MOE_TILE, n)
    n_groups = w13.shape[0]
    hid = w2.shape[1] // epg
    assert epg % 2 == 0
    tiles_per_batch = tokens_per_batch // tm if not per_token else 1
    tok = lambda last: pl.BlockSpec((tm, last), lambda i, g: (i, 0))
    if per_token:
        mods_spec = pl.BlockSpec(mods.shape, lambda i, g: (0, 0, 0))
    else:
        mods_spec = pl.BlockSpec((1, N_MOD, d), lambda i, g: (i // tiles_per_batch, 0, 0))
    return pl.pallas_call(
        functools.partial(_moe_kernel, per_token=per_token, epg=epg, hid=hid),
        grid=(n // tm, n_groups),
        in_specs=[tok(d), tok(LANES), tok(d), mods_spec,
                  pl.BlockSpec((1,) + w13.shape[1:], lambda i, g: (g, 0, 0, 0)),
                  pl.BlockSpec((1,) + w2.shape[1:], lambda i, g: (g, 0, 0)),
                  pl.BlockSpec((1, d), lambda i, g: (0, 0))],
        out_specs=tok(d),
        out_shape=jax.ShapeDtypeStruct((n, d), F32),
        scratch_shapes=[pltpu.VMEM((tm, d), F32)],
        compiler_params=_cparams(("parallel", "arbitrary")),
        name="moe_sample" if per_token else "moe_prompt",
    )(h2, comb, x1, mods, w13, w2, g_final)


def kernel(x_prompt, x_sample, cache_sb_k, cache_sb_v, cache_fox_k, cache_fox_v, cache_fox_logf,
           c_prompt, c_sample, w_ada, b_ada, g_mix, w_in, b_f, w_ba, w_bb, w_gate, b_gate, w_out,
           g_moe, w_rg, b_rg, w_re, b_re, w1, w3, w2, g_final):
    depth = w_ada.shape[0]
    assert depth == 1, "single-layer trunk"
    bp, t, d = x_prompt.shape
    bs, ts, _ = x_sample.shape
    n_heads = cache_sb_k.shape[2]
    p_len = cache_sb_k.shape[3]
    assert cache_fox_k.shape[2] == n_heads and n_heads * HEAD_DIM * 6 + n_heads == w_in.shape[2]
    assert n_heads == 8 and t % Q_TILE == 0 and Q_TILE % PRE_TILE == 0
    n_groups = w_rg.shape[2]
    n_exp = w_re.shape[2]
    epg = n_exp // n_groups
    assert n_groups + n_exp <= LANES
    w = n_heads * HEAD_DIM
    ns = bs * ts

    w_in_p = jnp.pad(w_in[0], ((0, 0), (0, LANES - n_heads))).astype(BF16)
    b_f_p = jnp.pad(b_f[0], (0, LANES - n_heads)).reshape(1, LANES)
    w_gate_b = w_gate[0].astype(BF16)
    b_gate_r = b_gate[0].reshape(1, -1)
    g_mix_r = g_mix[0].reshape(1, d)
    g_moe_r = g_moe[0].reshape(1, d)
    g_fin_r = g_final.reshape(1, d)
    w_ba_b, w_bb_b, w_out_b = w_ba[0].astype(BF16), w_bb[0].astype(BF16), w_out[0].astype(BF16)
    w_r = jnp.pad(jnp.concatenate([w_rg[0], w_re[0]], axis=1), ((0, 0), (0, LANES - n_groups - n_exp)))
    w_r_hi = w_r.astype(BF16)
    w_r_lo = (w_r - w_r_hi.astype(F32)).astype(BF16)
    wr2 = jnp.concatenate([w_r_hi, w_r_lo], axis=1)
    b_r = jnp.pad(jnp.concatenate([b_rg[0], b_re[0]]), (0, LANES - n_groups - n_exp)).reshape(1, LANES)
    hid = w1.shape[3]
    w13 = jnp.concatenate([w1[0], w3[0]], axis=2).astype(BF16).reshape(n_groups, epg, d, 2 * hid)
    w2_b = w2[0].astype(BF16).reshape(n_groups, epg * hid, d)

    ada = _ada(jnp.concatenate([c_prompt, c_sample], axis=0), w_ada[0], b_ada[0])
    mods_p = ada[:bp].reshape(bp, N_MOD, d)
    mods_s = jnp.repeat(ada[bp:].reshape(bs, N_MOD, d).transpose(1, 0, 2), ts, axis=1)

    hsel = jnp.repeat(jnp.eye(n_heads, dtype=BF16), HEAD_DIM, axis=1)
    (qsT, ks, vsT, qfT, kf, vfT, ksl, vsl, kfl, vfl, logf_p, kn2, gates_p) = _pre_prompt(
        x_prompt, mods_p, g_mix_r, w_in_p, b_f_p, w_gate_b, b_gate_r, hsel, n_heads)
    tk = STRIP
    ids = jnp.arange(tk)
    ntri_p = -(ids[None, :] > ids[:, None]).astype(BF16)
    yaT = _attn_prompt(_sb_prompt_kernel, qsT, ks, vsT, (ntri_p,)).reshape(bp, w, t)
    ybT = _attn_prompt(_fox_prompt_kernel, qfT, kf, vfT, (kn2[:, :, None, :],)).reshape(bp, w, t)
    x1_p, h2_p, comb_p = _post(yaT, ybT, gates_p, x_prompt, mods_p, w_ba_b, w_bb_b, w_out_b,
                               g_moe_r, wr2, w_r_hi, b_r, False, n_groups, epg)
    y_prompt = _moe(h2_p.reshape(bp * t, d), comb_p.reshape(bp * t, LANES), x1_p.reshape(bp * t, d),
                    mods_p, w13, w2_b, g_fin_r, False, t, epg).reshape(bp, t, d)

    proj_s, logf_s, gates_s = _pre_sample(x_sample.reshape(ns, d), mods_s, g_mix_r, w_in_p, b_f_p,
                                          w_gate_b, b_gate_r, n_heads)
    heads = lambda i: proj_s[:, i * w:(i + 1) * w].reshape(bs, ts, n_heads, HEAD_DIM).transpose(0, 2, 1, 3)
    qa_s, ka_s, va_s, qb_s, kb_s, vb_s = [heads(i) for i in range(6)]
    lf_s = logf_s[:, :n_heads].reshape(bs, ts, n_heads).transpose(0, 2, 1)
    pad_k = lambda a: jnp.pad(jnp.swapaxes(a, 2, 3), ((0, 0), (0, 0), (0, 0), (0, LANES - ts)))
    chan = lambda cache: jnp.swapaxes(cache[0], 2, 3)
    nk = p_len + LANES
    ids = jnp.arange(nk)
    tri_s = (ids[:, None] > ids[None, :]).astype(BF16)
    lf_all = jnp.concatenate([cache_fox_logf[0], jnp.pad(lf_s, ((0, 0), (0, 0), (0, LANES - ts)))], axis=2)
    ya_s = _attn_sample(_sb_sample_kernel, qa_s, chan(cache_sb_k), pad_k(ka_s), chan(cache_sb_v), pad_k(va_s),
                        tri_s, pl.BlockSpec(tri_s.shape, lambda bi: (0, 0)))
    yb_s = _attn_sample(_fox_sample_kernel, qb_s, chan(cache_fox_k), pad_k(kb_s), chan(cache_fox_v), pad_k(vb_s),
                        lf_all, pl.BlockSpec((1, n_heads, nk), lambda bi: (bi, 0, 0)))
    to_chan = lambda y: y.reshape(ns, w).T.astype(BF16)[None]
    x1_s, h2_s, comb_s = _post(to_chan(ya_s), to_chan(yb_s), gates_s[None], x_sample.reshape(1, ns, d),
                               mods_s, w_ba_b, w_bb_b, w_out_b, g_moe_r, wr2, w_r_hi, b_r,
                               True, n_groups, epg)
    y_sample = _moe(h2_s[0], comb_s[0], x1_s[0], mods_s, w13, w2_b, g_fin_r, True, ns, epg).reshape(bs, ts, d)

    lead = lambda a: a[None]
    tok_major = lambda a: jnp.swapaxes(a, 2, 3)[None]
    return (y_prompt, y_sample,
            tok_major(ksl), tok_major(vsl), tok_major(kfl), tok_major(vfl), lead(logf_p),
            lead(ka_s), lead(va_s), lead(kb_s), lead(vb_s), lead(lf_s))
```

```python
import functools

import jax
import jax.numpy as jnp
from jax import lax
from jax.experimental import pallas as pl
from jax.experimental.pallas import tpu as pltpu

F32 = jnp.float32
BF16 = jnp.bfloat16

HEAD_DIM = 64
RMS_EPS = 1e-6
N_MOD = 6
LANES = 128
NEG_BIG = -1e30

PRE_TILE = 512
Q_TILE = 1024
STRIP = 256
N_STRIPS = Q_TILE // STRIP
LOG2E = 1.4426950408889634
DEAD_LOG2 = -128.0
NORM_SLACK = 1.02
POST_TILE = 512
MOE_TILE = 1024
VMEM_LIMIT = 56 * 1024 * 1024


def _cparams(sem):
    return pltpu.CompilerParams(dimension_semantics=sem, vmem_limit_bytes=VMEM_LIMIT)


def _log_sigmoid(x):
    return jnp.minimum(x, 0.0) - jnp.log1p(jnp.exp(-jnp.abs(x)))


def _rms_mod(x, g, scale, shift):
    ms = jnp.mean(x * x, axis=-1, keepdims=True)
    y = x * lax.rsqrt(ms + RMS_EPS)
    return (y * g) * (1.0 + scale) + shift


def _mod(mods_ref, i, per_token):
    return mods_ref[i] if per_token else mods_ref[0, i:i + 1, :]


def _dot(a, b):
    return jnp.dot(a, b, preferred_element_type=F32)


def _split3(f):
    hi = f.astype(BF16).astype(F32)
    r = f - hi
    mid = r.astype(BF16).astype(F32)
    lo = (r - mid).astype(BF16).astype(F32)
    return hi, mid, lo


def _lane_cumsum(x):
    n = x.shape[1]
    lane = lax.broadcasted_iota(jnp.int32, x.shape, 1)
    d = 1
    while d < n:
        x = x + jnp.where(lane >= d, pltpu.roll(x, d, axis=1), 0.0)
        d *= 2
    return x


def _ada_kernel(c_ref, w_ref, b_ref, o_ref):
    c = c_ref[...]
    s = c * jax.nn.sigmoid(c)
    o_ref[...] = jnp.dot(s, w_ref[...], preferred_element_type=F32,
                         precision=lax.Precision.HIGHEST) + b_ref[...]


def _ada(c_all, w_ada, b_ada):
    n, d = c_all.shape
    nout = w_ada.shape[1]
    tn = 1024
    return pl.pallas_call(
        _ada_kernel,
        grid=(nout // tn,),
        in_specs=[pl.BlockSpec((n, d), lambda j: (0, 0)),
                  pl.BlockSpec((d, tn), lambda j: (0, j)),
                  pl.BlockSpec((1, tn), lambda j: (0, j))],
        out_specs=pl.BlockSpec((n, tn), lambda j: (0, j)),
        out_shape=jax.ShapeDtypeStruct((n, nout), F32),
        compiler_params=_cparams(("arbitrary",)),
        name="ada",
    )(c_all, w_ada, b_ada.reshape(1, nout))


def _pre_core(x, mods_ref, gmix_ref, win_ref, wg_ref, bg_ref, per_token):
    h = _rms_mod(x, gmix_ref[...], _mod(mods_ref, 1, per_token), _mod(mods_ref, 0, per_token))
    hb = h.astype(BF16)
    proj = _dot(hb, win_ref[...])
    gates = jax.nn.sigmoid(_dot(hb, wg_ref[...]) + bg_ref[...])
    return proj, gates.astype(BF16)


def _pre_prompt_kernel(x_ref, mods_ref, gmix_ref, win_ref, bf_ref, wg_ref, bg_ref, hsel_ref,
                       qsT_ref, ks_ref, vsT_ref, qfT_ref, kf_ref, vfT_ref,
                       ksl_ref, vsl_ref, kfl_ref, vfl_ref, logf_ref, kn2_ref, gates_ref,
                       carry_ref, *, n_heads):
    tm = x_ref.shape[1]
    w = n_heads * HEAD_DIM

    @pl.when(pl.program_id(1) == 0)
    def _():
        carry_ref[...] = jnp.zeros_like(carry_ref)

    proj, gates = _pre_core(x_ref[0], mods_ref, gmix_ref, win_ref, wg_ref, bg_ref, False)
    gates_ref[0] = gates
    qa, ka, va, qb, kb, vb = [proj[:, i * w:(i + 1) * w] for i in range(6)]
    fg = proj[:, 6 * w:6 * w + LANES]
    scale = HEAD_DIM ** -0.5 * LOG2E
    qaT = (qa * scale).T
    kaT = ka.T
    vaT = va.T
    qbT = (qb * scale).T
    kbT = kb.T
    vbT = vb.T
    kn2_ref[0] = _dot(hsel_ref[...], (kbT * kbT).astype(BF16))

    logfT = _log_sigmoid(fg + bf_ref[...]).T[0:n_heads, :]
    logf_ref[0] = logfT
    f = _lane_cumsum(logfT) + carry_ref[:, 0:1]
    carry_ref[...] = jnp.broadcast_to(f[:, tm - 1:tm], carry_ref.shape)
    f_hi, f_mid, f_lo = _split3(f * LOG2E)

    row64 = lax.broadcasted_iota(jnp.int32, (HEAD_DIM, tm), 0)
    row8 = lax.broadcasted_iota(jnp.int32, (8, tm), 0)
    zeros64 = jnp.zeros((HEAD_DIM, tm), BF16)
    ke_parts = []
    for h in range(n_heads):
        def bc(a, n):
            return jnp.broadcast_to(a[h:h + 1, :], (n, tm))
        qe = jnp.where(row64 < 3, 1.0,
                       jnp.where(row64 == 3, bc(f_hi, HEAD_DIM),
                                 jnp.where(row64 == 4, bc(f_mid, HEAD_DIM),
                                           jnp.where(row64 == 5, bc(f_lo, HEAD_DIM), 0.0))))
        sl = slice(h * HEAD_DIM, (h + 1) * HEAD_DIM)
        qfT_ref[0, h, 0:HEAD_DIM, :] = qbT[sl, :].astype(BF16)
        qfT_ref[0, h, HEAD_DIM:2 * HEAD_DIM, :] = qe.astype(BF16)
        qsT_ref[0, h, 0:HEAD_DIM, :] = qaT[sl, :].astype(BF16)
        qsT_ref[0, h, HEAD_DIM:2 * HEAD_DIM, :] = zeros64
        for j in range(tm // STRIP):
            vsT_ref[0, h, j] = vaT[sl, j * STRIP:(j + 1) * STRIP].astype(BF16)
            vfT_ref[0, h, j] = vbT[sl, j * STRIP:(j + 1) * STRIP].astype(BF16)
        ksl_ref[0, h] = kaT[sl, :]
        vsl_ref[0, h] = vaT[sl, :]
        kfl_ref[0, h] = kbT[sl, :]
        vfl_ref[0, h] = vbT[sl, :]
        ke_parts.append(
            jnp.where(row8 == 0, -bc(f_hi, 8),
                      jnp.where(row8 == 1, -bc(f_mid, 8),
                                jnp.where(row8 == 2, -bc(f_lo, 8),
                                          jnp.where(row8 < 6, 1.0, 0.0)))))
    ke_parts.append(jnp.zeros((LANES - 8 * n_heads, tm), F32))
    ke = jnp.concatenate(ke_parts, axis=0).T

    lane = lax.broadcasted_iota(jnp.int32, (tm, LANES), 1)
    for h in range(n_heads):
        base = (h // 2) * LANES
        ka_slab = ka[:, base:base + LANES]
        kb_slab = kb[:, base:base + LANES]
        if h % 2:
            ka_slab = pltpu.roll(ka_slab, HEAD_DIM, axis=1)
            kb_slab = pltpu.roll(kb_slab, HEAD_DIM, axis=1)
        ext = pltpu.roll(ke, HEAD_DIM - 8 * h, axis=1)
        ks_ref[0, h] = jnp.where(lane < HEAD_DIM, ka_slab, 0.0).astype(BF16)
        kf_ref[0, h] = jnp.where(lane < HEAD_DIM, kb_slab,
                                 jnp.where(lane < HEAD_DIM + 8, ext, 0.0)).astype(BF16)


def _pre_prompt(x, mods, g_mix, w_in_p, b_f_p, w_gate, b_gate, hsel, n_heads):
    b, t, d = x.shape
    tm = PRE_TILE
    nt = t // tm
    h, hd = n_heads, HEAD_DIM
    const2 = lambda bi, ti: (0, 0)
    head_t = lambda bi, ti: (bi, 0, 0, ti)
    head_s = lambda bi, ti: (bi, 0, ti, 0)
    blk_t = lambda bi, ti: (bi, 0, ti, 0, 0)
    out_shape = (
        jax.ShapeDtypeStruct((b, h, 2 * hd, t), BF16),
        jax.ShapeDtypeStruct((b, h, t, 2 * hd), BF16),
        jax.ShapeDtypeStruct((b, h, t // STRIP, hd, STRIP), BF16),
        jax.ShapeDtypeStruct((b, h, 2 * hd, t), BF16),
        jax.ShapeDtypeStruct((b, h, t, 2 * hd), BF16),
        jax.ShapeDtypeStruct((b, h, t // STRIP, hd, STRIP), BF16),
        jax.ShapeDtypeStruct((b, h, hd, t), F32),
        jax.ShapeDtypeStruct((b, h, hd, t), F32),
        jax.ShapeDtypeStruct((b, h, hd, t), F32),
        jax.ShapeDtypeStruct((b, h, hd, t), F32),
        jax.ShapeDtypeStruct((b, h, t), F32),
        jax.ShapeDtypeStruct((b, h, t), F32),
        jax.ShapeDtypeStruct((b, t, w_gate.shape[1]), BF16),
    )
    out_specs = (
        pl.BlockSpec((1, h, 2 * hd, tm), head_t),
        pl.BlockSpec((1, h, tm, 2 * hd), head_s),
        pl.BlockSpec((1, h, tm // STRIP, hd, STRIP), blk_t),
        pl.BlockSpec((1, h, 2 * hd, tm), head_t),
        pl.BlockSpec((1, h, tm, 2 * hd), head_s),
        pl.BlockSpec((1, h, tm // STRIP, hd, STRIP), blk_t),
        pl.BlockSpec((1, h, hd, tm), head_t),
        pl.BlockSpec((1, h, hd, tm), head_t),
        pl.BlockSpec((1, h, hd, tm), head_t),
        pl.BlockSpec((1, h, hd, tm), head_t),
        pl.BlockSpec((1, h, tm), lambda bi, ti: (bi, 0, ti)),
        pl.BlockSpec((1, h, tm), lambda bi, ti: (bi, 0, ti)),
        pl.BlockSpec((1, tm, w_gate.shape[1]), lambda bi, ti: (bi, ti, 0)),
    )
    in_specs = [
        pl.BlockSpec((1, tm, d), lambda bi, ti: (bi, ti, 0)),
        pl.BlockSpec((1, N_MOD, d), lambda bi, ti: (bi, 0, 0)),
        pl.BlockSpec((1, d), const2),
        pl.BlockSpec(w_in_p.shape, const2, pipeline_mode=pl.Buffered(1)),
        pl.BlockSpec((1, LANES), const2),
        pl.BlockSpec(w_gate.shape, const2, pipeline_mode=pl.Buffered(1)),
        pl.BlockSpec((1, w_gate.shape[1]), const2),
        pl.BlockSpec(hsel.shape, const2),
    ]
    return pl.pallas_call(
        functools.partial(_pre_prompt_kernel, n_heads=n_heads),
        grid=(b, nt),
        in_specs=in_specs,
        out_specs=out_specs,
        out_shape=out_shape,
        scratch_shapes=[pltpu.VMEM((h, LANES), F32)],
        compiler_params=_cparams(("arbitrary", "arbitrary")),
        name="pre_prompt",
    )(x, mods, g_mix, w_in_p, b_f_p, w_gate, b_gate, hsel)


def _pre_sample_kernel(x_ref, mods_ref, gmix_ref, win_ref, bf_ref, wg_ref, bg_ref,
                       proj_ref, logf_ref, gates_ref, *, n_heads):
    w = n_heads * HEAD_DIM
    proj, gates = _pre_core(x_ref[...], mods_ref, gmix_ref, win_ref, wg_ref, bg_ref, True)
    proj_ref[...] = proj
    gates_ref[...] = gates
    logf_ref[...] = _log_sigmoid(proj[:, 6 * w:6 * w + LANES] + bf_ref[...])


def _pre_sample(x, mods_tok, g_mix, w_in_p, b_f_p, w_gate, b_gate, n_heads):
    n = x.shape[0]
    full = lambda a: pl.BlockSpec(a.shape, lambda i: (0,) * a.ndim)
    args = (x, mods_tok, g_mix, w_in_p, b_f_p, w_gate, b_gate)
    out_shape = (jax.ShapeDtypeStruct((n, w_in_p.shape[1]), F32),
                 jax.ShapeDtypeStruct((n, LANES), F32),
                 jax.ShapeDtypeStruct((n, w_gate.shape[1]), BF16))
    return pl.pallas_call(
        functools.partial(_pre_sample_kernel, n_heads=n_heads),
        grid=(1,),
        in_specs=[full(a) for a in args],
        out_specs=tuple(pl.BlockSpec(s.shape, lambda i: (0, 0)) for s in out_shape),
        out_shape=out_shape,
        compiler_params=_cparams(("arbitrary",)),
        name="pre_sample",
    )(*args)


SB_DEPTHS = 3
FOX_DEPTHS = 6


def _attn_parts(qT_ref, k_ref, vT_ref, static_depths):
    assert qT_ref.shape[3] == N_STRIPS * STRIP and vT_ref.shape[4] == STRIP
    nk = vT_ref.shape[2]
    base = pl.program_id(2) * N_STRIPS
    q_strip = lambda s: qT_ref[0, 0, :, s * STRIP:(s + 1) * STRIP]
    k_block = lambda j: k_ref[0, 0, pl.ds(pl.multiple_of(j * STRIP, STRIP), STRIP), :]
    v_block = lambda j: vT_ref[0, 0, j]
    block = lambda s, u: base + s - u
    clip = lambda j: jnp.clip(j, 0, nk - 1)

    def qk(u, z_scr, strips=range(N_STRIPS)):
        for s in strips:
            z_scr[s] = _dot(k_block(clip(block(s, u))), q_strip(s))

    row = lax.broadcasted_iota(jnp.int32, (STRIP, STRIP), 0)
    col = lax.broadcasted_iota(jnp.int32, (STRIP, STRIP), 1)
    n_pairs = (base + N_STRIPS - static_depths + 1) // 2
    return q_strip, k_block, v_block, block, clip, qk, row, col, n_pairs


def _sb_prompt_kernel(qT_ref, k_ref, vT_ref, ntri_ref, o_ref, z_a, z_b, w_a, w_b):
    STATIC_DEPTHS = SB_DEPTHS
    q_strip, k_block, v_block, block, clip, qk, row, col, n_pairs = _attn_parts(
        qT_ref, k_ref, vT_ref, STATIC_DEPTHS)
    diag = row < col
    sign = jnp.uint32(0x80000000)

    def softplus2(z):
        neg_abs = lax.bitcast_convert_type(lax.bitcast_convert_type(z, jnp.uint32) | sign, F32)
        return jnp.maximum(z, 0.0) + jnp.log2(1.0 + jnp.exp2(neg_abs))

    def suffix(sp):
        sp_b = sp.astype(BF16)
        later = _dot(ntri_ref[...], sp_b)
        return later, later[0:1, :] - sp_b[0:1, :].astype(F32)

    def masked(x, s, u):
        if isinstance(u, int) and u == 0:
            return jnp.where(diag, x, 0.0)
        if isinstance(u, int) and u <= s:
            return x
        return jnp.where(block(s, u) >= 0, x, 0.0)

    w_a[...] = jnp.zeros(w_a.shape, BF16)
    w_b[...] = jnp.zeros(w_b.shape, BF16)
    subs = [(s, u) for u in range(STATIC_DEPTHS) for s in range(N_STRIPS)]
    zs = [_dot(k_block(clip(block(s, u))), q_strip(s)) for s, u in subs]
    qk(STATIC_DEPTHS, z_a)
    c = [jnp.zeros((1, STRIP), F32)] * N_STRIPS
    acc = [jnp.zeros((HEAD_DIM, STRIP), F32)] * N_STRIPS
    mids = []
    for (s, u), z in zip(subs, zs):
        sp_full = softplus2(z)
        later, total = suffix(masked(sp_full, s, u))
        mids.append((z - sp_full, later, c[s]))
        c[s] = c[s] + total
    for (s, u), (lsig, later, c_before) in zip(subs, mids):
        wgt = masked(jnp.exp2(lsig + later), s, u)
        acc[s] = acc[s] + _dot(v_block(clip(block(s, u))), wgt.astype(BF16)) * jnp.exp2(c_before)

    def pv(u, w_scr, carry, which, strips=range(N_STRIPS)):
        return tuple((cr[0], cr[1] + _dot(v_block(clip(block(s, u))), w_scr[s]) * cr[2 + which]) + cr[2:]
                     if s in strips else cr for s, cr in enumerate(carry))

    def softplus_phase(z_scr, carry, which, u):
        carry = list(carry)
        mids = []
        for s in range(N_STRIPS):
            c, acc, sa, sb = carry[s]
            z = z_scr[s]
            sp_full = softplus2(z)
            later, total = suffix(masked(sp_full, s, u))
            mids.append((z - sp_full, later))
            scale = jnp.exp2(c)
            carry[s] = (c + total, acc, sa, scale) if which else (c + total, acc, scale, sb)
        return tuple(carry), mids

    def weight_phase(w_scr, mids, u):
        for s, (lsig, later) in enumerate(mids):
            w_scr[s] = masked(jnp.exp2(lsig + later), s, u).astype(BF16)

    def pair(u, carry):
        carry = pv(u - 2, w_a, carry, 0)
        for s in range(N_STRIPS):
            carry = pv(u - 1, w_b, carry, 1, (s,))
            qk(u + 1, z_b, (s,))
        carry, mids_a = softplus_phase(z_a, carry, 0, u)
        qk(u + 2, z_a)
        carry, mids_b = softplus_phase(z_b, carry, 1, u + 1)
        weight_phase(w_a, mids_a, u)
        weight_phase(w_b, mids_b, u + 1)
        return carry

    def alive(carry, u_next):
        left = [jnp.where(block(s, u_next) >= 0, cr[0], NEG_BIG) for s, cr in enumerate(carry)]
        return jnp.max(functools.reduce(jnp.maximum, left)) > DEAD_LOG2

    def body(state):
        i, _, cr = state
        u = STATIC_DEPTHS + 2 * i
        cr = pair(u, cr)
        return i + 1, alive(cr, u + 2), cr

    one = jnp.ones((1, STRIP), F32)
    carry = tuple((c[s], acc[s], one, one) for s in range(N_STRIPS))
    i_end, _, carry = lax.while_loop(lambda st: jnp.logical_and(st[0] < n_pairs, st[1]), body,
                                     (jnp.int32(0), alive(carry, STATIC_DEPTHS), carry))
    last = STATIC_DEPTHS + 2 * i_end - 1
    carry = pv(last - 1, w_a, carry, 0)
    carry = pv(last, w_b, carry, 1)
    for s in range(N_STRIPS):
        o_ref[0, 0, :, s * STRIP:(s + 1) * STRIP] = carry[s][1].astype(BF16)


def _fox_prompt_kernel(qT_ref, k_ref, vT_ref, kn2_ref, o_ref, z_a, z_b, p_a, p_b):
    STATIC_DEPTHS = FOX_DEPTHS
    q_strip, k_block, v_block, block, clip, qk, row, col, n_pairs = _attn_parts(
        qT_ref, k_ref, vT_ref, STATIC_DEPTHS)
    diag = row <= col

    def masked(sc, s, u):
        if isinstance(u, int) and u == 0:
            return jnp.where(diag, sc, NEG_BIG)
        if isinstance(u, int) and u <= s:
            return sc
        return jnp.where(block(s, u) >= 0, sc, NEG_BIG)

    p_a[...] = jnp.zeros(p_a.shape, BF16)
    p_b[...] = jnp.zeros(p_b.shape, BF16)
    subs = [(s, u) for u in range(STATIC_DEPTHS) for s in range(N_STRIPS)]
    zs = [_dot(k_block(clip(block(s, u))), q_strip(s)) for s, u in subs]
    qk(STATIC_DEPTHS, z_a)
    m = [jnp.full((1, STRIP), NEG_BIG, F32)] * N_STRIPS
    l = [jnp.zeros((1, STRIP), F32)] * N_STRIPS
    acc = [jnp.zeros((HEAD_DIM, STRIP), F32)] * N_STRIPS
    for (s, u), sc in zip(subs, zs):
        sc = masked(sc, s, u)
        m_new = jnp.maximum(m[s], jnp.max(sc, axis=0, keepdims=True))
        alpha = jnp.exp2(m[s] - m_new)
        p = jnp.exp2(sc - m_new)
        l[s] = alpha * l[s] + jnp.sum(p, axis=0, keepdims=True)
        acc[s] = alpha * acc[s] + _dot(v_block(clip(block(s, u))), p.astype(BF16))
        m[s] = m_new

    def pv(u, p_scr, carry, which, strips=range(N_STRIPS)):
        return tuple((cr[0], cr[1], cr[3 + which] * cr[2] + _dot(v_block(clip(block(s, u))), p_scr[s])) + cr[3:]
                     if s in strips else cr for s, cr in enumerate(carry))

    def valu(z_scr, p_scr, carry, which, u):
        out = []
        for s in range(N_STRIPS):
            m, l, acc, aa, ab = carry[s]
            m_new = jnp.maximum(m, jnp.max(masked(z_scr[s], s, u), axis=0, keepdims=True))
            alpha = jnp.exp2(m - m_new)
            p = jnp.exp2(masked(z_scr[s], s, u) - m_new)
            p_scr[s] = p.astype(BF16)
            l = alpha * l + jnp.sum(p, axis=0, keepdims=True)
            out.append((m_new, l, acc, aa, alpha) if which else (m_new, l, acc, alpha, ab))
        return tuple(out)

    def pair(u, carry):
        carry = pv(u - 2, p_a, carry, 0)
        carry = pv(u - 1, p_b, carry, 1)
        qk(u + 1, z_b)
        carry = valu(z_a, p_a, carry, 0, u)
        qk(u + 2, z_a)
        return valu(z_b, p_b, carry, 1, u + 1)

    q_all = qT_ref[0, 0]
    q_f = q_all[0:HEAD_DIM, :].astype(F32)
    q_norm = jnp.sqrt(jnp.sum(q_f * q_f, axis=0, keepdims=True))
    f_rows = q_all[HEAD_DIM:HEAD_DIM + 16, :].astype(F32)
    f_q = f_rows[3:4, :] + f_rows[4:5, :] + f_rows[5:6, :]
    k_norm = jnp.sqrt(jnp.max(kn2_ref[0, 0], axis=1, keepdims=True))
    bound_q = NORM_SLACK * q_norm * k_norm + f_q

    def alive(carry, u_next):
        left = []
        for s in range(N_STRIPS):
            j = block(s, u_next)
            rows = k_ref[0, 0, pl.ds(pl.multiple_of(clip(j) * STRIP + STRIP - 16, 16), 16), :].astype(F32)
            f_end = -(rows[15:16, HEAD_DIM:HEAD_DIM + 1] + rows[15:16, HEAD_DIM + 1:HEAD_DIM + 2]
                      + rows[15:16, HEAD_DIM + 2:HEAD_DIM + 3])
            gap = bound_q[:, s * STRIP:(s + 1) * STRIP] - f_end - carry[s][0]
            left.append(jnp.where(j >= 0, gap, NEG_BIG))
        return jnp.max(functools.reduce(jnp.maximum, left)) > DEAD_LOG2

    def body(state):
        i, _, cr = state
        u = STATIC_DEPTHS + 2 * i
        cr = pair(u, cr)
        return i + 1, alive(cr, u + 2), cr

    one = jnp.ones((1, STRIP), F32)
    carry = tuple((m[s], l[s], acc[s], one, one) for s in range(N_STRIPS))
    i_end, _, carry = lax.while_loop(lambda st: jnp.logical_and(st[0] < n_pairs, st[1]), body,
                                     (jnp.int32(0), alive(carry, STATIC_DEPTHS), carry))
    last = STATIC_DEPTHS + 2 * i_end - 1
    carry = pv(last - 1, p_a, carry, 0)
    carry = pv(last, p_b, carry, 1)
    for s in range(N_STRIPS):
        _, l, acc, _, _ = carry[s]
        o_ref[0, 0, :, s * STRIP:(s + 1) * STRIP] = (acc / l).astype(BF16)


def _attn_prompt(kernel, qT, k, vT, extra=()):
    b, h, kd, t = qT.shape
    nk, tk = vT.shape[2], vT.shape[4]
    tq = Q_TILE
    in_specs = [
        pl.BlockSpec((1, 1, kd, tq), lambda bi, hi, qi: (bi, hi, 0, qi)),
        pl.BlockSpec((1, 1, t, kd), lambda bi, hi, qi: (bi, hi, 0, 0)),
        pl.BlockSpec((1, 1, nk, HEAD_DIM, tk), lambda bi, hi, qi: (bi, hi, 0, 0, 0)),
    ] + [pl.BlockSpec(a.shape, lambda bi, hi, qi: (0, 0)) if a.ndim == 2 else
         pl.BlockSpec((1, 1) + a.shape[2:], lambda bi, hi, qi: (bi, hi, 0, 0)) for a in extra]
    scores = pltpu.VMEM((N_STRIPS, STRIP, STRIP), F32)
    probs = pltpu.VMEM((N_STRIPS, STRIP, STRIP), BF16)
    return pl.pallas_call(
        kernel,
        grid=(b, h, t // tq),
        in_specs=in_specs,
        out_specs=pl.BlockSpec((1, 1, HEAD_DIM, tq), lambda bi, hi, qi: (bi, hi, 0, qi)),
        out_shape=jax.ShapeDtypeStruct((b, h, HEAD_DIM, t), BF16),
        scratch_shapes=[scores, scores, probs, probs],
        compiler_params=_cparams(("parallel", "parallel", "arbitrary")),
        name=kernel.__name__.strip("_"),
    )(qT, k, vT, *extra)


def _sample_kv(ck_ref, kn_ref, cv_ref, vn_ref, h):
    kT = jnp.concatenate([ck_ref[0, h], kn_ref[0, h]], axis=1).astype(BF16)
    vT = jnp.concatenate([cv_ref[0, h], vn_ref[0, h]], axis=1).astype(BF16)
    return kT, vT


def _pv(p, vT):
    return lax.dot_general(p, vT, (((1,), (1,)), ((), ())), preferred_element_type=F32)


def _sb_sample_kernel(q_ref, ck_ref, kn_ref, cv_ref, vn_ref, tri_ref, o_ref):
    n_heads, tq = q_ref.shape[1], q_ref.shape[2]
    p_len = ck_ref.shape[3]
    nk = p_len + kn_ref.shape[3]
    row = lax.broadcasted_iota(jnp.int32, (tq, nk), 0)
    col = lax.broadcasted_iota(jnp.int32, (tq, nk), 1)
    mask = col < row + p_len
    heads = []
    for h in range(n_heads):
        kT, vT = _sample_kv(ck_ref, kn_ref, cv_ref, vn_ref, h)
        q = (q_ref[0, h] * HEAD_DIM ** -0.5).astype(BF16)
        z = _dot(q, kT)
        ls_full = -(jnp.maximum(z, 0.0) + jnp.log1p(jnp.exp(-jnp.abs(z))))
        heads.append((z + ls_full, jnp.where(mask, ls_full, 0.0).astype(BF16), vT))
    later = _dot(jnp.concatenate([ls for _, ls, _ in heads], axis=0), tri_ref[...])
    for h, (lsig, _, vT) in enumerate(heads):
        wgt = jnp.where(mask, jnp.exp(lsig + later[h * tq:(h + 1) * tq, :]), 0.0)
        o_ref[0, :, h * HEAD_DIM:(h + 1) * HEAD_DIM] = _pv(wgt.astype(BF16), vT)


def _fox_sample_kernel(q_ref, ck_ref, kn_ref, cv_ref, vn_ref, lf_ref, o_ref):
    n_heads, tq = q_ref.shape[1], q_ref.shape[2]
    p_len = ck_ref.shape[3]
    nk = p_len + kn_ref.shape[3]
    row = lax.broadcasted_iota(jnp.int32, (tq, nk), 0)
    col = lax.broadcasted_iota(jnp.int32, (tq, nk), 1)
    mask = col <= row + p_len
    diag = col == row + p_len
    f_all = _lane_cumsum(lf_ref[0])
    kvs = [_sample_kv(ck_ref, kn_ref, cv_ref, vn_ref, h) for h in range(n_heads)]
    zs = [_dot((q_ref[0, h] * HEAD_DIM ** -0.5).astype(BF16), kvs[h][0]) for h in range(n_heads)]
    ps = []
    for h in range(n_heads):
        fk = f_all[h:h + 1, :]
        fq = jnp.sum(jnp.where(diag, fk, 0.0), axis=1, keepdims=True)
        z = jnp.where(mask, zs[h] + fq - fk, NEG_BIG)
        p = jnp.exp(z - jnp.max(z, axis=1, keepdims=True))
        ps.append((p.astype(BF16), jnp.sum(p, axis=1, keepdims=True)))
    for h, (p, l) in enumerate(ps):
        o_ref[0, :, h * HEAD_DIM:(h + 1) * HEAD_DIM] = _pv(p, kvs[h][1]) / l


def _attn_sample(kernel, q, ck, kn, cv, vn, extra, extra_spec):
    b, h, tq, hd = q.shape
    per_b = lambda a: pl.BlockSpec((1,) + a.shape[1:], lambda bi: (bi,) + (0,) * (a.ndim - 1))
    return pl.pallas_call(
        kernel,
        grid=(b,),
        in_specs=[per_b(a) for a in (q, ck, kn, cv, vn)] + [extra_spec],
        out_specs=pl.BlockSpec((1, tq, h * hd), lambda bi: (bi, 0, 0)),
        out_shape=jax.ShapeDtypeStruct((b, tq, h * hd), F32),
        compiler_params=_cparams(("parallel",)),
        name=kernel.__name__.strip("_"),
    )(q, ck, kn, cv, vn, extra)


def _route(lt, n_groups, epg):
    n = lt.shape[1]
    g = [lt[i:i + 1, :] for i in range(n_groups)]
    gmax = functools.reduce(jnp.maximum, g)
    g_w = 1.0 / functools.reduce(jnp.add, [jnp.exp(gi - gmax) for gi in g])
    is_g, taken = [], None
    for gi in g:
        hit = gi >= gmax
        if taken is not None:
            hit = jnp.logical_and(hit, jnp.logical_not(taken))
        taken = hit if taken is None else jnp.logical_or(taken, hit)
        is_g.append(hit)
    le = [lt[n_groups + i:n_groups + i + 1, :] for i in range(n_groups * epg)]
    e_sel = []
    for i in range(epg):
        v = le[(n_groups - 1) * epg + i]
        for gi in reversed(range(n_groups - 1)):
            v = jnp.where(is_g[gi], le[gi * epg + i], v)
        e_sel.append(v)
    emax = functools.reduce(jnp.maximum, e_sel)
    pe = [jnp.exp(v - emax) for v in e_sel]
    pden = functools.reduce(jnp.add, pe)
    prob = [p / pden for p in pe]

    def first_argmax(vals):
        vmax = functools.reduce(jnp.maximum, vals)
        hits, tk = [], None
        for v in vals:
            hit = v >= vmax
            if tk is not None:
                hit = jnp.logical_and(hit, jnp.logical_not(tk))
            tk = hit if tk is None else jnp.logical_or(tk, hit)
            hits.append(hit)
        return vmax, hits

    p1, t1 = first_argmax(prob)
    p2, t2 = first_argmax([jnp.where(t, -1.0, p) for t, p in zip(t1, prob)])
    tot = p1 + p2
    w1 = g_w * (p1 / tot)
    w2 = g_w * (p2 / tot)
    rows = lax.broadcasted_iota(jnp.int32, (LANES, n), 0)
    comb = jnp.zeros((LANES, n), F32)
    for gi in range(n_groups):
        for i in range(epg):
            val = jnp.where(is_g[gi], jnp.where(t1[i], w1, 0.0) + jnp.where(t2[i], w2, 0.0), 0.0)
            comb = jnp.where(rows == gi * epg + i, jnp.broadcast_to(val, (LANES, n)), comb)
    return comb


def _post_kernel(yaT_ref, ybT_ref, gates_ref, x_ref, mods_ref, wba_ref, wbb_ref, wout_ref,
                 gmoe_ref, wr2_ref, wrhi_ref, br_ref, x1_ref, h2_ref, comb_ref,
                 *, per_token, n_groups, epg):
    d = x_ref.shape[2]
    tdot = lambda aT, w: lax.dot_general(aT, w, (((0,), (0,)), ((), ())), preferred_element_type=F32)
    ua = tdot(yaT_ref[0], wba_ref[...])
    ub = tdot(ybT_ref[0], wbb_ref[...])
    gates = gates_ref[0].astype(F32)
    mix = gates[:, :d] * ua + gates[:, d:] * ub
    x1 = x_ref[0] + _mod(mods_ref, 2, per_token) * _dot(mix.astype(BF16), wout_ref[...])
    x1_ref[0] = x1
    h2 = _rms_mod(x1, gmoe_ref[...], _mod(mods_ref, 4, per_token), _mod(mods_ref, 3, per_token))
    h2_hi = h2.astype(BF16)
    h2_ref[0] = h2_hi
    h2_lo = (h2 - h2_hi.astype(F32)).astype(BF16)
    a = _dot(h2_hi, wr2_ref[...])
    logits = a[:, :LANES] + a[:, LANES:] + _dot(h2_lo, wrhi_ref[...]) + br_ref[...]
    comb_ref[0] = _route(logits.T, n_groups, epg).T


def _post(yaT, ybT, gates, x, mods, w_ba, w_bb, w_out, g_moe, wr2, wrhi, br, per_token, n_groups, epg):
    b, t, d = x.shape
    tm = min(POST_TILE, t)
    w = yaT.shape[1]
    const2 = lambda bi, ti: (0, 0)
    tok = lambda last: pl.BlockSpec((1, tm, last), lambda bi, ti: (bi, ti, 0))
    chan = pl.BlockSpec((1, w, tm), lambda bi, ti: (bi, 0, ti))
    if per_token:
        mods_spec = pl.BlockSpec(mods.shape, lambda bi, ti: (0, 0, 0))
    else:
        mods_spec = pl.BlockSpec((1, N_MOD, d), lambda bi, ti: (bi, 0, 0))
    in_specs = [chan, chan, tok(2 * d), tok(d), mods_spec,
                pl.BlockSpec(w_ba.shape, const2), pl.BlockSpec(w_bb.shape, const2),
                pl.BlockSpec(w_out.shape, const2), pl.BlockSpec((1, d), const2),
                pl.BlockSpec(wr2.shape, const2), pl.BlockSpec(wrhi.shape, const2),
                pl.BlockSpec((1, LANES), const2)]
    out_shape = (jax.ShapeDtypeStruct((b, t, d), F32),
                 jax.ShapeDtypeStruct((b, t, d), BF16),
                 jax.ShapeDtypeStruct((b, t, LANES), F32))
    return pl.pallas_call(
        functools.partial(_post_kernel, per_token=per_token, n_groups=n_groups, epg=epg),
        grid=(b, t // tm),
        in_specs=in_specs,
        out_specs=(tok(d), tok(d), tok(LANES)),
        out_shape=out_shape,
        compiler_params=_cparams(("parallel", "parallel")),
        name="post_sample" if per_token else "post_prompt",
    )(yaT, ybT, gates, x, mods, w_ba, w_bb, w_out, g_moe, wr2, wrhi, br)


def _moe_kernel(h2_ref, comb_ref, x1_ref, mods_ref, w13_ref, w2_ref, gfin_ref, y_ref, acc_ref,
                *, per_token, epg, hid):
    g = pl.program_id(1)

    @pl.when(g == 0)
    def _():
        acc_ref[...] = jnp.zeros_like(acc_ref)

    h2 = h2_ref[...]
    comb = comb_ref[...]
    lane = lax.broadcasted_iota(jnp.int32, comb.shape, 1)
    ups = [_dot(h2, w13_ref[0, i]) for i in range(epg)]
    total = None
    for c in range(0, epg, 2):
        acts = []
        for i in range(c, c + 2):
            a1 = ups[i][:, :hid]
            a3 = ups[i][:, hid:]
            cw = jnp.sum(jnp.where(lane == g * epg + i, comb, 0.0), axis=1, keepdims=True)
            acts.append(((a1 * jax.nn.sigmoid(a1)) * a3 * cw).astype(BF16))
        part = _dot(jnp.concatenate(acts, axis=1), w2_ref[0, c * hid:(c + 2) * hid, :])
        total = part if total is None else total + part
    acc_ref[...] += total

    @pl.when(g == pl.num_programs(1) - 1)
    def _():
        x2 = x1_ref[...] + _mod(mods_ref, 5, per_token) * acc_ref[...]
        ms = jnp.mean(x2 * x2, axis=-1, keepdims=True)
        y_ref[...] = (x2 * lax.rsqrt(ms + RMS_EPS)) * gfin_ref[...]


def _moe(h2, comb, x1, mods, w13, w2, g_final, per_token, tokens_per_batch, epg):
    n, d = x1.shape
    tm = min(MOE_TILE, n)
    n_groups = w13.shape[0]
    hid = w2.shape[1] // epg
    assert epg % 2 == 0
    tiles_per_batch = tokens_per_batch // tm if not per_token else 1
    tok = lambda last: pl.BlockSpec((tm, last), lambda i, g: (i, 0))
    if per_token:
        mods_spec = pl.BlockSpec(mods.shape, lambda i, g: (0, 0, 0))
    else:
        mods_spec = pl.BlockSpec((1, N_MOD, d), lambda i, g: (i // tiles_per_batch, 0, 0))
    return pl.pallas_call(
        functools.partial(_moe_kernel, per_token=per_token, epg=epg, hid=hid),
        grid=(n // tm, n_groups),
        in_specs=[tok(d), tok(LANES), tok(d), mods_spec,
                  pl.BlockSpec((1,) + w13.shape[1:], lambda i, g: (g, 0, 0, 0)),
                  pl.BlockSpec((1,) + w2.shape[1:], lambda i, g: (g, 0, 0)),
                  pl.BlockSpec((1, d), lambda i, g: (0, 0))],
        out_specs=tok(d),
        out_shape=jax.ShapeDtypeStruct((n, d), F32),
        scratch_shapes=[pltpu.VMEM((tm, d), F32)],
        compiler_params=_cparams(("parallel", "arbitrary")),
        name="moe_sample" if per_token else "moe_prompt",
    )(h2, comb, x1, mods, w13, w2, g_final)


def kernel(x_prompt, x_sample, cache_sb_k, cache_sb_v, cache_fox_k, cache_fox_v, cache_fox_logf,
           c_prompt, c_sample, w_ada, b_ada, g_mix, w_in, b_f, w_ba, w_bb, w_gate, b_gate, w_out,
           g_moe, w_rg, b_rg, w_re, b_re, w1, w3, w2, g_final):
    depth = w_ada.shape[0]
    assert depth == 1, "single-layer trunk"
    bp, t, d = x_prompt.shape
    bs, ts, _ = x_sample.shape
    n_heads = cache_sb_k.shape[2]
    p_len = cache_sb_k.shape[3]
    assert cache_fox_k.shape[2] == n_heads and n_heads * HEAD_DIM * 6 + n_heads == w_in.shape[2]
    assert n_heads == 8 and t % Q_TILE == 0 and Q_TILE % PRE_TILE == 0
    n_groups = w_rg.shape[2]
    n_exp = w_re.shape[2]
    epg = n_exp // n_groups
    assert n_groups + n_exp <= LANES
    w = n_heads * HEAD_DIM
    ns = bs * ts

    w_in_p = jnp.pad(w_in[0], ((0, 0), (0, LANES - n_heads))).astype(BF16)
    b_f_p = jnp.pad(b_f[0], (0, LANES - n_heads)).reshape(1, LANES)
    w_gate_b = w_gate[0].astype(BF16)
    b_gate_r = b_gate[0].reshape(1, -1)
    g_mix_r = g_mix[0].reshape(1, d)
    g_moe_r = g_moe[0].reshape(1, d)
    g_fin_r = g_final.reshape(1, d)
    w_ba_b, w_bb_b, w_out_b = w_ba[0].astype(BF16), w_bb[0].astype(BF16), w_out[0].astype(BF16)
    w_r = jnp.pad(jnp.concatenate([w_rg[0], w_re[0]], axis=1), ((0, 0), (0, LANES - n_groups - n_exp)))
    w_r_hi = w_r.astype(BF16)
    w_r_lo = (w_r - w_r_hi.astype(F32)).astype(BF16)
    wr2 = jnp.concatenate([w_r_hi, w_r_lo], axis=1)
    b_r = jnp.pad(jnp.concatenate([b_rg[0], b_re[0]]), (0, LANES - n_groups - n_exp)).reshape(1, LANES)
    hid = w1.shape[3]
    w13 = jnp.concatenate([w1[0], w3[0]], axis=2).astype(BF16).reshape(n_groups, epg, d, 2 * hid)
    w2_b = w2[0].astype(BF16).reshape(n_groups, epg * hid, d)

    ada = _ada(jnp.concatenate([c_prompt, c_sample], axis=0), w_ada[0], b_ada[0])
    mods_p = ada[:bp].reshape(bp, N_MOD, d)
    mods_s = jnp.repeat(ada[bp:].reshape(bs, N_MOD, d).transpose(1, 0, 2), ts, axis=1)

    hsel = jnp.repeat(jnp.eye(n_heads, dtype=BF16), HEAD_DIM, axis=1)
    (qsT, ks, vsT, qfT, kf, vfT, ksl, vsl, kfl, vfl, logf_p, kn2, gates_p) = _pre_prompt(
        x_prompt, mods_p, g_mix_r, w_in_p, b_f_p, w_gate_b, b_gate_r, hsel, n_heads)
    tk = STRIP
    ids = jnp.arange(tk)
    ntri_p = -(ids[None, :] > ids[:, None]).astype(BF16)
    yaT = _attn_prompt(_sb_prompt_kernel, qsT, ks, vsT, (ntri_p,)).reshape(bp, w, t)
    ybT = _attn_prompt(_fox_prompt_kernel, qfT, kf, vfT, (kn2[:, :, None, :],)).reshape(bp, w, t)
    x1_p, h2_p, comb_p = _post(yaT, ybT, gates_p, x_prompt, mods_p, w_ba_b, w_bb_b, w_out_b,
                               g_moe_r, wr2, w_r_hi, b_r, False, n_groups, epg)
    y_prompt = _moe(h2_p.reshape(bp * t, d), comb_p.reshape(bp * t, LANES), x1_p.reshape(bp * t, d),
                    mods_p, w13, w2_b, g_fin_r, False, t, epg).reshape(bp, t, d)

    proj_s, logf_s, gates_s = _pre_sample(x_sample.reshape(ns, d), mods_s, g_mix_r, w_in_p, b_f_p,
                                          w_gate_b, b_gate_r, n_heads)
    heads = lambda i: proj_s[:, i * w:(i + 1) * w].reshape(bs, ts, n_heads, HEAD_DIM).transpose(0, 2, 1, 3)
    qa_s, ka_s, va_s, qb_s, kb_s, vb_s = [heads(i) for i in range(6)]
    lf_s = logf_s[:, :n_heads].reshape(bs, ts, n_heads).transpose(0, 2, 1)
    pad_k = lambda a: jnp.pad(jnp.swapaxes(a, 2, 3), ((0, 0), (0, 0), (0, 0), (0, LANES - ts)))
    chan = lambda cache: jnp.swapaxes(cache[0], 2, 3)
    nk = p_len + LANES
    ids = jnp.arange(nk)
    tri_s = (ids[:, None] > ids[None, :]).astype(BF16)
    lf_all = jnp.concatenate([cache_fox_logf[0], jnp.pad(lf_s, ((0, 0), (0, 0), (0, LANES - ts)))], axis=2)
    ya_s = _attn_sample(_sb_sample_kernel, qa_s, chan(cache_sb_k), pad_k(ka_s), chan(cache_sb_v), pad_k(va_s),
                        tri_s, pl.BlockSpec(tri_s.shape, lambda bi: (0, 0)))
    yb_s = _attn_sample(_fox_sample_kernel, qb_s, chan(cache_fox_k), pad_k(kb_s), chan(cache_fox_v), pad_k(vb_s),
                        lf_all, pl.BlockSpec((1, n_heads, nk), lambda bi: (bi, 0, 0)))
    to_chan = lambda y: y.reshape(ns, w).T.astype(BF16)[None]
    x1_s, h2_s, comb_s = _post(to_chan(ya_s), to_chan(yb_s), gates_s[None], x_sample.reshape(1, ns, d),
                               mods_s, w_ba_b, w_bb_b, w_out_b, g_moe_r, wr2, w_r_hi, b_r,
                               True, n_groups, epg)
    y_sample = _moe(h2_s[0], comb_s[0], x1_s[0], mods_s, w13, w2_b, g_fin_r, True, ns, epg).reshape(bs, ts, d)

    lead = lambda a: a[None]
    tok_major = lambda a: jnp.swapaxes(a, 2, 3)[None]
    return (y_prompt, y_sample,
            tok_major(ksl), tok_major(vsl), tok_major(kfl), tok_major(vfl), lead(logf_p),
            lead(ka_s), lead(va_s), lead(kb_s), lead(vb_s), lead(lf_s))
```

```python
import functools

import jax
import jax.numpy as jnp
from jax import lax
from jax.experimental import pallas as pl
from jax.experimental.pallas import tpu as pltpu

F32 = jnp.float32
BF16 = jnp.bfloat16

HEAD_DIM = 64
RMS_EPS = 1e-6
N_MOD = 6
LANES = 128
NEG_BIG = -1e30

PRE_TILE = 512
Q_TILE = 1024
STRIP = 256
N_STRIPS = Q_TILE // STRIP
LOG2E = 1.4426950408889634
DEAD_LOG2 = -128.0
NORM_SLACK = 1.02
POST_TILE = 512
MOE_TILE = 1024
VMEM_LIMIT = 56 * 1024 * 1024


def _cparams(sem):
    return pltpu.CompilerParams(dimension_semantics=sem, vmem_limit_bytes=VMEM_LIMIT)


def _log_sigmoid(x):
    return jnp.minimum(x, 0.0) - jnp.log1p(jnp.exp(-jnp.abs(x)))


def _rms_mod(x, g, scale, shift):
    ms = jnp.mean(x * x, axis=-1, keepdims=True)
    y = x * lax.rsqrt(ms + RMS_EPS)
    return (y * g) * (1.0 + scale) + shift


def _mod(mods_ref, i, per_token):
    return mods_ref[i] if per_token else mods_ref[0, i:i + 1, :]


def _dot(a, b):
    return jnp.dot(a, b, preferred_element_type=F32)


def _split3(f):
    hi = f.astype(BF16).astype(F32)
    r = f - hi
    mid = r.astype(BF16).astype(F32)
    lo = (r - mid).astype(BF16).astype(F32)
    return hi, mid, lo


def _lane_cumsum(x):
    n = x.shape[1]
    lane = lax.broadcasted_iota(jnp.int32, x.shape, 1)
    d = 1
    while d < n:
        x = x + jnp.where(lane >= d, pltpu.roll(x, d, axis=1), 0.0)
        d *= 2
    return x


def _ada_kernel(c_ref, w_ref, b_ref, o_ref):
    c = c_ref[...]
    s = c * jax.nn.sigmoid(c)
    o_ref[...] = jnp.dot(s, w_ref[...], preferred_element_type=F32,
                         precision=lax.Precision.HIGHEST) + b_ref[...]


def _ada(c_all, w_ada, b_ada):
    n, d = c_all.shape
    nout = w_ada.shape[1]
    tn = 1024
    return pl.pallas_call(
        _ada_kernel,
        grid=(nout // tn,),
        in_specs=[pl.BlockSpec((n, d), lambda j: (0, 0)),
                  pl.BlockSpec((d, tn), lambda j: (0, j)),
                  pl.BlockSpec((1, tn), lambda j: (0, j))],
        out_specs=pl.BlockSpec((n, tn), lambda j: (0, j)),
        out_shape=jax.ShapeDtypeStruct((n, nout), F32),
        compiler_params=_cparams(("arbitrary",)),
        name="ada",
    )(c_all, w_ada, b_ada.reshape(1, nout))


def _pre_core(x, mods_ref, gmix_ref, win_ref, wg_ref, bg_ref, per_token):
    h = _rms_mod(x, gmix_ref[...], _mod(mods_ref, 1, per_token), _mod(mods_ref, 0, per_token))
    hb = h.astype(BF16)
    proj = _dot(hb, win_ref[...])
    gates = jax.nn.sigmoid(_dot(hb, wg_ref[...]) + bg_ref[...])
    return proj, gates.astype(BF16)


def _pre_prompt_kernel(x_ref, mods_ref, gmix_ref, win_ref, bf_ref, wg_ref, bg_ref, hsel_ref,
                       qsT_ref, ks_ref, vsT_ref, qfT_ref, kf_ref, vfT_ref,
                       ksl_ref, vsl_ref, kfl_ref, vfl_ref, logf_ref, kn2_ref, gates_ref,
                       carry_ref, *, n_heads):
    tm = x_ref.shape[1]
    w = n_heads * HEAD_DIM

    @pl.when(pl.program_id(1) == 0)
    def _():
        carry_ref[...] = jnp.zeros_like(carry_ref)

    proj, gates = _pre_core(x_ref[0], mods_ref, gmix_ref, win_ref, wg_ref, bg_ref, False)
    gates_ref[0] = gates
    qa, ka, va, qb, kb, vb = [proj[:, i * w:(i + 1) * w] for i in range(6)]
    fg = proj[:, 6 * w:6 * w + LANES]
    scale = HEAD_DIM ** -0.5 * LOG2E
    qaT = (qa * scale).T
    kaT = ka.T
    vaT = va.T
    qbT = (qb * scale).T
    kbT = kb.T
    vbT = vb.T
    kn2_ref[0] = _dot(hsel_ref[...], (kbT * kbT).astype(BF16))

    logfT = _log_sigmoid(fg + bf_ref[...]).T[0:n_heads, :]
    logf_ref[0] = logfT
    f = _lane_cumsum(logfT) + carry_ref[:, 0:1]
    carry_ref[...] = jnp.broadcast_to(f[:, tm - 1:tm], carry_ref.shape)
    f_hi, f_mid, f_lo = _split3(f * LOG2E)

    row64 = lax.broadcasted_iota(jnp.int32, (HEAD_DIM, tm), 0)
    row8 = lax.broadcasted_iota(jnp.int32, (8, tm), 0)
    zeros64 = jnp.zeros((HEAD_DIM, tm), BF16)
    ke_parts = []
    for h in range(n_heads):
        def bc(a, n):
            return jnp.broadcast_to(a[h:h + 1, :], (n, tm))
        qe = jnp.where(row64 < 3, 1.0,
                       jnp.where(row64 == 3, bc(f_hi, HEAD_DIM),
                                 jnp.where(row64 == 4, bc(f_mid, HEAD_DIM),
                                           jnp.where(row64 == 5, bc(f_lo, HEAD_DIM), 0.0))))
        sl = slice(h * HEAD_DIM, (h + 1) * HEAD_DIM)
        qfT_ref[0, h, 0:HEAD_DIM, :] = qbT[sl, :].astype(BF16)
        qfT_ref[0, h, HEAD_DIM:2 * HEAD_DIM, :] = qe.astype(BF16)
        qsT_ref[0, h, 0:HEAD_DIM, :] = qaT[sl, :].astype(BF16)
        qsT_ref[0, h, HEAD_DIM:2 * HEAD_DIM, :] = zeros64
        for j in range(tm // STRIP):
            vsT_ref[0, h, j] = vaT[sl, j * STRIP:(j + 1) * STRIP].astype(BF16)
            vfT_ref[0, h, j] = vbT[sl, j * STRIP:(j + 1) * STRIP].astype(BF16)
        ksl_ref[0, h] = kaT[sl, :]
        vsl_ref[0, h] = vaT[sl, :]
        kfl_ref[0, h] = kbT[sl, :]
        vfl_ref[0, h] = vbT[sl, :]
        ke_parts.append(
            jnp.where(row8 == 0, -bc(f_hi, 8),
                      jnp.where(row8 == 1, -bc(f_mid, 8),
                                jnp.where(row8 == 2, -bc(f_lo, 8),
                                          jnp.where(row8 < 6, 1.0, 0.0)))))
    ke_parts.append(jnp.zeros((LANES - 8 * n_heads, tm), F32))
    ke = jnp.concatenate(ke_parts, axis=0).T

    lane = lax.broadcasted_iota(jnp.int32, (tm, LANES), 1)
    for h in range(n_heads):
        base = (h // 2) * LANES
        ka_slab = ka[:, base:base + LANES]
        kb_slab = kb[:, base:base + LANES]
        if h % 2:
            ka_slab = pltpu.roll(ka_slab, HEAD_DIM, axis=1)
            kb_slab = pltpu.roll(kb_slab, HEAD_DIM, axis=1)
        ext = pltpu.roll(ke, HEAD_DIM - 8 * h, axis=1)
        ks_ref[0, h] = jnp.where(lane < HEAD_DIM, ka_slab, 0.0).astype(BF16)
        kf_ref[0, h] = jnp.where(lane < HEAD_DIM, kb_slab,
                                 jnp.where(lane < HEAD_DIM + 8, ext, 0.0)).astype(BF16)


def _pre_prompt(x, mods, g_mix, w_in_p, b_f_p, w_gate, b_gate, hsel, n_heads):
    b, t, d = x.shape
    tm = PRE_TILE
    nt = t // tm
    h, hd = n_heads, HEAD_DIM
    const2 = lambda bi, ti: (0, 0)
    head_t = lambda bi, ti: (bi, 0, 0, ti)
    head_s = lambda bi, ti: (bi, 0, ti, 0)
    blk_t = lambda bi, ti: (bi, 0, ti, 0, 0)
    out_shape = (
        jax.ShapeDtypeStruct((b, h, 2 * hd, t), BF16),
        jax.ShapeDtypeStruct((b, h, t, 2 * hd), BF16),
        jax.ShapeDtypeStruct((b, h, t // STRIP, hd, STRIP), BF16),
        jax.ShapeDtypeStruct((b, h, 2 * hd, t), BF16),
        jax.ShapeDtypeStruct((b, h, t, 2 * hd), BF16),
        jax.ShapeDtypeStruct((b, h, t // STRIP, hd, STRIP), BF16),
        jax.ShapeDtypeStruct((b, h, hd, t), F32),
        jax.ShapeDtypeStruct((b, h, hd, t), F32),
        jax.ShapeDtypeStruct((b, h, hd, t), F32),
        jax.ShapeDtypeStruct((b, h, hd, t), F32),
        jax.ShapeDtypeStruct((b, h, t), F32),
        jax.ShapeDtypeStruct((b, h, t), F32),
        jax.ShapeDtypeStruct((b, t, w_gate.shape[1]), BF16),
    )
    out_specs = (
        pl.BlockSpec((1, h, 2 * hd, tm), head_t),
        pl.BlockSpec((1, h, tm, 2 * hd), head_s),
        pl.BlockSpec((1, h, tm // STRIP, hd, STRIP), blk_t),
        pl.BlockSpec((1, h, 2 * hd, tm), head_t),
        pl.BlockSpec((1, h, tm, 2 * hd), head_s),
        pl.BlockSpec((1, h, tm // STRIP, hd, STRIP), blk_t),
        pl.BlockSpec((1, h, hd, tm), head_t),
        pl.BlockSpec((1, h, hd, tm), head_t),
        pl.BlockSpec((1, h, hd, tm), head_t),
        pl.BlockSpec((1, h, hd, tm), head_t),
        pl.BlockSpec((1, h, tm), lambda bi, ti: (bi, 0, ti)),
        pl.BlockSpec((1, h, tm), lambda bi, ti: (bi, 0, ti)),
        pl.BlockSpec((1, tm, w_gate.shape[1]), lambda bi, ti: (bi, ti, 0)),
    )
    in_specs = [
        pl.BlockSpec((1, tm, d), lambda bi, ti: (bi, ti, 0)),
        pl.BlockSpec((1, N_MOD, d), lambda bi, ti: (bi, 0, 0)),
        pl.BlockSpec((1, d), const2),
        pl.BlockSpec(w_in_p.shape, const2, pipeline_mode=pl.Buffered(1)),
        pl.BlockSpec((1, LANES), const2),
        pl.BlockSpec(w_gate.shape, const2, pipeline_mode=pl.Buffered(1)),
        pl.BlockSpec((1, w_gate.shape[1]), const2),
        pl.BlockSpec(hsel.shape, const2),
    ]
    return pl.pallas_call(
        functools.partial(_pre_prompt_kernel, n_heads=n_heads),
        grid=(b, nt),
        in_specs=in_specs,
        out_specs=out_specs,
        out_shape=out_shape,
        scratch_shapes=[pltpu.VMEM((h, LANES), F32)],
        compiler_params=_cparams(("arbitrary", "arbitrary")),
        name="pre_prompt",
    )(x, mods, g_mix, w_in_p, b_f_p, w_gate, b_gate, hsel)


def _pre_sample_kernel(x_ref, mods_ref, gmix_ref, win_ref, bf_ref, wg_ref, bg_ref,
                       proj_ref, logf_ref, gates_ref, *, n_heads):
    w = n_heads * HEAD_DIM
    proj, gates = _pre_core(x_ref[...], mods_ref, gmix_ref, win_ref, wg_ref, bg_ref, True)
    proj_ref[...] = proj
    gates_ref[...] = gates
    logf_ref[...] = _log_sigmoid(proj[:, 6 * w:6 * w + LANES] + bf_ref[...])


def _pre_sample(x, mods_tok, g_mix, w_in_p, b_f_p, w_gate, b_gate, n_heads):
    n = x.shape[0]
    full = lambda a: pl.BlockSpec(a.shape, lambda i: (0,) * a.ndim)
    args = (x, mods_tok, g_mix, w_in_p, b_f_p, w_gate, b_gate)
    out_shape = (jax.ShapeDtypeStruct((n, w_in_p.shape[1]), F32),
                 jax.ShapeDtypeStruct((n, LANES), F32),
                 jax.ShapeDtypeStruct((n, w_gate.shape[1]), BF16))
    return pl.pallas_call(
        functools.partial(_pre_sample_kernel, n_heads=n_heads),
        grid=(1,),
        in_specs=[full(a) for a in args],
        out_specs=tuple(pl.BlockSpec(s.shape, lambda i: (0, 0)) for s in out_shape),
        out_shape=out_shape,
        compiler_params=_cparams(("arbitrary",)),
        name="pre_sample",
    )(*args)


SB_DEPTHS = 3
FOX_DEPTHS = 4


def _attn_parts(qT_ref, k_ref, vT_ref, static_depths):
    assert qT_ref.shape[3] == N_STRIPS * STRIP and vT_ref.shape[4] == STRIP
    nk = vT_ref.shape[2]
    base = pl.program_id(2) * N_STRIPS
    q_strip = lambda s: qT_ref[0, 0, :, s * STRIP:(s + 1) * STRIP]
    k_block = lambda j: k_ref[0, 0, pl.ds(pl.multiple_of(j * STRIP, STRIP), STRIP), :]
    v_block = lambda j: vT_ref[0, 0, j]
    block = lambda s, u: base + s - u
    clip = lambda j: jnp.clip(j, 0, nk - 1)

    def qk(u, z_scr, strips=range(N_STRIPS)):
        for s in strips:
            z_scr[s] = _dot(k_block(clip(block(s, u))), q_strip(s))

    row = lax.broadcasted_iota(jnp.int32, (STRIP, STRIP), 0)
    col = lax.broadcasted_iota(jnp.int32, (STRIP, STRIP), 1)
    n_pairs = (base + N_STRIPS - static_depths + 1) // 2
    return q_strip, k_block, v_block, block, clip, qk, row, col, n_pairs


def _sb_prompt_kernel(qT_ref, k_ref, vT_ref, ntri_ref, o_ref, z_a, z_b, w_a, w_b):
    STATIC_DEPTHS = SB_DEPTHS
    q_strip, k_block, v_block, block, clip, qk, row, col, n_pairs = _attn_parts(
        qT_ref, k_ref, vT_ref, STATIC_DEPTHS)
    diag = row < col
    sign = jnp.uint32(0x80000000)

    def softplus2(z):
        neg_abs = lax.bitcast_convert_type(lax.bitcast_convert_type(z, jnp.uint32) | sign, F32)
        return jnp.maximum(z, 0.0) + jnp.log2(1.0 + jnp.exp2(neg_abs))

    def suffix(sp):
        sp_b = sp.astype(BF16)
        later = _dot(ntri_ref[...], sp_b)
        return later, later[0:1, :] - sp_b[0:1, :].astype(F32)

    def masked(x, s, u):
        if isinstance(u, int) and u == 0:
            return jnp.where(diag, x, 0.0)
        if isinstance(u, int) and u <= s:
            return x
        return jnp.where(block(s, u) >= 0, x, 0.0)

    w_a[...] = jnp.zeros(w_a.shape, BF16)
    w_b[...] = jnp.zeros(w_b.shape, BF16)
    subs = [(s, u) for u in range(STATIC_DEPTHS) for s in range(N_STRIPS)]
    zs = [_dot(k_block(clip(block(s, u))), q_strip(s)) for s, u in subs]
    qk(STATIC_DEPTHS, z_a)
    c = [jnp.zeros((1, STRIP), F32)] * N_STRIPS
    acc = [jnp.zeros((HEAD_DIM, STRIP), F32)] * N_STRIPS
    mids = []
    for (s, u), z in zip(subs, zs):
        sp_full = softplus2(z)
        later, total = suffix(masked(sp_full, s, u))
        mids.append((z - sp_full, later, c[s]))
        c[s] = c[s] + total
    for (s, u), (lsig, later, c_before) in zip(subs, mids):
        wgt = masked(jnp.exp2(lsig + later), s, u)
        acc[s] = acc[s] + _dot(v_block(clip(block(s, u))), wgt.astype(BF16)) * jnp.exp2(c_before)

    def pv(u, w_scr, carry, which, strips=range(N_STRIPS)):
        return tuple((cr[0], cr[1] + _dot(v_block(clip(block(s, u))), w_scr[s]) * cr[2 + which]) + cr[2:]
                     if s in strips else cr for s, cr in enumerate(carry))

    def softplus_phase(z_scr, carry, which, u):
        carry = list(carry)
        mids = []
        for s in range(N_STRIPS):
            c, acc, sa, sb = carry[s]
            z = z_scr[s]
            sp_full = softplus2(z)
            later, total = suffix(masked(sp_full, s, u))
            mids.append((z - sp_full, later))
            scale = jnp.exp2(c)
            carry[s] = (c + total, acc, sa, scale) if which else (c + total, acc, scale, sb)
        return tuple(carry), mids

    def weight_phase(w_scr, mids, u):
        for s, (lsig, later) in enumerate(mids):
            w_scr[s] = masked(jnp.exp2(lsig + later), s, u).astype(BF16)

    def pair(u, carry):
        carry = pv(u - 2, w_a, carry, 0)
        for s in range(N_STRIPS):
            carry = pv(u - 1, w_b, carry, 1, (s,))
            qk(u + 1, z_b, (s,))
        carry, mids_a = softplus_phase(z_a, carry, 0, u)
        qk(u + 2, z_a)
        carry, mids_b = softplus_phase(z_b, carry, 1, u + 1)
        weight_phase(w_a, mids_a, u)
        weight_phase(w_b, mids_b, u + 1)
        return carry

    def alive(carry, u_next):
        left = [jnp.where(block(s, u_next) >= 0, cr[0], NEG_BIG) for s, cr in enumerate(carry)]
        return jnp.max(functools.reduce(jnp.maximum, left)) > DEAD_LOG2

    def body(state):
        i, _, cr = state
        u = STATIC_DEPTHS + 2 * i
        cr = pair(u, cr)
        return i + 1, alive(cr, u + 2), cr

    one = jnp.ones((1, STRIP), F32)
    carry = tuple((c[s], acc[s], one, one) for s in range(N_STRIPS))
    i_end, _, carry = lax.while_loop(lambda st: jnp.logical_and(st[0] < n_pairs, st[1]), body,
                                     (jnp.int32(0), alive(carry, STATIC_DEPTHS), carry))
    last = STATIC_DEPTHS + 2 * i_end - 1
    carry = pv(last - 1, w_a, carry, 0)
    carry = pv(last, w_b, carry, 1)
    for s in range(N_STRIPS):
        o_ref[0, 0, :, s * STRIP:(s + 1) * STRIP] = carry[s][1].astype(BF16)


def _fox_prompt_kernel(qT_ref, k_ref, vT_ref, kn2_ref, o_ref, z_a, z_b, p_a, p_b):
    STATIC_DEPTHS = FOX_DEPTHS
    q_strip, k_block, v_block, block, clip, qk, row, col, n_pairs = _attn_parts(
        qT_ref, k_ref, vT_ref, STATIC_DEPTHS)
    diag = row <= col

    def masked(sc, s, u):
        if isinstance(u, int) and u == 0:
            return jnp.where(diag, sc, NEG_BIG)
        if isinstance(u, int) and u <= s:
            return sc
        return jnp.where(block(s, u) >= 0, sc, NEG_BIG)

    p_a[...] = jnp.zeros(p_a.shape, BF16)
    p_b[...] = jnp.zeros(p_b.shape, BF16)
    subs = [(s, u) for u in range(STATIC_DEPTHS) for s in range(N_STRIPS)]
    zs = [_dot(k_block(clip(block(s, u))), q_strip(s)) for s, u in subs]
    qk(STATIC_DEPTHS, z_a)
    m = [jnp.full((1, STRIP), NEG_BIG, F32)] * N_STRIPS
    l = [jnp.zeros((1, STRIP), F32)] * N_STRIPS
    acc = [jnp.zeros((HEAD_DIM, STRIP), F32)] * N_STRIPS
    for (s, u), sc in zip(subs, zs):
        sc = masked(sc, s, u)
        m_new = jnp.maximum(m[s], jnp.max(sc, axis=0, keepdims=True))
        alpha = jnp.exp2(m[s] - m_new)
        p = jnp.exp2(sc - m_new)
        l[s] = alpha * l[s] + jnp.sum(p, axis=0, keepdims=True)
        acc[s] = alpha * acc[s] + _dot(v_block(clip(block(s, u))), p.astype(BF16))
        m[s] = m_new

    def pv(u, p_scr, carry, which, strips=range(N_STRIPS)):
        return tuple((cr[0], cr[1], cr[3 + which] * cr[2] + _dot(v_block(clip(block(s, u))), p_scr[s])) + cr[3:]
                     if s in strips else cr for s, cr in enumerate(carry))

    def valu(z_scr, p_scr, carry, which, u):
        out = []
        for s in range(N_STRIPS):
            m, l, acc, aa, ab = carry[s]
            m_new = jnp.maximum(m, jnp.max(masked(z_scr[s], s, u), axis=0, keepdims=True))
            alpha = jnp.exp2(m - m_new)
            p = jnp.exp2(masked(z_scr[s], s, u) - m_new)
            p_scr[s] = p.astype(BF16)
            l = alpha * l + jnp.sum(p, axis=0, keepdims=True)
            out.append((m_new, l, acc, aa, alpha) if which else (m_new, l, acc, alpha, ab))
        return tuple(out)

    def pair(u, carry):
        carry = pv(u - 2, p_a, carry, 0)
        carry = pv(u - 1, p_b, carry, 1)
        qk(u + 1, z_b)
        carry = valu(z_a, p_a, carry, 0, u)
        qk(u + 2, z_a)
        return valu(z_b, p_b, carry, 1, u + 1)

    q_all = qT_ref[0, 0]
    q_f = q_all[0:HEAD_DIM, :].astype(F32)
    q_norm = jnp.sqrt(jnp.sum(q_f * q_f, axis=0, keepdims=True))
    f_rows = q_all[HEAD_DIM:HEAD_DIM + 16, :].astype(F32)
    f_q = f_rows[3:4, :] + f_rows[4:5, :] + f_rows[5:6, :]
    k_norm = jnp.sqrt(jnp.max(kn2_ref[0, 0], axis=1, keepdims=True))
    bound_q = NORM_SLACK * q_norm * k_norm + f_q

    def alive(carry, u_next):
        left = []
        for s in range(N_STRIPS):
            j = block(s, u_next)
            rows = k_ref[0, 0, pl.ds(pl.multiple_of(clip(j) * STRIP + STRIP - 16, 16), 16), :].astype(F32)
            f_end = -(rows[15:16, HEAD_DIM:HEAD_DIM + 1] + rows[15:16, HEAD_DIM + 1:HEAD_DIM + 2]
                      + rows[15:16, HEAD_DIM + 2:HEAD_DIM + 3])
            gap = bound_q[:, s * STRIP:(s + 1) * STRIP] - f_end - carry[s][0]
            left.append(jnp.where(j >= 0, gap, NEG_BIG))
        return jnp.max(functools.reduce(jnp.maximum, left)) > DEAD_LOG2

    def body(state):
        i, _, cr = state
        u = STATIC_DEPTHS + 2 * i
        cr = pair(u, cr)
        return i + 1, alive(cr, u + 2), cr

    one = jnp.ones((1, STRIP), F32)
    carry = tuple((m[s], l[s], acc[s], one, one) for s in range(N_STRIPS))
    i_end, _, carry = lax.while_loop(lambda st: jnp.logical_and(st[0] < n_pairs, st[1]), body,
                                     (jnp.int32(0), alive(carry, STATIC_DEPTHS), carry))
    last = STATIC_DEPTHS + 2 * i_end - 1
    carry = pv(last - 1, p_a, carry, 0)
    carry = pv(last, p_b, carry, 1)
    for s in range(N_STRIPS):
        _, l, acc, _, _ = carry[s]
        o_ref[0, 0, :, s * STRIP:(s + 1) * STRIP] = (acc / l).astype(BF16)


def _attn_prompt(kernel, qT, k, vT, extra=()):
    b, h, kd, t = qT.shape
    nk, tk = vT.shape[2], vT.shape[4]
    tq = Q_TILE
    in_specs = [
        pl.BlockSpec((1, 1, kd, tq), lambda bi, hi, qi: (bi, hi, 0, qi)),
        pl.BlockSpec((1, 1, t, kd), lambda bi, hi, qi: (bi, hi, 0, 0)),
        pl.BlockSpec((1, 1, nk, HEAD_DIM, tk), lambda bi, hi, qi: (bi, hi, 0, 0, 0)),
    ] + [pl.BlockSpec(a.shape, lambda bi, hi, qi: (0, 0)) if a.ndim == 2 else
         pl.BlockSpec((1, 1) + a.shape[2:], lambda bi, hi, qi: (bi, hi, 0, 0)) for a in extra]
    scores = pltpu.VMEM((N_STRIPS, STRIP, STRIP), F32)
    probs = pltpu.VMEM((N_STRIPS, STRIP, STRIP), BF16)
    return pl.pallas_call(
        kernel,
        grid=(b, h, t // tq),
        in_specs=in_specs,
        out_specs=pl.BlockSpec((1, 1, HEAD_DIM, tq), lambda bi, hi, qi: (bi, hi, 0, qi)),
        out_shape=jax.ShapeDtypeStruct((b, h, HEAD_DIM, t), BF16),
        scratch_shapes=[scores, scores, probs, probs],
        compiler_params=_cparams(("parallel", "parallel", "arbitrary")),
        name=kernel.__name__.strip("_"),
    )(qT, k, vT, *extra)


def _sample_kv(ck_ref, kn_ref, cv_ref, vn_ref, h):
    kT = jnp.concatenate([ck_ref[0, h], kn_ref[0, h]], axis=1).astype(BF16)
    vT = jnp.concatenate([cv_ref[0, h], vn_ref[0, h]], axis=1).astype(BF16)
    return kT, vT


def _pv(p, vT):
    return lax.dot_general(p, vT, (((1,), (1,)), ((), ())), preferred_element_type=F32)


def _sb_sample_kernel(q_ref, ck_ref, kn_ref, cv_ref, vn_ref, tri_ref, o_ref):
    n_heads, tq = q_ref.shape[1], q_ref.shape[2]
    p_len = ck_ref.shape[3]
    nk = p_len + kn_ref.shape[3]
    row = lax.broadcasted_iota(jnp.int32, (tq, nk), 0)
    col = lax.broadcasted_iota(jnp.int32, (tq, nk), 1)
    mask = col < row + p_len
    heads = []
    for h in range(n_heads):
        kT, vT = _sample_kv(ck_ref, kn_ref, cv_ref, vn_ref, h)
        q = (q_ref[0, h] * HEAD_DIM ** -0.5).astype(BF16)
        z = _dot(q, kT)
        ls_full = -(jnp.maximum(z, 0.0) + jnp.log1p(jnp.exp(-jnp.abs(z))))
        heads.append((z + ls_full, jnp.where(mask, ls_full, 0.0).astype(BF16), vT))
    later = _dot(jnp.concatenate([ls for _, ls, _ in heads], axis=0), tri_ref[...])
    for h, (lsig, _, vT) in enumerate(heads):
        wgt = jnp.where(mask, jnp.exp(lsig + later[h * tq:(h + 1) * tq, :]), 0.0)
        o_ref[0, :, h * HEAD_DIM:(h + 1) * HEAD_DIM] = _pv(wgt.astype(BF16), vT)


def _fox_sample_kernel(q_ref, ck_ref, kn_ref, cv_ref, vn_ref, lf_ref, o_ref):
    n_heads, tq = q_ref.shape[1], q_ref.shape[2]
    p_len = ck_ref.shape[3]
    nk = p_len + kn_ref.shape[3]
    row = lax.broadcasted_iota(jnp.int32, (tq, nk), 0)
    col = lax.broadcasted_iota(jnp.int32, (tq, nk), 1)
    mask = col <= row + p_len
    diag = col == row + p_len
    f_all = _lane_cumsum(lf_ref[0])
    kvs = [_sample_kv(ck_ref, kn_ref, cv_ref, vn_ref, h) for h in range(n_heads)]
    zs = [_dot((q_ref[0, h] * HEAD_DIM ** -0.5).astype(BF16), kvs[h][0]) for h in range(n_heads)]
    ps = []
    for h in range(n_heads):
        fk = f_all[h:h + 1, :]
        fq = jnp.sum(jnp.where(diag, fk, 0.0), axis=1, keepdims=True)
        z = jnp.where(mask, zs[h] + fq - fk, NEG_BIG)
        p = jnp.exp(z - jnp.max(z, axis=1, keepdims=True))
        ps.append((p.astype(BF16), jnp.sum(p, axis=1, keepdims=True)))
    for h, (p, l) in enumerate(ps):
        o_ref[0, :, h * HEAD_DIM:(h + 1) * HEAD_DIM] = _pv(p, kvs[h][1]) / l


def _attn_sample(kernel, q, ck, kn, cv, vn, extra, extra_spec):
    b, h, tq, hd = q.shape
    per_b = lambda a: pl.BlockSpec((1,) + a.shape[1:], lambda bi: (bi,) + (0,) * (a.ndim - 1))
    return pl.pallas_call(
        kernel,
        grid=(b,),
        in_specs=[per_b(a) for a in (q, ck, kn, cv, vn)] + [extra_spec],
        out_specs=pl.BlockSpec((1, tq, h * hd), lambda bi: (bi, 0, 0)),
        out_shape=jax.ShapeDtypeStruct((b, tq, h * hd), F32),
        compiler_params=_cparams(("parallel",)),
        name=kernel.__name__.strip("_"),
    )(q, ck, kn, cv, vn, extra)


def _route(lt, n_groups, epg):
    n = lt.shape[1]
    g = [lt[i:i + 1, :] for i in range(n_groups)]
    gmax = functools.reduce(jnp.maximum, g)
    g_w = 1.0 / functools.reduce(jnp.add, [jnp.exp(gi - gmax) for gi in g])
    is_g, taken = [], None
    for gi in g:
        hit = gi >= gmax
        if taken is not None:
            hit = jnp.logical_and(hit, jnp.logical_not(taken))
        taken = hit if taken is None else jnp.logical_or(taken, hit)
        is_g.append(hit)
    le = [lt[n_groups + i:n_groups + i + 1, :] for i in range(n_groups * epg)]
    e_sel = []
    for i in range(epg):
        v = le[(n_groups - 1) * epg + i]
        for gi in reversed(range(n_groups - 1)):
            v = jnp.where(is_g[gi], le[gi * epg + i], v)
        e_sel.append(v)
    emax = functools.reduce(jnp.maximum, e_sel)
    pe = [jnp.exp(v - emax) for v in e_sel]
    pden = functools.reduce(jnp.add, pe)
    prob = [p / pden for p in pe]

    def first_argmax(vals):
        vmax = functools.reduce(jnp.maximum, vals)
        hits, tk = [], None
        for v in vals:
            hit = v >= vmax
            if tk is not None:
                hit = jnp.logical_and(hit, jnp.logical_not(tk))
            tk = hit if tk is None else jnp.logical_or(tk, hit)
            hits.append(hit)
        return vmax, hits

    p1, t1 = first_argmax(prob)
    p2, t2 = first_argmax([jnp.where(t, -1.0, p) for t, p in zip(t1, prob)])
    tot = p1 + p2
    w1 = g_w * (p1 / tot)
    w2 = g_w * (p2 / tot)
    rows = lax.broadcasted_iota(jnp.int32, (LANES, n), 0)
    comb = jnp.zeros((LANES, n), F32)
    for gi in range(n_groups):
        for i in range(epg):
            val = jnp.where(is_g[gi], jnp.where(t1[i], w1, 0.0) + jnp.where(t2[i], w2, 0.0), 0.0)
            comb = jnp.where(rows == gi * epg + i, jnp.broadcast_to(val, (LANES, n)), comb)
    return comb


def _post_kernel(yaT_ref, ybT_ref, gates_ref, x_ref, mods_ref, wba_ref, wbb_ref, wout_ref,
                 gmoe_ref, wr2_ref, wrhi_ref, br_ref, x1_ref, h2_ref, comb_ref,
                 *, per_token, n_groups, epg):
    d = x_ref.shape[2]
    tdot = lambda aT, w: lax.dot_general(aT, w, (((0,), (0,)), ((), ())), preferred_element_type=F32)
    ua = tdot(yaT_ref[0], wba_ref[...])
    ub = tdot(ybT_ref[0], wbb_ref[...])
    gates = gates_ref[0].astype(F32)
    mix = gates[:, :d] * ua + gates[:, d:] * ub
    x1 = x_ref[0] + _mod(mods_ref, 2, per_token) * _dot(mix.astype(BF16), wout_ref[...])
    x1_ref[0] = x1
    h2 = _rms_mod(x1, gmoe_ref[...], _mod(mods_ref, 4, per_token), _mod(mods_ref, 3, per_token))
    h2_hi = h2.astype(BF16)
    h2_ref[0] = h2_hi
    h2_lo = (h2 - h2_hi.astype(F32)).astype(BF16)
    a = _dot(h2_hi, wr2_ref[...])
    logits = a[:, :LANES] + a[:, LANES:] + _dot(h2_lo, wrhi_ref[...]) + br_ref[...]
    comb_ref[0] = _route(logits.T, n_groups, epg).T


def _post(yaT, ybT, gates, x, mods, w_ba, w_bb, w_out, g_moe, wr2, wrhi, br, per_token, n_groups, epg):
    b, t, d = x.shape
    tm = min(POST_TILE, t)
    w = yaT.shape[1]
    const2 = lambda bi, ti: (0, 0)
    tok = lambda last: pl.BlockSpec((1, tm, last), lambda bi, ti: (bi, ti, 0))
    chan = pl.BlockSpec((1, w, tm), lambda bi, ti: (bi, 0, ti))
    if per_token:
        mods_spec = pl.BlockSpec(mods.shape, lambda bi, ti: (0, 0, 0))
    else:
        mods_spec = pl.BlockSpec((1, N_MOD, d), lambda bi, ti: (bi, 0, 0))
    in_specs = [chan, chan, tok(2 * d), tok(d), mods_spec,
                pl.BlockSpec(w_ba.shape, const2), pl.BlockSpec(w_bb.shape, const2),
                pl.BlockSpec(w_out.shape, const2), pl.BlockSpec((1, d), const2),
                pl.BlockSpec(wr2.shape, const2), pl.BlockSpec(wrhi.shape, const2),
                pl.BlockSpec((1, LANES), const2)]
    out_shape = (jax.ShapeDtypeStruct((b, t, d), F32),
                 jax.ShapeDtypeStruct((b, t, d), BF16),
                 jax.ShapeDtypeStruct((b, t, LANES), F32))
    return pl.pallas_call(
        functools.partial(_post_kernel, per_token=per_token, n_groups=n_groups, epg=epg),
        grid=(b, t // tm),
        in_specs=in_specs,
        out_specs=(tok(d), tok(d), tok(LANES)),
        out_shape=out_shape,
        compiler_params=_cparams(("parallel", "parallel")),
        name="post_sample" if per_token else "post_prompt",
    )(yaT, ybT, gates, x, mods, w_ba, w_bb, w_out, g_moe, wr2, wrhi, br)


def _moe_kernel(h2_ref, comb_ref, x1_ref, mods_ref, w13_ref, w2_ref, gfin_ref, y_ref, acc_ref,
                *, per_token, epg, hid):
    g = pl.program_id(1)

    @pl.when(g == 0)
    def _():
        acc_ref[...] = jnp.zeros_like(acc_ref)

    h2 = h2_ref[...]
    comb = comb_ref[...]
    lane = lax.broadcasted_iota(jnp.int32, comb.shape, 1)
    ups = [_dot(h2, w13_ref[0, i]) for i in range(epg)]
    total = None
    for c in range(0, epg, 2):
        acts = []
        for i in range(c, c + 2):
            a1 = ups[i][:, :hid]
            a3 = ups[i][:, hid:]
            cw = jnp.sum(jnp.where(lane == g * epg + i, comb, 0.0), axis=1, keepdims=True)
            acts.append(((a1 * jax.nn.sigmoid(a1)) * a3 * cw).astype(BF16))
        part = _dot(jnp.concatenate(acts, axis=1), w2_ref[0, c * hid:(c + 2) * hid, :])
        total = part if total is None else total + part
    acc_ref[...] += total

    @pl.when(g == pl.num_programs(1) - 1)
    def _():
        x2 = x1_ref[...] + _mod(mods_ref, 5, per_token) * acc_ref[...]
        ms = jnp.mean(x2 * x2, axis=-1, keepdims=True)
        y_ref[...] = (x2 * lax.rsqrt(ms + RMS_EPS)) * gfin_ref[...]


def _moe(h2, comb, x1, mods, w13, w2, g_final, per_token, tokens_per_batch, epg):
    n, d = x1.shape
    tm = min(MOE_TILE, n)
    n_groups = w13.shape[0]
    hid = w2.shape[1] // epg
    assert epg % 2 == 0
    tiles_per_batch = tokens_per_batch // tm if not per_token else 1
    tok = lambda last: pl.BlockSpec((tm, last), lambda i, g: (i, 0))
    if per_token:
        mods_spec = pl.BlockSpec(mods.shape, lambda i, g: (0, 0, 0))
    else:
        mods_spec = pl.BlockSpec((1, N_MOD, d), lambda i, g: (i // tiles_per_batch, 0, 0))
    return pl.pallas_call(
        functools.partial(_moe_kernel, per_token=per_token, epg=epg, hid=hid),
        grid=(n // tm, n_groups),
        in_specs=[tok(d), tok(LANES), tok(d), mods_spec,
                  pl.BlockSpec((1,) + w13.shape[1:], lambda i, g: (g, 0, 0, 0)),
                  pl.BlockSpec((1,) + w2.shape[1:], lambda i, g: (g, 0, 0)),
                  pl.BlockSpec((1, d), lambda i, g: (0, 0))],
        out_specs=tok(d),
        out_shape=jax.ShapeDtypeStruct((n, d), F32),
        scratch_shapes=[pltpu.VMEM((tm, d), F32)],
        compiler_params=_cparams(("parallel", "arbitrary")),
        name="moe_sample" if per_token else "moe_prompt",
    )(h2, comb, x1, mods, w13, w2, g_final)


def kernel(x_prompt, x_sample, cache_sb_k, cache_sb_v, cache_fox_k, cache_fox_v, cache_fox_logf,
           c_prompt, c_sample, w_ada, b_ada, g_mix, w_in, b_f, w_ba, w_bb, w_gate, b_gate, w_out,
           g_moe, w_rg, b_rg, w_re, b_re, w1, w3, w2, g_final):
    depth = w_ada.shape[0]
    assert depth == 1, "single-layer trunk"
    bp, t, d = x_prompt.shape
    bs, ts, _ = x_sample.shape
    n_heads = cache_sb_k.shape[2]
    p_len = cache_sb_k.shape[3]
    assert cache_fox_k.shape[2] == n_heads and n_heads * HEAD_DIM * 6 + n_heads == w_in.shape[2]
    assert n_heads == 8 and t % Q_TILE == 0 and Q_TILE % PRE_TILE == 0
    n_groups = w_rg.shape[2]
    n_exp = w_re.shape[2]
    epg = n_exp // n_groups
    assert n_groups + n_exp <= LANES
    w = n_heads * HEAD_DIM
    ns = bs * ts

    w_in_p = jnp.pad(w_in[0], ((0, 0), (0, LANES - n_heads))).astype(BF16)
    b_f_p = jnp.pad(b_f[0], (0, LANES - n_heads)).reshape(1, LANES)
    w_gate_b = w_gate[0].astype(BF16)
    b_gate_r = b_gate[0].reshape(1, -1)
    g_mix_r = g_mix[0].reshape(1, d)
    g_moe_r = g_moe[0].reshape(1, d)
    g_fin_r = g_final.reshape(1, d)
    w_ba_b, w_bb_b, w_out_b = w_ba[0].astype(BF16), w_bb[0].astype(BF16), w_out[0].astype(BF16)
    w_r = jnp.pad(jnp.concatenate([w_rg[0], w_re[0]], axis=1), ((0, 0), (0, LANES - n_groups - n_exp)))
    w_r_hi = w_r.astype(BF16)
    w_r_lo = (w_r - w_r_hi.astype(F32)).astype(BF16)
    wr2 = jnp.concatenate([w_r_hi, w_r_lo], axis=1)
    b_r = jnp.pad(jnp.concatenate([b_rg[0], b_re[0]]), (0, LANES - n_groups - n_exp)).reshape(1, LANES)
    hid = w1.shape[3]
    w13 = jnp.concatenate([w1[0], w3[0]], axis=2).astype(BF16).reshape(n_groups, epg, d, 2 * hid)
    w2_b = w2[0].astype(BF16).reshape(n_groups, epg * hid, d)

    ada = _ada(jnp.concatenate([c_prompt, c_sample], axis=0), w_ada[0], b_ada[0])
    mods_p = ada[:bp].reshape(bp, N_MOD, d)
    mods_s = jnp.repeat(ada[bp:].reshape(bs, N_MOD, d).transpose(1, 0, 2), ts, axis=1)

    hsel = jnp.repeat(jnp.eye(n_heads, dtype=BF16), HEAD_DIM, axis=1)
    (qsT, ks, vsT, qfT, kf, vfT, ksl, vsl, kfl, vfl, logf_p, kn2, gates_p) = _pre_prompt(
        x_prompt, mods_p, g_mix_r, w_in_p, b_f_p, w_gate_b, b_gate_r, hsel, n_heads)
    tk = STRIP
    ids = jnp.arange(tk)
    ntri_p = -(ids[None, :] > ids[:, None]).astype(BF16)
    yaT = _attn_prompt(_sb_prompt_kernel, qsT, ks, vsT, (ntri_p,)).reshape(bp, w, t)
    ybT = _attn_prompt(_fox_prompt_kernel, qfT, kf, vfT, (kn2[:, :, None, :],)).reshape(bp, w, t)
    x1_p, h2_p, comb_p = _post(yaT, ybT, gates_p, x_prompt, mods_p, w_ba_b, w_bb_b, w_out_b,
                               g_moe_r, wr2, w_r_hi, b_r, False, n_groups, epg)
    y_prompt = _moe(h2_p.reshape(bp * t, d), comb_p.reshape(bp * t, LANES), x1_p.reshape(bp * t, d),
                    mods_p, w13, w2_b, g_fin_r, False, t, epg).reshape(bp, t, d)

    proj_s, logf_s, gates_s = _pre_sample(x_sample.reshape(ns, d), mods_s, g_mix_r, w_in_p, b_f_p,
                                          w_gate_b, b_gate_r, n_heads)
    heads = lambda i: proj_s[:, i * w:(i + 1) * w].reshape(bs, ts, n_heads, HEAD_DIM).transpose(0, 2, 1, 3)
    qa_s, ka_s, va_s, qb_s, kb_s, vb_s = [heads(i) for i in range(6)]
    lf_s = logf_s[:, :n_heads].reshape(bs, ts, n_heads).transpose(0, 2, 1)
    pad_k = lambda a: jnp.pad(jnp.swapaxes(a, 2, 3), ((0, 0), (0, 0), (0, 0), (0, LANES - ts)))
    chan = lambda cache: jnp.swapaxes(cache[0], 2, 3)
    nk = p_len + LANES
    ids = jnp.arange(nk)
    tri_s = (ids[:, None] > ids[None, :]).astype(BF16)
    lf_all = jnp.concatenate([cache_fox_logf[0], jnp.pad(lf_s, ((0, 0), (0, 0), (0, LANES - ts)))], axis=2)
    ya_s = _attn_sample(_sb_sample_kernel, qa_s, chan(cache_sb_k), pad_k(ka_s), chan(cache_sb_v), pad_k(va_s),
                        tri_s, pl.BlockSpec(tri_s.shape, lambda bi: (0, 0)))
    yb_s = _attn_sample(_fox_sample_kernel, qb_s, chan(cache_fox_k), pad_k(kb_s), chan(cache_fox_v), pad_k(vb_s),
                        lf_all, pl.BlockSpec((1, n_heads, nk), lambda bi: (bi, 0, 0)))
    to_chan = lambda y: y.reshape(ns, w).T.astype(BF16)[None]
    x1_s, h2_s, comb_s = _post(to_chan(ya_s), to_chan(yb_s), gates_s[None], x_sample.reshape(1, ns, d),
                               mods_s, w_ba_b, w_bb_b, w_out_b, g_moe_r, wr2, w_r_hi, b_r,
                               True, n_groups, epg)
    y_sample = _moe(h2_s[0], comb_s[0], x1_s[0], mods_s, w13, w2_b, g_fin_r, True, ns, epg).reshape(bs, ts, d)

    lead = lambda a: a[None]
    tok_major = lambda a: jnp.swapaxes(a, 2, 3)[None]
    return (y_prompt, y_sample,
            tok_major(ksl), tok_major(vsl), tok_major(kfl), tok_major(vfl), lead(logf_p),
            lead(ka_s), lead(va_s), lead(kb_s), lead(vb_s), lead(lf_s))
```

```python
import functools

import jax
import jax.numpy as jnp
from jax import lax
from jax.experimental import pallas as pl
from jax.experimental.pallas import tpu as pltpu

F32 = jnp.float32
BF16 = jnp.bfloat16

HEAD_DIM = 64
RMS_EPS = 1e-6
N_MOD = 6
LANES = 128
NEG_BIG = -1e30

PRE_TILE = 512
Q_TILE = 1024
STRIP = 256
N_STRIPS = Q_TILE // STRIP
LOG2E = 1.4426950408889634
DEAD_LOG2 = -128.0
NORM_SLACK = 1.02
POST_TILE = 512
MOE_TILE = 1024
VMEM_LIMIT = 56 * 1024 * 1024


def _cparams(sem):
    return pltpu.CompilerParams(dimension_semantics=sem, vmem_limit_bytes=VMEM_LIMIT)


def _log_sigmoid(x):
    return jnp.minimum(x, 0.0) - jnp.log1p(jnp.exp(-jnp.abs(x)))


def _rms_mod(x, g, scale, shift):
    ms = jnp.mean(x * x, axis=-1, keepdims=True)
    y = x * lax.rsqrt(ms + RMS_EPS)
    return (y * g) * (1.0 + scale) + shift


def _mod(mods_ref, i, per_token):
    return mods_ref[i] if per_token else mods_ref[0, i:i + 1, :]


def _dot(a, b):
    return jnp.dot(a, b, preferred_element_type=F32)


def _split3(f):
    hi = f.astype(BF16).astype(F32)
    r = f - hi
    mid = r.astype(BF16).astype(F32)
    lo = (r - mid).astype(BF16).astype(F32)
    return hi, mid, lo


def _lane_cumsum(x):
    n = x.shape[1]
    lane = lax.broadcasted_iota(jnp.int32, x.shape, 1)
    d = 1
    while d < n:
        x = x + jnp.where(lane >= d, pltpu.roll(x, d, axis=1), 0.0)
        d *= 2
    return x


def _ada_kernel(c_ref, w_ref, b_ref, o_ref):
    c = c_ref[...]
    s = c * jax.nn.sigmoid(c)
    o_ref[...] = jnp.dot(s, w_ref[...], preferred_element_type=F32,
                         precision=lax.Precision.HIGHEST) + b_ref[...]


def _ada(c_all, w_ada, b_ada):
    n, d = c_all.shape
    nout = w_ada.shape[1]
    tn = 1024
    return pl.pallas_call(
        _ada_kernel,
        grid=(nout // tn,),
        in_specs=[pl.BlockSpec((n, d), lambda j: (0, 0)),
                  pl.BlockSpec((d, tn), lambda j: (0, j)),
                  pl.BlockSpec((1, tn), lambda j: (0, j))],
        out_specs=pl.BlockSpec((n, tn), lambda j: (0, j)),
        out_shape=jax.ShapeDtypeStruct((n, nout), F32),
        compiler_params=_cparams(("arbitrary",)),
        name="ada",
    )(c_all, w_ada, b_ada.reshape(1, nout))


def _pre_core(x, mods_ref, gmix_ref, win_ref, wg_ref, bg_ref, per_token):
    h = _rms_mod(x, gmix_ref[...], _mod(mods_ref, 1, per_token), _mod(mods_ref, 0, per_token))
    hb = h.astype(BF16)
    proj = _dot(hb, win_ref[...])
    gates = jax.nn.sigmoid(_dot(hb, wg_ref[...]) + bg_ref[...])
    return proj, gates.astype(BF16)


def _pre_prompt_kernel(x_ref, mods_ref, gmix_ref, win_ref, bf_ref, wg_ref, bg_ref, hsel_ref,
                       qsT_ref, ks_ref, vsT_ref, qfT_ref, kf_ref, vfT_ref,
                       ksl_ref, vsl_ref, kfl_ref, vfl_ref, logf_ref, kn2_ref, gates_ref,
                       carry_ref, *, n_heads):
    tm = x_ref.shape[1]
    w = n_heads * HEAD_DIM

    @pl.when(pl.program_id(1) == 0)
    def _():
        carry_ref[...] = jnp.zeros_like(carry_ref)

    proj, gates = _pre_core(x_ref[0], mods_ref, gmix_ref, win_ref, wg_ref, bg_ref, False)
    gates_ref[0] = gates
    qa, ka, va, qb, kb, vb = [proj[:, i * w:(i + 1) * w] for i in range(6)]
    fg = proj[:, 6 * w:6 * w + LANES]
    scale = HEAD_DIM ** -0.5 * LOG2E
    qaT = (qa * scale).T
    kaT = ka.T
    vaT = va.T
    qbT = (qb * scale).T
    kbT = kb.T
    vbT = vb.T
    kn2_ref[0] = _dot(hsel_ref[...], (kbT * kbT).astype(BF16))

    logfT = _log_sigmoid(fg + bf_ref[...]).T[0:n_heads, :]
    logf_ref[0] = logfT
    f = _lane_cumsum(logfT) + carry_ref[:, 0:1]
    carry_ref[...] = jnp.broadcast_to(f[:, tm - 1:tm], carry_ref.shape)
    f_hi, f_mid, f_lo = _split3(f * LOG2E)

    row64 = lax.broadcasted_iota(jnp.int32, (HEAD_DIM, tm), 0)
    row8 = lax.broadcasted_iota(jnp.int32, (8, tm), 0)
    zeros64 = jnp.zeros((HEAD_DIM, tm), BF16)
    ke_parts = []
    for h in range(n_heads):
        def bc(a, n):
            return jnp.broadcast_to(a[h:h + 1, :], (n, tm))
        qe = jnp.where(row64 < 3, 1.0,
                       jnp.where(row64 == 3, bc(f_hi, HEAD_DIM),
                                 jnp.where(row64 == 4, bc(f_mid, HEAD_DIM),
                                           jnp.where(row64 == 5, bc(f_lo, HEAD_DIM), 0.0))))
        sl = slice(h * HEAD_DIM, (h + 1) * HEAD_DIM)
        qfT_ref[0, h, 0:HEAD_DIM, :] = qbT[sl, :].astype(BF16)
        qfT_ref[0, h, HEAD_DIM:2 * HEAD_DIM, :] = qe.astype(BF16)
        qsT_ref[0, h, 0:HEAD_DIM, :] = qaT[sl, :].astype(BF16)
        qsT_ref[0, h, HEAD_DIM:2 * HEAD_DIM, :] = zeros64
        for j in range(tm // STRIP):
            vsT_ref[0, h, j] = vaT[sl, j * STRIP:(j + 1) * STRIP].astype(BF16)
            vfT_ref[0, h, j] = vbT[sl, j * STRIP:(j + 1) * STRIP].astype(BF16)
        ksl_ref[0, h] = kaT[sl, :]
        vsl_ref[0, h] = vaT[sl, :]
        kfl_ref[0, h] = kbT[sl, :]
        vfl_ref[0, h] = vbT[sl, :]
        ke_parts.append(
            jnp.where(row8 == 0, -bc(f_hi, 8),
                      jnp.where(row8 == 1, -bc(f_mid, 8),
                                jnp.where(row8 == 2, -bc(f_lo, 8),
                                          jnp.where(row8 < 6, 1.0, 0.0)))))
    ke_parts.append(jnp.zeros((LANES - 8 * n_heads, tm), F32))
    ke = jnp.concatenate(ke_parts, axis=0).T

    lane = lax.broadcasted_iota(jnp.int32, (tm, LANES), 1)
    for h in range(n_heads):
        base = (h // 2) * LANES
        ka_slab = ka[:, base:base + LANES]
        kb_slab = kb[:, base:base + LANES]
        if h % 2:
            ka_slab = pltpu.roll(ka_slab, HEAD_DIM, axis=1)
            kb_slab = pltpu.roll(kb_slab, HEAD_DIM, axis=1)
        ext = pltpu.roll(ke, HEAD_DIM - 8 * h, axis=1)
        ks_ref[0, h] = jnp.where(lane < HEAD_DIM, ka_slab, 0.0).astype(BF16)
        kf_ref[0, h] = jnp.where(lane < HEAD_DIM, kb_slab,
                                 jnp.where(lane < HEAD_DIM + 8, ext, 0.0)).astype(BF16)


def _pre_prompt(x, mods, g_mix, w_in_p, b_f_p, w_gate, b_gate, hsel, n_heads):
    b, t, d = x.shape
    tm = PRE_TILE
    nt = t // tm
    h, hd = n_heads, HEAD_DIM
    const2 = lambda bi, ti: (0, 0)
    head_t = lambda bi, ti: (bi, 0, 0, ti)
    head_s = lambda bi, ti: (bi, 0, ti, 0)
    blk_t = lambda bi, ti: (bi, 0, ti, 0, 0)
    out_shape = (
        jax.ShapeDtypeStruct((b, h, 2 * hd, t), BF16),
        jax.ShapeDtypeStruct((b, h, t, 2 * hd), BF16),
        jax.ShapeDtypeStruct((b, h, t // STRIP, hd, STRIP), BF16),
        jax.ShapeDtypeStruct((b, h, 2 * hd, t), BF16),
        jax.ShapeDtypeStruct((b, h, t, 2 * hd), BF16),
        jax.ShapeDtypeStruct((b, h, t // STRIP, hd, STRIP), BF16),
        jax.ShapeDtypeStruct((b, h, hd, t), F32),
        jax.ShapeDtypeStruct((b, h, hd, t), F32),
        jax.ShapeDtypeStruct((b, h, hd, t), F32),
        jax.ShapeDtypeStruct((b, h, hd, t), F32),
        jax.ShapeDtypeStruct((b, h, t), F32),
        jax.ShapeDtypeStruct((b, h, t), F32),
        jax.ShapeDtypeStruct((b, t, w_gate.shape[1]), BF16),
    )
    out_specs = (
        pl.BlockSpec((1, h, 2 * hd, tm), head_t),
        pl.BlockSpec((1, h, tm, 2 * hd), head_s),
        pl.BlockSpec((1, h, tm // STRIP, hd, STRIP), blk_t),
        pl.BlockSpec((1, h, 2 * hd, tm), head_t),
        pl.BlockSpec((1, h, tm, 2 * hd), head_s),
        pl.BlockSpec((1, h, tm // STRIP, hd, STRIP), blk_t),
        pl.BlockSpec((1, h, hd, tm), head_t),
        pl.BlockSpec((1, h, hd, tm), head_t),
        pl.BlockSpec((1, h, hd, tm), head_t),
        pl.BlockSpec((1, h, hd, tm), head_t),
        pl.BlockSpec((1, h, tm), lambda bi, ti: (bi, 0, ti)),
        pl.BlockSpec((1, h, tm), lambda bi, ti: (bi, 0, ti)),
        pl.BlockSpec((1, tm, w_gate.shape[1]), lambda bi, ti: (bi, ti, 0)),
    )
    in_specs = [
        pl.BlockSpec((1, tm, d), lambda bi, ti: (bi, ti, 0)),
        pl.BlockSpec((1, N_MOD, d), lambda bi, ti: (bi, 0, 0)),
        pl.BlockSpec((1, d), const2),
        pl.BlockSpec(w_in_p.shape, const2, pipeline_mode=pl.Buffered(1)),
        pl.BlockSpec((1, LANES), const2),
        pl.BlockSpec(w_gate.shape, const2, pipeline_mode=pl.Buffered(1)),
        pl.BlockSpec((1, w_gate.shape[1]), const2),
        pl.BlockSpec(hsel.shape, const2),
    ]
    return pl.pallas_call(
        functools.partial(_pre_prompt_kernel, n_heads=n_heads),
        grid=(b, nt),
        in_specs=in_specs,
        out_specs=out_specs,
        out_shape=out_shape,
        scratch_shapes=[pltpu.VMEM((h, LANES), F32)],
        compiler_params=_cparams(("arbitrary", "arbitrary")),
        name="pre_prompt",
    )(x, mods, g_mix, w_in_p, b_f_p, w_gate, b_gate, hsel)


def _pre_sample_kernel(x_ref, mods_ref, gmix_ref, win_ref, bf_ref, wg_ref, bg_ref,
                       proj_ref, logf_ref, gates_ref, *, n_heads):
    w = n_heads * HEAD_DIM
    proj, gates = _pre_core(x_ref[...], mods_ref, gmix_ref, win_ref, wg_ref, bg_ref, True)
    proj_ref[...] = proj
    gates_ref[...] = gates
    logf_ref[...] = _log_sigmoid(proj[:, 6 * w:6 * w + LANES] + bf_ref[...])


def _pre_sample(x, mods_tok, g_mix, w_in_p, b_f_p, w_gate, b_gate, n_heads):
    n = x.shape[0]
    full = lambda a: pl.BlockSpec(a.shape, lambda i: (0,) * a.ndim)
    args = (x, mods_tok, g_mix, w_in_p, b_f_p, w_gate, b_gate)
    out_shape = (jax.ShapeDtypeStruct((n, w_in_p.shape[1]), F32),
                 jax.ShapeDtypeStruct((n, LANES), F32),
                 jax.ShapeDtypeStruct((n, w_gate.shape[1]), BF16))
    return pl.pallas_call(
        functools.partial(_pre_sample_kernel, n_heads=n_heads),
        grid=(1,),
        in_specs=[full(a) for a in args],
        out_specs=tuple(pl.BlockSpec(s.shape, lambda i: (0, 0)) for s in out_shape),
        out_shape=out_shape,
        compiler_params=_cparams(("arbitrary",)),
        name="pre_sample",
    )(*args)


SB_DEPTHS = 3
FOX_DEPTHS = 4


def _attn_parts(qT_ref, k_ref, vT_ref, static_depths):
    assert qT_ref.shape[3] == N_STRIPS * STRIP and vT_ref.shape[4] == STRIP
    nk = vT_ref.shape[2]
    base = pl.program_id(2) * N_STRIPS
    q_strip = lambda s: qT_ref[0, 0, :, s * STRIP:(s + 1) * STRIP]
    k_block = lambda j: k_ref[0, 0, pl.ds(pl.multiple_of(j * STRIP, STRIP), STRIP), :]
    v_block = lambda j: vT_ref[0, 0, j]
    block = lambda s, u: base + s - u
    clip = lambda j: jnp.clip(j, 0, nk - 1)

    def qk(u, z_scr, strips=range(N_STRIPS)):
        for s in strips:
            z_scr[s] = _dot(k_block(clip(block(s, u))), q_strip(s))

    row = lax.broadcasted_iota(jnp.int32, (STRIP, STRIP), 0)
    col = lax.broadcasted_iota(jnp.int32, (STRIP, STRIP), 1)
    n_pairs = (base + N_STRIPS - static_depths + 1) // 2
    return q_strip, k_block, v_block, block, clip, qk, row, col, n_pairs


def _sb_prompt_kernel(qT_ref, k_ref, vT_ref, ntri_ref, o_ref, z_a, z_b, w_a, w_b):
    STATIC_DEPTHS = SB_DEPTHS
    q_strip, k_block, v_block, block, clip, qk, row, col, n_pairs = _attn_parts(
        qT_ref, k_ref, vT_ref, STATIC_DEPTHS)
    diag = row < col
    sign = jnp.uint32(0x80000000)

    def softplus2(z):
        neg_abs = lax.bitcast_convert_type(lax.bitcast_convert_type(z, jnp.uint32) | sign, F32)
        return jnp.maximum(z, 0.0) + jnp.log2(1.0 + jnp.exp2(neg_abs))

    def suffix(sp):
        sp_b = sp.astype(BF16)
        later = _dot(ntri_ref[...], sp_b)
        return later, later[0:1, :] - sp_b[0:1, :].astype(F32)

    def masked(x, s, u):
        if isinstance(u, int) and u == 0:
            return jnp.where(diag, x, 0.0)
        if isinstance(u, int) and u <= s:
            return x
        return jnp.where(block(s, u) >= 0, x, 0.0)

    w_a[...] = jnp.zeros(w_a.shape, BF16)
    w_b[...] = jnp.zeros(w_b.shape, BF16)
    subs = [(s, u) for u in range(STATIC_DEPTHS) for s in range(N_STRIPS)]
    zs = [_dot(k_block(clip(block(s, u))), q_strip(s)) for s, u in subs]
    qk(STATIC_DEPTHS, z_a)
    c = [jnp.zeros((1, STRIP), F32)] * N_STRIPS
    acc = [jnp.zeros((HEAD_DIM, STRIP), F32)] * N_STRIPS
    mids = []
    for (s, u), z in zip(subs, zs):
        sp_full = softplus2(z)
        later, total = suffix(masked(sp_full, s, u))
        mids.append((z - sp_full, later, c[s]))
        c[s] = c[s] + total
    for (s, u), (lsig, later, c_before) in zip(subs, mids):
        wgt = masked(jnp.exp2(lsig + later), s, u)
        acc[s] = acc[s] + _dot(v_block(clip(block(s, u))), wgt.astype(BF16)) * jnp.exp2(c_before)

    def pv(u, w_scr, carry, which, strips=range(N_STRIPS)):
        return tuple((cr[0], cr[1] + _dot(v_block(clip(block(s, u))), w_scr[s]) * cr[2 + which]) + cr[2:]
                     if s in strips else cr for s, cr in enumerate(carry))

    def softplus_phase(z_scr, carry, which, u):
        carry = list(carry)
        mids = []
        for s in range(N_STRIPS):
            c, acc, sa, sb = carry[s]
            z = z_scr[s]
            sp_full = softplus2(z)
            later, total = suffix(masked(sp_full, s, u))
            mids.append((z - sp_full, later))
            scale = jnp.exp2(c)
            carry[s] = (c + total, acc, sa, scale) if which else (c + total, acc, scale, sb)
        return tuple(carry), mids

    def weight_phase(w_scr, mids, u):
        for s, (lsig, later) in enumerate(mids):
            w_scr[s] = masked(jnp.exp2(lsig + later), s, u).astype(BF16)

    def pair(u, carry):
        carry = pv(u - 2, w_a, carry, 0)
        for s in range(N_STRIPS):
            carry = pv(u - 1, w_b, carry, 1, (s,))
            qk(u + 1, z_b, (s,))
        carry, mids_a = softplus_phase(z_a, carry, 0, u)
        qk(u + 2, z_a)
        carry, mids_b = softplus_phase(z_b, carry, 1, u + 1)
        weight_phase(w_a, mids_a, u)
        weight_phase(w_b, mids_b, u + 1)
        return carry

    def alive(carry, u_next):
        left = [jnp.where(block(s, u_next) >= 0, cr[0], NEG_BIG) for s, cr in enumerate(carry)]
        return jnp.max(functools.reduce(jnp.maximum, left)) > DEAD_LOG2

    def body(state):
        i, _, cr = state
        u = STATIC_DEPTHS + 2 * i
        cr = pair(u, cr)
        return i + 1, alive(cr, u + 2), cr

    one = jnp.ones((1, STRIP), F32)
    carry = tuple((c[s], acc[s], one, one) for s in range(N_STRIPS))
    i_end, _, carry = lax.while_loop(lambda st: jnp.logical_and(st[0] < n_pairs, st[1]), body,
                                     (jnp.int32(0), alive(carry, STATIC_DEPTHS), carry))
    last = STATIC_DEPTHS + 2 * i_end - 1
    carry = lax.cond(i_end > 0, lambda cr: pv(last, w_b, pv(last - 1, w_a, cr, 0), 1), lambda cr: cr, carry)
    for s in range(N_STRIPS):
        o_ref[0, 0, :, s * STRIP:(s + 1) * STRIP] = carry[s][1].astype(BF16)


def _fox_prompt_kernel(qT_ref, k_ref, vT_ref, kn2_ref, o_ref, z_a, z_b, p_a, p_b):
    STATIC_DEPTHS = FOX_DEPTHS
    q_strip, k_block, v_block, block, clip, qk, row, col, n_pairs = _attn_parts(
        qT_ref, k_ref, vT_ref, STATIC_DEPTHS)
    diag = row <= col

    def masked(sc, s, u):
        if isinstance(u, int) and u == 0:
            return jnp.where(diag, sc, NEG_BIG)
        if isinstance(u, int) and u <= s:
            return sc
        return jnp.where(block(s, u) >= 0, sc, NEG_BIG)

    p_a[...] = jnp.zeros(p_a.shape, BF16)
    p_b[...] = jnp.zeros(p_b.shape, BF16)
    subs = [(s, u) for u in range(STATIC_DEPTHS) for s in range(N_STRIPS)]
    zs = [_dot(k_block(clip(block(s, u))), q_strip(s)) for s, u in subs]
    qk(STATIC_DEPTHS, z_a)
    m = [jnp.full((1, STRIP), NEG_BIG, F32)] * N_STRIPS
    l = [jnp.zeros((1, STRIP), F32)] * N_STRIPS
    acc = [jnp.zeros((HEAD_DIM, STRIP), F32)] * N_STRIPS
    for (s, u), sc in zip(subs, zs):
        sc = masked(sc, s, u)
        m_new = jnp.maximum(m[s], jnp.max(sc, axis=0, keepdims=True))
        alpha = jnp.exp2(m[s] - m_new)
        p = jnp.exp2(sc - m_new)
        l[s] = alpha * l[s] + jnp.sum(p, axis=0, keepdims=True)
        acc[s] = alpha * acc[s] + _dot(v_block(clip(block(s, u))), p.astype(BF16))
        m[s] = m_new

    def pv(u, p_scr, carry, which, strips=range(N_STRIPS)):
        return tuple((cr[0], cr[1], cr[3 + which] * cr[2] + _dot(v_block(clip(block(s, u))), p_scr[s])) + cr[3:]
                     if s in strips else cr for s, cr in enumerate(carry))

    def valu(z_scr, p_scr, carry, which, u):
        out = []
        for s in range(N_STRIPS):
            m, l, acc, aa, ab = carry[s]
            m_new = jnp.maximum(m, jnp.max(masked(z_scr[s], s, u), axis=0, keepdims=True))
            alpha = jnp.exp2(m - m_new)
            p = jnp.exp2(masked(z_scr[s], s, u) - m_new)
            p_scr[s] = p.astype(BF16)
            l = alpha * l + jnp.sum(p, axis=0, keepdims=True)
            out.append((m_new, l, acc, aa, alpha) if which else (m_new, l, acc, alpha, ab))
        return tuple(out)

    def pair(u, carry):
        carry = pv(u - 2, p_a, carry, 0)
        carry = pv(u - 1, p_b, carry, 1)
        qk(u + 1, z_b)
        carry = valu(z_a, p_a, carry, 0, u)
        qk(u + 2, z_a)
        return valu(z_b, p_b, carry, 1, u + 1)

    q_all = qT_ref[0, 0]
    q_f = q_all[0:HEAD_DIM, :].astype(F32)
    q_norm = jnp.sqrt(jnp.sum(q_f * q_f, axis=0, keepdims=True))
    f_rows = q_all[HEAD_DIM:HEAD_DIM + 16, :].astype(F32)
    f_q = f_rows[3:4, :] + f_rows[4:5, :] + f_rows[5:6, :]
    k_norm = jnp.sqrt(jnp.max(kn2_ref[0, 0], axis=1, keepdims=True))
    bound_q = NORM_SLACK * q_norm * k_norm + f_q

    def alive(carry, u_next):
        left = []
        for s in range(N_STRIPS):
            j = block(s, u_next)
            rows = k_ref[0, 0, pl.ds(pl.multiple_of(clip(j) * STRIP + STRIP - 16, 16), 16), :].astype(F32)
            f_end = -(rows[15:16, HEAD_DIM:HEAD_DIM + 1] + rows[15:16, HEAD_DIM + 1:HEAD_DIM + 2]
                      + rows[15:16, HEAD_DIM + 2:HEAD_DIM + 3])
            gap = bound_q[:, s * STRIP:(s + 1) * STRIP] - f_end - carry[s][0]
            left.append(jnp.where(j >= 0, gap, NEG_BIG))
        return jnp.max(functools.reduce(jnp.maximum, left)) > DEAD_LOG2

    def body(state):
        i, _, cr = state
        u = STATIC_DEPTHS + 2 * i
        cr = pair(u, cr)
        return i + 1, alive(cr, u + 2), cr

    one = jnp.ones((1, STRIP), F32)
    carry = tuple((m[s], l[s], acc[s], one, one) for s in range(N_STRIPS))
    i_end, _, carry = lax.while_loop(lambda st: jnp.logical_and(st[0] < n_pairs, st[1]), body,
                                     (jnp.int32(0), alive(carry, STATIC_DEPTHS), carry))
    last = STATIC_DEPTHS + 2 * i_end - 1
    carry = lax.cond(i_end > 0, lambda cr: pv(last, p_b, pv(last - 1, p_a, cr, 0), 1), lambda cr: cr, carry)
    for s in range(N_STRIPS):
        _, l, acc, _, _ = carry[s]
        o_ref[0, 0, :, s * STRIP:(s + 1) * STRIP] = (acc / l).astype(BF16)


def _attn_prompt(kernel, qT, k, vT, extra=()):
    b, h, kd, t = qT.shape
    nk, tk = vT.shape[2], vT.shape[4]
    tq = Q_TILE
    in_specs = [
        pl.BlockSpec((1, 1, kd, tq), lambda bi, hi, qi: (bi, hi, 0, qi)),
        pl.BlockSpec((1, 1, t, kd), lambda bi, hi, qi: (bi, hi, 0, 0)),
        pl.BlockSpec((1, 1, nk, HEAD_DIM, tk), lambda bi, hi, qi: (bi, hi, 0, 0, 0)),
    ] + [pl.BlockSpec(a.shape, lambda bi, hi, qi: (0, 0)) if a.ndim == 2 else
         pl.BlockSpec((1, 1) + a.shape[2:], lambda bi, hi, qi: (bi, hi, 0, 0)) for a in extra]
    scores = pltpu.VMEM((N_STRIPS, STRIP, STRIP), F32)
    probs = pltpu.VMEM((N_STRIPS, STRIP, STRIP), BF16)
    return pl.pallas_call(
        kernel,
        grid=(b, h, t // tq),
        in_specs=in_specs,
        out_specs=pl.BlockSpec((1, 1, HEAD_DIM, tq), lambda bi, hi, qi: (bi, hi, 0, qi)),
        out_shape=jax.ShapeDtypeStruct((b, h, HEAD_DIM, t), BF16),
        scratch_shapes=[scores, scores, probs, probs],
        compiler_params=_cparams(("parallel", "parallel", "arbitrary")),
        name=kernel.__name__.strip("_"),
    )(qT, k, vT, *extra)


def _sample_kv(ck_ref, kn_ref, cv_ref, vn_ref, h):
    kT = jnp.concatenate([ck_ref[0, h], kn_ref[0, h]], axis=1).astype(BF16)
    vT = jnp.concatenate([cv_ref[0, h], vn_ref[0, h]], axis=1).astype(BF16)
    return kT, vT


def _pv(p, vT):
    return lax.dot_general(p, vT, (((1,), (1,)), ((), ())), preferred_element_type=F32)


def _sb_sample_kernel(q_ref, ck_ref, kn_ref, cv_ref, vn_ref, tri_ref, o_ref):
    n_heads, tq = q_ref.shape[1], q_ref.shape[2]
    p_len = ck_ref.shape[3]
    nk = p_len + kn_ref.shape[3]
    row = lax.broadcasted_iota(jnp.int32, (tq, nk), 0)
    col = lax.broadcasted_iota(jnp.int32, (tq, nk), 1)
    mask = col < row + p_len
    heads = []
    for h in range(n_heads):
        kT, vT = _sample_kv(ck_ref, kn_ref, cv_ref, vn_ref, h)
        q = (q_ref[0, h] * HEAD_DIM ** -0.5).astype(BF16)
        z = _dot(q, kT)
        ls_full = -(jnp.maximum(z, 0.0) + jnp.log1p(jnp.exp(-jnp.abs(z))))
        heads.append((z + ls_full, jnp.where(mask, ls_full, 0.0).astype(BF16), vT))
    later = _dot(jnp.concatenate([ls for _, ls, _ in heads], axis=0), tri_ref[...])
    for h, (lsig, _, vT) in enumerate(heads):
        wgt = jnp.where(mask, jnp.exp(lsig + later[h * tq:(h + 1) * tq, :]), 0.0)
        o_ref[0, :, h * HEAD_DIM:(h + 1) * HEAD_DIM] = _pv(wgt.astype(BF16), vT)


def _fox_sample_kernel(q_ref, ck_ref, kn_ref, cv_ref, vn_ref, lf_ref, o_ref):
    n_heads, tq = q_ref.shape[1], q_ref.shape[2]
    p_len = ck_ref.shape[3]
    nk = p_len + kn_ref.shape[3]
    row = lax.broadcasted_iota(jnp.int32, (tq, nk), 0)
    col = lax.broadcasted_iota(jnp.int32, (tq, nk), 1)
    mask = col <= row + p_len
    diag = col == row + p_len
    f_all = _lane_cumsum(lf_ref[0])
    kvs = [_sample_kv(ck_ref, kn_ref, cv_ref, vn_ref, h) for h in range(n_heads)]
    zs = [_dot((q_ref[0, h] * HEAD_DIM ** -0.5).astype(BF16), kvs[h][0]) for h in range(n_heads)]
    ps = []
    for h in range(n_heads):
        fk = f_all[h:h + 1, :]
        fq = jnp.sum(jnp.where(diag, fk, 0.0), axis=1, keepdims=True)
        z = jnp.where(mask, zs[h] + fq - fk, NEG_BIG)
        p = jnp.exp(z - jnp.max(z, axis=1, keepdims=True))
        ps.append((p.astype(BF16), jnp.sum(p, axis=1, keepdims=True)))
    for h, (p, l) in enumerate(ps):
        o_ref[0, :, h * HEAD_DIM:(h + 1) * HEAD_DIM] = _pv(p, kvs[h][1]) / l


def _attn_sample(kernel, q, ck, kn, cv, vn, extra, extra_spec):
    b, h, tq, hd = q.shape
    per_b = lambda a: pl.BlockSpec((1,) + a.shape[1:], lambda bi: (bi,) + (0,) * (a.ndim - 1))
    return pl.pallas_call(
        kernel,
        grid=(b,),
        in_specs=[per_b(a) for a in (q, ck, kn, cv, vn)] + [extra_spec],
        out_specs=pl.BlockSpec((1, tq, h * hd), lambda bi: (bi, 0, 0)),
        out_shape=jax.ShapeDtypeStruct((b, tq, h * hd), F32),
        compiler_params=_cparams(("parallel",)),
        name=kernel.__name__.strip("_"),
    )(q, ck, kn, cv, vn, extra)


def _route(lt, n_groups, epg):
    n = lt.shape[1]
    g = [lt[i:i + 1, :] for i in range(n_groups)]
    gmax = functools.reduce(jnp.maximum, g)
    g_w = 1.0 / functools.reduce(jnp.add, [jnp.exp(gi - gmax) for gi in g])
    is_g, taken = [], None
    for gi in g:
        hit = gi >= gmax
        if taken is not None:
            hit = jnp.logical_and(hit, jnp.logical_not(taken))
        taken = hit if taken is None else jnp.logical_or(taken, hit)
        is_g.append(hit)
    le = [lt[n_groups + i:n_groups + i + 1, :] for i in range(n_groups * epg)]
    e_sel = []
    for i in range(epg):
        v = le[(n_groups - 1) * epg + i]
        for gi in reversed(range(n_groups - 1)):
            v = jnp.where(is_g[gi], le[gi * epg + i], v)
        e_sel.append(v)
    emax = functools.reduce(jnp.maximum, e_sel)
    pe = [jnp.exp(v - emax) for v in e_sel]
    pden = functools.reduce(jnp.add, pe)
    prob = [p / pden for p in pe]

    def first_argmax(vals):
        vmax = functools.reduce(jnp.maximum, vals)
        hits, tk = [], None
        for v in vals:
            hit = v >= vmax
            if tk is not None:
                hit = jnp.logical_and(hit, jnp.logical_not(tk))
            tk = hit if tk is None else jnp.logical_or(tk, hit)
            hits.append(hit)
        return vmax, hits

    p1, t1 = first_argmax(prob)
    p2, t2 = first_argmax([jnp.where(t, -1.0, p) for t, p in zip(t1, prob)])
    tot = p1 + p2
    w1 = g_w * (p1 / tot)
    w2 = g_w * (p2 / tot)
    rows = lax.broadcasted_iota(jnp.int32, (LANES, n), 0)
    comb = jnp.zeros((LANES, n), F32)
    for gi in range(n_groups):
        for i in range(epg):
            val = jnp.where(is_g[gi], jnp.where(t1[i], w1, 0.0) + jnp.where(t2[i], w2, 0.0), 0.0)
            comb = jnp.where(rows == gi * epg + i, jnp.broadcast_to(val, (LANES, n)), comb)
    return comb


def _post_kernel(yaT_ref, ybT_ref, gates_ref, x_ref, mods_ref, wba_ref, wbb_ref, wout_ref,
                 gmoe_ref, wr2_ref, wrhi_ref, br_ref, x1_ref, h2_ref, comb_ref,
                 *, per_token, n_groups, epg):
    d = x_ref.shape[2]
    tdot = lambda aT, w: lax.dot_general(aT, w, (((0,), (0,)), ((), ())), preferred_element_type=F32)
    ua = tdot(yaT_ref[0], wba_ref[...])
    ub = tdot(ybT_ref[0], wbb_ref[...])
    gates = gates_ref[0].astype(F32)
    mix = gates[:, :d] * ua + gates[:, d:] * ub
    x1 = x_ref[0] + _mod(mods_ref, 2, per_token) * _dot(mix.astype(BF16), wout_ref[...])
    x1_ref[0] = x1
    h2 = _rms_mod(x1, gmoe_ref[...], _mod(mods_ref, 4, per_token), _mod(mods_ref, 3, per_token))
    h2_hi = h2.astype(BF16)
    h2_ref[0] = h2_hi
    h2_lo = (h2 - h2_hi.astype(F32)).astype(BF16)
    a = _dot(h2_hi, wr2_ref[...])
    logits = a[:, :LANES] + a[:, LANES:] + _dot(h2_lo, wrhi_ref[...]) + br_ref[...]
    comb_ref[0] = _route(logits.T, n_groups, epg).T


def _post(yaT, ybT, gates, x, mods, w_ba, w_bb, w_out, g_moe, wr2, wrhi, br, per_token, n_groups, epg):
    b, t, d = x.shape
    tm = min(POST_TILE, t)
    w = yaT.shape[1]
    const2 = lambda bi, ti: (0, 0)
    tok = lambda last: pl.BlockSpec((1, tm, last), lambda bi, ti: (bi, ti, 0))
    chan = pl.BlockSpec((1, w, tm), lambda bi, ti: (bi, 0, ti))
    if per_token:
        mods_spec = pl.BlockSpec(mods.shape, lambda bi, ti: (0, 0, 0))
    else:
        mods_spec = pl.BlockSpec((1, N_MOD, d), lambda bi, ti: (bi, 0, 0))
    in_specs = [chan, chan, tok(2 * d), tok(d), mods_spec,
                pl.BlockSpec(w_ba.shape, const2), pl.BlockSpec(w_bb.shape, const2),
                pl.BlockSpec(w_out.shape, const2), pl.BlockSpec((1, d), const2),
                pl.BlockSpec(wr2.shape, const2), pl.BlockSpec(wrhi.shape, const2),
                pl.BlockSpec((1, LANES), const2)]
    out_shape = (jax.ShapeDtypeStruct((b, t, d), F32),
                 jax.ShapeDtypeStruct((b, t, d), BF16),
                 jax.ShapeDtypeStruct((b, t, LANES), F32))
    return pl.pallas_call(
        functools.partial(_post_kernel, per_token=per_token, n_groups=n_groups, epg=epg),
        grid=(b, t // tm),
        in_specs=in_specs,
        out_specs=(tok(d), tok(d), tok(LANES)),
        out_shape=out_shape,
        compiler_params=_cparams(("parallel", "parallel")),
        name="post_sample" if per_token else "post_prompt",
    )(yaT, ybT, gates, x, mods, w_ba, w_bb, w_out, g_moe, wr2, wrhi, br)


def _moe_kernel(h2_ref, comb_ref, x1_ref, mods_ref, w13_ref, w2_ref, gfin_ref, y_ref, acc_ref,
                *, per_token, epg, hid):
    g = pl.program_id(1)

    @pl.when(g == 0)
    def _():
        acc_ref[...] = jnp.zeros_like(acc_ref)

    h2 = h2_ref[...]
    comb = comb_ref[...]
    lane = lax.broadcasted_iota(jnp.int32, comb.shape, 1)
    ups = [_dot(h2, w13_ref[0, i]) for i in range(epg)]
    total = None
    for c in range(0, epg, 2):
        acts = []
        for i in range(c, c + 2):
            a1 = ups[i][:, :hid]
            a3 = ups[i][:, hid:]
            cw = jnp.sum(jnp.where(lane == g * epg + i, comb, 0.0), axis=1, keepdims=True)
            acts.append(((a1 * jax.nn.sigmoid(a1)) * a3 * cw).astype(BF16))
        part = _dot(jnp.concatenate(acts, axis=1), w2_ref[0, c * hid:(c + 2) * hid, :])
        total = part if total is None else total + part
    acc_ref[...] += total

    @pl.when(g == pl.num_programs(1) - 1)
    def _():
        x2 = x1_ref[...] + _mod(mods_ref, 5, per_token) * acc_ref[...]
        ms = jnp.mean(x2 * x2, axis=-1, keepdims=True)
        y_ref[...] = (x2 * lax.rsqrt(ms + RMS_EPS)) * gfin_ref[...]


def _moe(h2, comb, x1, mods, w13, w2, g_final, per_token, tokens_per_batch, epg):
    n, d = x1.shape
    tm = min(MOE_TILE, n)
    n_groups = w13.shape[0]
    hid = w2.shape[1] // epg
    assert epg % 2 == 0
    tiles_per_batch = tokens_per_batch // tm if not per_token else 1
    tok = lambda last: pl.BlockSpec((tm, last), lambda i, g: (i, 0))
    if per_token:
        mods_spec = pl.BlockSpec(mods.shape, lambda i, g: (0, 0, 0))
    else:
        mods_spec = pl.BlockSpec((1, N_MOD, d), lambda i, g: (i // tiles_per_batch, 0, 0))
    return pl.pallas_call(
        functools.partial(_moe_kernel, per_token=per_token, epg=epg, hid=hid),
        grid=(n // tm, n_groups),
        in_specs=[tok(d), tok(LANES), tok(d), mods_spec,
                  pl.BlockSpec((1,) + w13.shape[1:], lambda i, g: (g, 0, 0, 0)),
                  pl.BlockSpec((1,) + w2.shape[1:], lambda i, g: (g, 0, 0)),
                  pl.BlockSpec((1, d), lambda i, g: (0, 0))],
        out_specs=tok(d),
        out_shape=jax.ShapeDtypeStruct((n, d), F32),
        scratch_shapes=[pltpu.VMEM((tm, d), F32)],
        compiler_params=_cparams(("parallel", "arbitrary")),
        name="moe_sample" if per_token else "moe_prompt",
    )(h2, comb, x1, mods, w13, w2, g_final)


def kernel(x_prompt, x_sample, cache_sb_k, cache_sb_v, cache_fox_k, cache_fox_v, cache_fox_logf,
           c_prompt, c_sample, w_ada, b_ada, g_mix, w_in, b_f, w_ba, w_bb, w_gate, b_gate, w_out,
           g_moe, w_rg, b_rg, w_re, b_re, w1, w3, w2, g_final):
    depth = w_ada.shape[0]
    assert depth == 1, "single-layer trunk"
    bp, t, d = x_prompt.shape
    bs, ts, _ = x_sample.shape
    n_heads = cache_sb_k.shape[2]
    p_len = cache_sb_k.shape[3]
    assert cache_fox_k.shape[2] == n_heads and n_heads * HEAD_DIM * 6 + n_heads == w_in.shape[2]
    assert n_heads == 8 and t % Q_TILE == 0 and Q_TILE % PRE_TILE == 0
    n_groups = w_rg.shape[2]
    n_exp = w_re.shape[2]
    epg = n_exp // n_groups
    assert n_groups + n_exp <= LANES
    w = n_heads * HEAD_DIM
    ns = bs * ts

    w_in_p = jnp.pad(w_in[0], ((0, 0), (0, LANES - n_heads))).astype(BF16)
    b_f_p = jnp.pad(b_f[0], (0, LANES - n_heads)).reshape(1, LANES)
    w_gate_b = w_gate[0].astype(BF16)
    b_gate_r = b_gate[0].reshape(1, -1)
    g_mix_r = g_mix[0].reshape(1, d)
    g_moe_r = g_moe[0].reshape(1, d)
    g_fin_r = g_final.reshape(1, d)
    w_ba_b, w_bb_b, w_out_b = w_ba[0].astype(BF16), w_bb[0].astype(BF16), w_out[0].astype(BF16)
    w_r = jnp.pad(jnp.concatenate([w_rg[0], w_re[0]], axis=1), ((0, 0), (0, LANES - n_groups - n_exp)))
    w_r_hi = w_r.astype(BF16)
    w_r_lo = (w_r - w_r_hi.astype(F32)).astype(BF16)
    wr2 = jnp.concatenate([w_r_hi, w_r_lo], axis=1)
    b_r = jnp.pad(jnp.concatenate([b_rg[0], b_re[0]]), (0, LANES - n_groups - n_exp)).reshape(1, LANES)
    hid = w1.shape[3]
    w13 = jnp.concatenate([w1[0], w3[0]], axis=2).astype(BF16).reshape(n_groups, epg, d, 2 * hid)
    w2_b = w2[0].astype(BF16).reshape(n_groups, epg * hid, d)

    ada = _ada(jnp.concatenate([c_prompt, c_sample], axis=0), w_ada[0], b_ada[0])
    mods_p = ada[:bp].reshape(bp, N_MOD, d)
    mods_s = jnp.repeat(ada[bp:].reshape(bs, N_MOD, d).transpose(1, 0, 2), ts, axis=1)

    hsel = jnp.repeat(jnp.eye(n_heads, dtype=BF16), HEAD_DIM, axis=1)
    (qsT, ks, vsT, qfT, kf, vfT, ksl, vsl, kfl, vfl, logf_p, kn2, gates_p) = _pre_prompt(
        x_prompt, mods_p, g_mix_r, w_in_p, b_f_p, w_gate_b, b_gate_r, hsel, n_heads)
    tk = STRIP
    ids = jnp.arange(tk)
    ntri_p = -(ids[None, :] > ids[:, None]).astype(BF16)
    yaT = _attn_prompt(_sb_prompt_kernel, qsT, ks, vsT, (ntri_p,)).reshape(bp, w, t)
    ybT = _attn_prompt(_fox_prompt_kernel, qfT, kf, vfT, (kn2[:, :, None, :],)).reshape(bp, w, t)
    x1_p, h2_p, comb_p = _post(yaT, ybT, gates_p, x_prompt, mods_p, w_ba_b, w_bb_b, w_out_b,
                               g_moe_r, wr2, w_r_hi, b_r, False, n_groups, epg)
    y_prompt = _moe(h2_p.reshape(bp * t, d), comb_p.reshape(bp * t, LANES), x1_p.reshape(bp * t, d),
                    mods_p, w13, w2_b, g_fin_r, False, t, epg).reshape(bp, t, d)

    proj_s, logf_s, gates_s = _pre_sample(x_sample.reshape(ns, d), mods_s, g_mix_r, w_in_p, b_f_p,
                                          w_gate_b, b_gate_r, n_heads)
    heads = lambda i: proj_s[:, i * w:(i + 1) * w].reshape(bs, ts, n_heads, HEAD_DIM).transpose(0, 2, 1, 3)
    qa_s, ka_s, va_s, qb_s, kb_s, vb_s = [heads(i) for i in range(6)]
    lf_s = logf_s[:, :n_heads].reshape(bs, ts, n_heads).transpose(0, 2, 1)
    pad_k = lambda a: jnp.pad(jnp.swapaxes(a, 2, 3), ((0, 0), (0, 0), (0, 0), (0, LANES - ts)))
    chan = lambda cache: jnp.swapaxes(cache[0], 2, 3)
    nk = p_len + LANES
    ids = jnp.arange(nk)
    tri_s = (ids[:, None] > ids[None, :]).astype(BF16)
    lf_all = jnp.concatenate([cache_fox_logf[0], jnp.pad(lf_s, ((0, 0), (0, 0), (0, LANES - ts)))], axis=2)
    ya_s = _attn_sample(_sb_sample_kernel, qa_s, chan(cache_sb_k), pad_k(ka_s), chan(cache_sb_v), pad_k(va_s),
                        tri_s, pl.BlockSpec(tri_s.shape, lambda bi: (0, 0)))
    yb_s = _attn_sample(_fox_sample_kernel, qb_s, chan(cache_fox_k), pad_k(kb_s), chan(cache_fox_v), pad_k(vb_s),
                        lf_all, pl.BlockSpec((1, n_heads, nk), lambda bi: (bi, 0, 0)))
    to_chan = lambda y: y.reshape(ns, w).T.astype(BF16)[None]
    x1_s, h2_s, comb_s = _post(to_chan(ya_s), to_chan(yb_s), gates_s[None], x_sample.reshape(1, ns, d),
                               mods_s, w_ba_b, w_bb_b, w_out_b, g_moe_r, wr2, w_r_hi, b_r,
                               True, n_groups, epg)
    y_sample = _moe(h2_s[0], comb_s[0], x1_s[0], mods_s, w13, w2_b, g_fin_r, True, ns, epg).reshape(bs, ts, d)

    lead = lambda a: a[None]
    tok_major = lambda a: jnp.swapaxes(a, 2, 3)[None]
    return (y_prompt, y_sample,
            tok_major(ksl), tok_major(vsl), tok_major(kfl), tok_major(vfl), lead(logf_p),
            lead(ka_s), lead(va_s), lead(kb_s), lead(vb_s), lead(lf_s))
```
